```python
import math
import jax, jax.numpy as jnp
from jax import lax
import numpy as np

D_MODEL = 1024
BATCH = 8
SEQ = 4096
DEPTH = 2

EXPAND = 2
D_MIX = EXPAND * D_MODEL
N_GROUPS = 4
GROUP_W = D_MIX // N_GROUPS
HEAD_DIM = 64
H_A = GROUP_W // (2 * HEAD_DIM)
DV_A = 2 * HEAD_DIM
H_B = GROUP_W // HEAD_DIM
H_C = GROUP_W // HEAD_DIM
H_D = GROUP_W // HEAD_DIM
RWKV_DECAY_RANK = 32
RWKV_ICLR_RANK = 32
RWKV_SHIFT_W = 3 * GROUP_W + RWKV_DECAY_RANK + RWKV_ICLR_RANK
NUM_BUCKETS = 32
MAX_DISTANCE = 128
Q_BLOCK = 128
NORM_EPS = 1e-6
RWKV_LN_EPS = 64e-5
NEG_INF = -1e30
IN_WIDTHS = (GROUP_W, GROUP_W, GROUP_W, GROUP_W,
             GROUP_W, GROUP_W, GROUP_W, GROUP_W,
             GROUP_W, GROUP_W, GROUP_W, GROUP_W, H_C,
             RWKV_SHIFT_W, GROUP_W)
N_IN = sum(IN_WIDTHS)

kernel_name = "hymba_style_diff_sb_fox_rwkv7_hybrid"


def _split(a, widths):
    outs, off = [], 0
    for w in widths:
        outs.append(a[..., off:off + w])
        off += w
    return outs


def rms_norm(x, gain, eps=NORM_EPS):
    xf = x.astype(jnp.float32)
    y = xf * lax.rsqrt(jnp.mean(xf * xf, axis=-1, keepdims=True) + eps)
    return (y * gain.astype(jnp.float32)).astype(x.dtype)


def t5_causal_bucket(dist):
    max_exact = NUM_BUCKETS // 2
    d = jnp.maximum(dist, 1).astype(jnp.float32)
    large = max_exact + (jnp.log(d / max_exact) / math.log(MAX_DISTANCE / max_exact)
                         * (NUM_BUCKETS - max_exact)).astype(jnp.int32)
    large = jnp.minimum(large, NUM_BUCKETS - 1)
    return jnp.where(dist < max_exact, dist, large)


def _blocks(a):
    b, h, s = a.shape[:3]
    a = a.reshape((b, h, s // Q_BLOCK, Q_BLOCK) + a.shape[3:])
    return jnp.moveaxis(a, 2, 0)


def _unblocks(o):
    nb, b, h, qb, dv = o.shape
    return jnp.moveaxis(o, 0, 2).reshape(b, h, nb * qb, dv)


def differential_attention(q, k, v, bias_by_dist, lam):
    seq = q.shape[2]
    pos = jnp.arange(seq)
    scale = q.shape[-1] ** -0.5

    def block(args):
        qb, tb = args
        dist = tb[:, None] - pos[None, :]
        bias = jnp.transpose(bias_by_dist[jnp.clip(dist, 0, seq - 1)], (2, 0, 1)).astype(jnp.float32)
        s = jnp.einsum('bhqcd,bhkcd->bhcqk', qb, k).astype(jnp.float32) * scale + bias[None, :, None]
        s = jnp.where(dist >= 0, s, NEG_INF)
        p = jax.nn.softmax(s, axis=-1)
        w = p[:, :, 0] - lam * p[:, :, 1]
        return jnp.einsum('bhqk,bhkd->bhqd', w.astype(v.dtype), v)

    return _unblocks(lax.map(block, (_blocks(q), pos.reshape(-1, Q_BLOCK))))


def stick_breaking_attention(q, k, v):
    seq = q.shape[2]
    pos = jnp.arange(seq)
    scale = q.shape[-1] ** -0.5

    def block(args):
        qb, tb = args
        z = jnp.einsum('bhqd,bhkd->bhqk', qb, k).astype(jnp.float32) * scale
        causal = tb[:, None] > pos[None, :]
        log_beta = jax.nn.log_sigmoid(z)
        log_1mb = jnp.where(causal, log_beta - z, 0.0)
        after = lax.cumsum(log_1mb, axis=z.ndim - 1, reverse=True) - log_1mb
        a = jnp.where(causal, jnp.exp(log_beta + after), 0.0)
        return jnp.einsum('bhqk,bhkd->bhqd', a.astype(v.dtype), v)

    return _unblocks(lax.map(block, (_blocks(q), pos.reshape(-1, Q_BLOCK))))


def forgetting_attention(q, k, v, log_f):
    seq = q.shape[2]
    pos = jnp.arange(seq)
    scale = q.shape[-1] ** -0.5
    cum_f = jnp.cumsum(log_f, axis=-1)

    def block(args):
        qb, fb, tb = args
        s = jnp.einsum('bhqd,bhkd->bhqk', qb, k).astype(jnp.float32) * scale
        s = s + (fb[..., :, None] - cum_f[..., None, :])
        s = jnp.where(tb[:, None] >= pos[None, :], s, NEG_INF)
        p = jax.nn.softmax(s, axis=-1)
        return jnp.einsum('bhqk,bhkd->bhqd', p.astype(v.dtype), v)

    return _unblocks(lax.map(block, (_blocks(q), _blocks(cum_f), pos.reshape(-1, Q_BLOCK))))


def rwkv7_scan(r, decay, k, v, kk, b):
    bsz, _, h, n = r.shape

    def step(state, inp):
        r_t, w_t, k_t, v_t, kk_t, b_t = inp
        sa = jnp.einsum('bhij,bhj->bhi', state, -kk_t)
        state = (state * w_t[:, :, None, :] + sa[..., None] * b_t[:, :, None, :]
                 + v_t[..., None] * k_t[:, :, None, :])
        return state, jnp.einsum('bhij,bhj->bhi', state, r_t)

    xs = tuple(jnp.moveaxis(u, 1, 0) for u in (r, decay, k, v, kk, b))
    _, y = lax.scan(step, jnp.zeros((bsz, h, n, n), jnp.float32), xs)
    return jnp.moveaxis(y, 0, 1)


def rwkv7_branch(u, mu, w_up, w0, a_up, a0, kkr, ln_gain, ln_bias):
    bsz, seq, _ = u.shape
    f32 = jnp.float32
    u_prev = jnp.pad(u, ((0, 0), (1, 0), (0, 0)))[:, :seq]
    u = u + (u_prev - u) * mu
    r, k, v, w_lo, a_lo = _split(u, (GROUP_W, GROUP_W, GROUP_W, RWKV_DECAY_RANK, RWKV_ICLR_RANK))
    w_log = -jax.nn.softplus(-(w0 + jnp.tanh(w_lo) @ w_up).astype(f32)) - 0.5
    decay = jnp.exp(-jnp.exp(w_log))
    a = jax.nn.sigmoid((a0 + a_lo @ a_up).astype(f32))
    k_k, k_a, r_k = kkr[0].astype(f32), kkr[1].astype(f32), kkr[2].astype(f32)
    r, k, v = r.astype(f32), k.astype(f32), v.astype(f32)
    heads = lambda t: t.reshape(bsz, seq, H_D, HEAD_DIM)
    kk = heads(k * k_k)
    kk = kk / jnp.maximum(jnp.sqrt(jnp.sum(kk * kk, axis=-1, keepdims=True)), 1e-12)
    k = k * (1.0 + (a - 1.0) * k_a)
    rh, kh, vh, ah = heads(r), heads(k), heads(v), heads(a)
    y = rwkv7_scan(rh, heads(decay), kh, vh, kk, kk * ah)
    mean = jnp.mean(y, axis=-1, keepdims=True)
    var = jnp.mean(jnp.square(y - mean), axis=-1, keepdims=True)
    y = ((y - mean) * lax.rsqrt(var + RWKV_LN_EPS) * ln_gain.astype(f32).reshape(H_D, HEAD_DIM)
         + ln_bias.astype(f32).reshape(H_D, HEAD_DIM))
    y = y + jnp.sum(rh * kh * r_k.reshape(H_D, HEAD_DIM), axis=-1, keepdims=True) * vh
    return y.reshape(bsz, seq, GROUP_W).astype(u.dtype)


def setup_inputs(seed: int = 0) -> dict:
    key = jax.random.key(seed)
    ks = jax.random.split(key, 17)
    f32 = jnp.float32
    nrm = lambda k_, shape: jax.random.normal(k_, shape, f32)
    decay_ramp = jnp.tile(jnp.linspace(-6.0, -1.0, HEAD_DIM, dtype=f32), H_D)
    return {
        "x": nrm(ks[0], (BATCH, SEQ, D_MODEL)),
        "norm_gain": 1.0 + 0.02 * nrm(ks[1], (DEPTH, D_MODEL)),
        "w_in": nrm(ks[2], (DEPTH, D_MODEL, N_IN)) * D_MODEL ** -0.5,
        "w_out": nrm(ks[3], (DEPTH, D_MIX, D_MODEL)) * D_MIX ** -0.5,
        "rel_bias": 0.1 * nrm(ks[4], (NUM_BUCKETS, H_A)),
        "qk_gain": 1.0 + 0.02 * nrm(ks[5], (DEPTH, 4, HEAD_DIM)),
        "diff_lambda": 0.1 * nrm(ks[6], (DEPTH, 4, HEAD_DIM)),
        "forget_bias": 3.0 + 0.5 * nrm(ks[7], (DEPTH, H_C)),
        "out_gain": 1.0 + 0.02 * nrm(ks[8], (DEPTH, D_MIX)),
        "rwkv_mu": jax.random.uniform(ks[9], (DEPTH, RWKV_SHIFT_W), f32),
        "rwkv_w_up": 0.5 * nrm(ks[10], (DEPTH, RWKV_DECAY_RANK, GROUP_W)) * RWKV_DECAY_RANK ** -0.5,
        "rwkv_w0": decay_ramp[None, :] + 0.1 * nrm(ks[11], (DEPTH, GROUP_W)),
        "rwkv_a_up": 0.5 * nrm(ks[12], (DEPTH, RWKV_ICLR_RANK, GROUP_W)) * RWKV_ICLR_RANK ** -0.5,
        "rwkv_a0": 0.1 * nrm(ks[13], (DEPTH, GROUP_W)),
        "rwkv_kkr": jnp.array([0.85, 1.0, -0.04], f32)[None, :, None] + 0.02 * nrm(ks[14], (DEPTH, 3, GROUP_W)),
        "rwkv_ln_gain": 1.0 + 0.02 * nrm(ks[15], (DEPTH, GROUP_W)),
        "rwkv_ln_bias": 0.02 * nrm(ks[16], (DEPTH, GROUP_W)),
    }


def reference(x, norm_gain, w_in, w_out, rel_bias, qk_gain, diff_lambda, forget_bias, out_gain,
              rwkv_mu, rwkv_w_up, rwkv_w0, rwkv_a_up, rwkv_a0, rwkv_kkr, rwkv_ln_gain, rwkv_ln_bias):
    bsz, seq, _ = x.shape
    f32 = jnp.float32
    bias_by_dist = rel_bias[t5_causal_bucket(jnp.arange(seq))]
    to_heads = lambda t, h, d: t.reshape(bsz, seq, h, d).transpose(0, 2, 1, 3)
    from_heads = lambda t: t.transpose(0, 2, 1, 3)
    h_res = x
    for l in range(DEPTH):
        h = rms_norm(h_res, norm_gain[l])
        proj = h @ w_in[l]
        (aq, ak, av, ag, bq, bk, bv, bg, cq, ck, cv, cg, cf, d_in, dg) = _split(proj, IN_WIDTHS)

        qa = rms_norm(aq.reshape(bsz, seq, H_A, 2, HEAD_DIM), qk_gain[l, 0]).transpose(0, 2, 1, 3, 4)
        ka = rms_norm(ak.reshape(bsz, seq, H_A, 2, HEAD_DIM), qk_gain[l, 1]).transpose(0, 2, 1, 3, 4)
        va = to_heads(av, H_A, DV_A)
        lam_init = 0.8 - 0.6 * math.exp(-0.3 * l)
        dl = diff_lambda[l].astype(f32)
        lam = jnp.exp(jnp.sum(dl[0] * dl[1])) - jnp.exp(jnp.sum(dl[2] * dl[3])) + lam_init
        oa = from_heads(differential_attention(qa, ka, va, bias_by_dist, lam))
        oa = rms_norm(oa, out_gain[l, :GROUP_W].reshape(H_A, DV_A)) * (1.0 - lam_init)
        oa = oa.reshape(bsz, seq, GROUP_W)

        ob = from_heads(stick_breaking_attention(to_heads(bq, H_B, HEAD_DIM), to_heads(bk, H_B, HEAD_DIM),
                                                 to_heads(bv, H_B, HEAD_DIM)))
        ob = rms_norm(ob, out_gain[l, GROUP_W:2 * GROUP_W].reshape(H_B, HEAD_DIM)).reshape(bsz, seq, GROUP_W)

        qc = from_heads(rms_norm(cq.reshape(bsz, seq, H_C, HEAD_DIM), qk_gain[l, 2]))
        kc = from_heads(rms_norm(ck.reshape(bsz, seq, H_C, HEAD_DIM), qk_gain[l, 3]))
        log_f = jax.nn.log_sigmoid((cf + forget_bias[l]).astype(f32)).transpose(0, 2, 1)
        oc = from_heads(forgetting_attention(qc, kc, to_heads(cv, H_C, HEAD_DIM), log_f))
        oc = rms_norm(oc, out_gain[l, 2 * GROUP_W:3 * GROUP_W].reshape(H_C, HEAD_DIM)).reshape(bsz, seq, GROUP_W)

        od = rwkv7_branch(d_in, rwkv_mu[l], rwkv_w_up[l], rwkv_w0[l], rwkv_a_up[l], rwkv_a0[l],
                          rwkv_kkr[l], rwkv_ln_gain[l] * out_gain[l, 3 * GROUP_W:], rwkv_ln_bias[l])

        mixed = jnp.concatenate([oa * jax.nn.silu(ag), ob * jax.nn.silu(bg),
                                 oc * jax.nn.silu(cg), od * jax.nn.silu(dg)], axis=-1)
        h_res = h_res + mixed @ w_out[l]
    return h_res
```

```python
import functools
import math

import numpy as np
import jax
import jax.numpy as jnp
from jax import lax
from jax.experimental import pallas as pl
from jax.experimental.pallas import tpu as pltpu

F32 = jnp.float32
BF16 = jnp.bfloat16
HIGHEST = lax.Precision.HIGHEST

D_MODEL = 1024
DEPTH = 2
GROUP_W = 512
HEAD_DIM = 64
H_A = 4
NUM_BUCKETS = 32
MAX_DISTANCE = 128
RWKV_RANK = 32
NORM_EPS = 1e-6
RWKV_LN_EPS = 64e-5
NEG_INF = -1e30

LANES = 128
N_PAIRS = GROUP_W // LANES
ATT_SCALE = HEAD_DIM ** -0.5

COL_A, COL_B, COL_C = 0, 16, 32
COL_D = 48
COL_DG = 60
COL_SMALL = 64
N_PROJ = (COL_SMALL + 1) * LANES
SMALL_WLO, SMALL_ALO = 32, 64

ATT_TILE = 256
PREP_ROWS = 512
RWKV_CHUNK = 128
VMEM_LIMIT = 48 * 1024 * 1024


def _cparams(sem):
    return pltpu.CompilerParams(dimension_semantics=sem, vmem_limit_bytes=VMEM_LIMIT)


def _dot(a, b, **kw):
    return jnp.dot(a, b, preferred_element_type=F32, **kw)


def _dot_nt(a, b, **kw):
    return lax.dot_general(a, b, (((1,), (1,)), ((), ())), preferred_element_type=F32, **kw)


def _lane_lo(shape):
    return lax.broadcasted_iota(jnp.int32, shape, len(shape) - 1) < HEAD_DIM


def _seg_sum(x):
    lo = _lane_lo(x.shape)
    s_lo = jnp.sum(jnp.where(lo, x, 0.0), axis=-1, keepdims=True)
    s_hi = jnp.sum(jnp.where(lo, 0.0, x), axis=-1, keepdims=True)
    return jnp.where(lo, s_lo, s_hi)


def _seg_rms(x, gain):
    return x * lax.rsqrt(_seg_sum(x * x) * (1.0 / HEAD_DIM) + NORM_EPS) * gain


def _log_sigmoid(z):
    return jnp.minimum(z, 0.0) - jnp.log1p(jnp.exp(-jnp.abs(z)))


def _silu(g):
    return g / (1.0 + jnp.exp(-g))


def _proj_kernel(x_ref, g_ref, w_ref, o_ref, h_ref):
    @pl.when(pl.program_id(1) == 0)
    def _():
        x = x_ref[...]
        ms = jnp.mean(x * x, axis=-1, keepdims=True)
        h_ref[...] = (x * lax.rsqrt(ms + NORM_EPS) * g_ref[...]).astype(BF16)

    o_ref[...] = _dot(h_ref[...], w_ref[...])


def _proj(x2d, gain, w_packed):
    m, d = x2d.shape
    n = w_packed.shape[1]
    tm = min(1024, m)
    tn = n // 5
    return pl.pallas_call(
        _proj_kernel,
        grid=(m // tm, n // tn),
        in_specs=[pl.BlockSpec((tm, d), lambda i, j: (i, 0)),
                  pl.BlockSpec((1, d), lambda i, j: (0, 0)),
                  pl.BlockSpec((d, tn), lambda i, j: (0, j))],
        out_specs=pl.BlockSpec((tm, tn), lambda i, j: (i, j)),
        out_shape=jax.ShapeDtypeStruct((m, n), F32),
        scratch_shapes=[pltpu.VMEM((tm, d), BF16)],
        compiler_params=_cparams(("arbitrary", "arbitrary")),
    )(x2d, gain.reshape(1, d), w_packed)


def _prep_kv(k_ref, v_ref, kbf, vbf, k_gain):
    seq = k_ref.shape[1]
    rows_per = min(PREP_ROWS, seq)

    def body(c, carry):
        rows = pl.ds(pl.multiple_of(c * rows_per, rows_per), rows_per)
        k = k_ref[0, rows, :]
        if k_gain is not None:
            k = _seg_rms(k, k_gain)
        kbf[rows, :] = k.astype(BF16)
        vbf[rows, :] = v_ref[0, rows, :].astype(BF16)
        return carry

    lax.fori_loop(0, seq // rows_per, body, 0)


def _head_masked(q):
    lo = _lane_lo(q.shape)
    return jnp.where(lo, q, 0.0).astype(BF16), jnp.where(lo, 0.0, q).astype(BF16)


def _softmax_tile(s, m, l, acc, vt):
    m_new = jnp.maximum(m, jnp.max(s, axis=-1, keepdims=True))
    alpha = jnp.exp(m - m_new)
    p = jnp.exp(s - m_new)
    l = alpha * l + jnp.sum(p, axis=-1, keepdims=True)
    acc = alpha * acc + _dot(p.astype(BF16), vt)
    return m_new, l, acc


def _softmax_init(t):
    return (jnp.full((t, 1), NEG_INF, F32), jnp.zeros((t, 1), F32), jnp.zeros((t, LANES), F32))


def _attn_a_kernel(q_ref, k_ref, v_ref, g_ref, tz_ref, qg_ref, kg_ref, og_ref, dl_ref, o_ref, kbf, vbf,
                   *, lam_init):
    qi = pl.program_id(2)
    t = q_ref.shape[1]

    @pl.when(qi == 0)
    def _():
        _prep_kv(k_ref, v_ref, kbf, vbf, kg_ref[...])

    q0, q1 = _head_masked(_seg_rms(q_ref[0], qg_ref[...]) * ATT_SCALE)
    row = lax.broadcasted_iota(jnp.int32, (t, t), 0)
    col = lax.broadcasted_iota(jnp.int32, (t, t), 1)

    def tile(j, carry, diag):
        ks = pl.ds(pl.multiple_of(j * t, t), t)
        kt, vt = kbf[ks, :], vbf[ks, :]
        bias = tz_ref[0, jnp.minimum(qi - j, 2)]
        out = []
        for qc, st in ((q0, carry[:3]), (q1, carry[3:])):
            s = _dot_nt(qc, kt) + bias
            if diag:
                s = jnp.where(row >= col, s, NEG_INF)
            out.extend(_softmax_tile(s, *st, vt))
        return tuple(out)

    carry = lax.fori_loop(0, qi, lambda j, c: tile(j, c, False), _softmax_init(t) * 2)
    _, l0, a0, _, l1, a1 = tile(qi, carry, True)

    dl = dl_ref[...]
    lam = (jnp.exp(jnp.sum(dl[0:1] * dl[1:2], axis=-1, keepdims=True))
           - jnp.exp(jnp.sum(dl[2:3] * dl[3:4], axis=-1, keepdims=True)) + lam_init)
    o = a0 / l0 - lam * (a1 / l1)
    y = o * lax.rsqrt(jnp.mean(o * o, axis=-1, keepdims=True) + NORM_EPS) * og_ref[0] * (1.0 - lam_init)
    o_ref[0] = (y * _silu(g_ref[0])).astype(o_ref.dtype)


def _attn_c_kernel(q_ref, k_ref, v_ref, g_ref, f_ref, qg_ref, kg_ref, og_ref, o_ref, kbf, vbf):
    qi = pl.program_id(2)
    t = q_ref.shape[1]

    @pl.when(qi == 0)
    def _():
        _prep_kv(k_ref, v_ref, kbf, vbf, kg_ref[...])

    q0, q1 = _head_masked(_seg_rms(q_ref[0], qg_ref[...]) * ATT_SCALE)
    row = lax.broadcasted_iota(jnp.int32, (t, t), 0)
    col = lax.broadcasted_iota(jnp.int32, (t, t), 1)

    def tile(j, carry, diag):
        ks = pl.ds(pl.multiple_of(j * t, t), t)
        kt, vt = kbf[ks, :], vbf[ks, :]
        out = []
        for h, (qc, st) in enumerate(((q0, carry[:3]), (q1, carry[3:]))):
            s = _dot_nt(qc, kt) - f_ref[0, 0, h:h + 1, ks]
            if diag:
                s = jnp.where(row >= col, s, NEG_INF)
            out.extend(_softmax_tile(s, *st, vt))
        return tuple(out)

    carry = lax.fori_loop(0, qi, lambda j, c: tile(j, c, False), _softmax_init(t) * 2)
    _, l0, a0, _, l1, a1 = tile(qi, carry, True)
    o = jnp.where(_lane_lo(a0.shape), a0 / l0, a1 / l1)
    o_ref[0] = (_seg_rms(o, og_ref[0]) * _silu(g_ref[0])).astype(o_ref.dtype)


def _attn_b_kernel(q_ref, k_ref, v_ref, g_ref, u_ref, og_ref, o_ref, kbf, vbf):
    qi = pl.program_id(2)
    t = q_ref.shape[1]

    @pl.when(qi == 0)
    def _():
        _prep_kv(k_ref, v_ref, kbf, vbf, None)

    q0, q1 = _head_masked(q_ref[0] * ATT_SCALE)
    row = lax.broadcasted_iota(jnp.int32, (t, t), 0)
    col = lax.broadcasted_iota(jnp.int32, (t, t), 1)
    umat = u_ref[...]

    def tile(j, carry, diag):
        ks = pl.ds(pl.multiple_of(j * t, t), t)
        kt, vt = kbf[ks, :], vbf[ks, :]
        out = []
        for qc, (c, acc) in ((q0, carry[:2]), (q1, carry[2:])):
            z = _dot_nt(qc, kt)
            log_beta = _log_sigmoid(z)
            log_1mb = log_beta - z
            if diag:
                log_1mb = jnp.where(row > col, log_1mb, 0.0)
            hi = log_1mb.astype(BF16)
            lo = (log_1mb - hi.astype(F32)).astype(BF16)
            after = _dot(hi, umat) + _dot(lo, umat)
            a = jnp.exp(log_beta + after + c)
            if diag:
                a = jnp.where(row > col, a, 0.0)
            acc = acc + _dot(a.astype(BF16), vt)
            c = c + jnp.sum(log_1mb, axis=-1, keepdims=True)
            out.extend((c, acc))
        return tuple(out)

    init = (jnp.zeros((t, 1), F32), jnp.zeros((t, LANES), F32)) * 2
    carry = tile(qi, init, True)
    _, a0, _, a1 = lax.fori_loop(0, qi, lambda i, c: tile(qi - 1 - i, c, False), carry)
    o = jnp.where(_lane_lo(a0.shape), a0, a1)
    o_ref[0] = (_seg_rms(o, og_ref[0]) * _silu(g_ref[0])).astype(o_ref.dtype)


def _attn_specs(seq, t, col):
    q_spec = pl.BlockSpec((1, t, LANES), lambda b, p, i: (b, i, col + p))
    k_spec = pl.BlockSpec((1, seq, LANES), lambda b, p, i: (b, 0, col + N_PAIRS + p))
    v_spec = pl.BlockSpec((1, seq, LANES), lambda b, p, i: (b, 0, col + 2 * N_PAIRS + p))
    g_spec = pl.BlockSpec((1, t, LANES), lambda b, p, i: (b, i, col + 3 * N_PAIRS + p))
    return [q_spec, k_spec, v_spec, g_spec]


def _pair_spec():
    return pl.BlockSpec((1, 1, LANES), lambda b, p, i: (p, 0, 0))


def _const_spec(shape):
    return pl.BlockSpec(shape, lambda b, p, i: (0,) * len(shape))


def _attn_call(body, proj, col, extra_specs, extra_args):
    bsz, seq, _ = proj.shape
    t = min(ATT_TILE, seq)
    return pl.pallas_call(
        body,
        grid=(bsz, N_PAIRS, seq // t),
        in_specs=_attn_specs(seq, t, col) + extra_specs,
        out_specs=pl.BlockSpec((1, t, LANES), lambda b, p, i: (b, i, p)),
        out_shape=jax.ShapeDtypeStruct((bsz, seq, GROUP_W), BF16),
        scratch_shapes=[pltpu.VMEM((seq, LANES), BF16), pltpu.VMEM((seq, LANES), BF16)],
        compiler_params=_cparams(("arbitrary", "arbitrary", "arbitrary")),
    )(proj, proj, proj, proj, *extra_args)


def _pair_gain(g64):
    return jnp.tile(g64.astype(F32), 2).reshape(1, LANES)


def _attn_a(proj, tz, q_gain, k_gain, o_gain, dlam, lam_init):
    t = tz.shape[-1]
    specs = [pl.BlockSpec((1, 3, t, t), lambda b, p, i: (p, 0, 0, 0)),
             _const_spec((1, LANES)), _const_spec((1, LANES)), _pair_spec(), _const_spec((4, HEAD_DIM))]
    args = (tz, _pair_gain(q_gain), _pair_gain(k_gain), o_gain.reshape(N_PAIRS, 1, LANES), dlam)
    return _attn_call(functools.partial(_attn_a_kernel, lam_init=lam_init), proj, COL_A, specs, args)


def _attn_b(proj, o_gain):
    t = min(ATT_TILE, proj.shape[1])
    umat = (lax.broadcasted_iota(jnp.int32, (t, t), 0) > lax.broadcasted_iota(jnp.int32, (t, t), 1)).astype(BF16)
    specs = [_const_spec((t, t)), _pair_spec()]
    return _attn_call(_attn_b_kernel, proj, COL_B, specs, (umat, o_gain.reshape(N_PAIRS, 1, LANES)))


def _attn_c(proj, cum_f, q_gain, k_gain, o_gain):
    seq = proj.shape[1]
    specs = [pl.BlockSpec((1, 1, 2, seq), lambda b, p, i: (b, p, 0, 0)),
             _const_spec((1, LANES)), _const_spec((1, LANES)), _pair_spec()]
    args = (cum_f, _pair_gain(q_gain), _pair_gain(k_gain), o_gain.reshape(N_PAIRS, 1, LANES))
    return _attn_call(_attn_c_kernel, proj, COL_C, specs, args)


def _fox_cum_kernel(s_ref, fb_ref, tri_ref, o_ref):
    seq = s_ref.shape[1]
    rows_per = tri_ref.shape[0]
    carry = jnp.zeros((LANES, 1), F32)
    for c in range(seq // rows_per):
        rows = slice(c * rows_per, (c + 1) * rows_per)
        log_f = _log_sigmoid(s_ref[0, rows, :] + fb_ref[...])
        log_f_t = log_f.T
        cum = _dot(log_f_t, tri_ref[...], precision=HIGHEST) + carry
        o_ref[0, :, rows] = cum[:8, :]
        carry = carry + jnp.sum(log_f_t, axis=-1, keepdims=True)


def _fox_cum(proj, forget_bias):
    bsz, seq, _ = proj.shape
    rows_per = min(PREP_ROWS, seq)
    tri = (lax.broadcasted_iota(jnp.int32, (rows_per, rows_per), 0)
           <= lax.broadcasted_iota(jnp.int32, (rows_per, rows_per), 1)).astype(F32)
    fb = jnp.zeros((1, LANES), F32).at[0, :forget_bias.shape[0]].set(forget_bias.astype(F32))
    out = pl.pallas_call(
        _fox_cum_kernel,
        grid=(bsz,),
        in_specs=[pl.BlockSpec((1, seq, LANES), lambda b: (b, 0, COL_SMALL)),
                  pl.BlockSpec((1, LANES), lambda b: (0, 0)),
                  pl.BlockSpec((rows_per, rows_per), lambda b: (0, 0))],
        out_specs=pl.BlockSpec((1, 8, seq), lambda b: (b, 0, 0)),
        out_shape=jax.ShapeDtypeStruct((bsz, 8, seq), F32),
        compiler_params=_cparams(("arbitrary",)),
    )(proj, fb, tri)
    return out.reshape(bsz, N_PAIRS, 2, seq)


def _rwkv_prep_kernel(u_ref, s_ref, mu_ref, mus_ref, wup_ref, w0_ref, aup_ref, a0_ref, kkr_ref,
                      r_o, lw_o, k_o, v_o, aa_o, b_o, prev_u, prev_s):
    @pl.when(pl.program_id(1) == 0)
    def _():
        prev_u[...] = jnp.zeros_like(prev_u)
        prev_s[...] = jnp.zeros_like(prev_s)

    def shift(x, prev, mu):
        first = lax.broadcasted_iota(jnp.int32, x.shape, 0) == 0
        x_prev = jnp.where(first, prev[...], pltpu.roll(x, 1, 0))
        prev[...] = x[x.shape[0] - 1:, :]
        return x + (x_prev - x) * mu

    u = shift(u_ref[0], prev_u, mu_ref[...])
    sm = shift(s_ref[0], prev_s, mus_ref[...])
    r, k, v = u[:, :GROUP_W], u[:, GROUP_W:2 * GROUP_W], u[:, 2 * GROUP_W:]
    w_pre = w0_ref[...] + _dot(jnp.tanh(sm), wup_ref[...], precision=HIGHEST)
    w_log = _log_sigmoid(w_pre) - 0.5
    a = 1.0 / (1.0 + jnp.exp(-(a0_ref[...] + _dot(sm, aup_ref[...], precision=HIGHEST))))
    kk = k * kkr_ref[0:1, :]
    kk_n = jnp.concatenate(
        [kk[:, i * LANES:(i + 1) * LANES]
         / jnp.maximum(jnp.sqrt(_seg_sum(kk[:, i * LANES:(i + 1) * LANES] ** 2)), 1e-12)
         for i in range(N_PAIRS)], axis=-1)
    r_o[0] = r
    lw_o[0] = -jnp.exp(w_log)
    k_o[0] = k * (1.0 + (a - 1.0) * kkr_ref[1:2, :])
    v_o[0] = v
    aa_o[0] = -kk_n
    b_o[0] = kk_n * a


def _rwkv_prep(proj, mu, w_up, w0, a_up, a0, kkr):
    bsz, seq, _ = proj.shape
    ts = min(PREP_ROWS, seq)
    mu = mu.astype(F32)
    mu_u = mu[:3 * GROUP_W].reshape(1, 3 * GROUP_W)
    mu_s = (jnp.zeros((1, LANES), F32)
            .at[0, SMALL_WLO:SMALL_WLO + RWKV_RANK].set(mu[3 * GROUP_W:3 * GROUP_W + RWKV_RANK])
            .at[0, SMALL_ALO:SMALL_ALO + RWKV_RANK].set(mu[3 * GROUP_W + RWKV_RANK:]))
    wup = jnp.zeros((LANES, GROUP_W), F32).at[SMALL_WLO:SMALL_WLO + RWKV_RANK].set(w_up.astype(F32))
    aup = jnp.zeros((LANES, GROUP_W), F32).at[SMALL_ALO:SMALL_ALO + RWKV_RANK].set(a_up.astype(F32))
    vec = lambda a: a.astype(F32).reshape(1, GROUP_W)
    full = lambda shape: pl.BlockSpec(shape, lambda b, i: (0,) * len(shape))
    out_spec = pl.BlockSpec((1, ts, GROUP_W), lambda b, i: (b, i, 0))
    out_sds = jax.ShapeDtypeStruct((bsz, seq, GROUP_W), F32)
    return pl.pallas_call(
        _rwkv_prep_kernel,
        grid=(bsz, seq // ts),
        in_specs=[pl.BlockSpec((1, ts, 3 * GROUP_W), lambda b, i: (b, i, COL_D * LANES // (3 * GROUP_W))),
                  pl.BlockSpec((1, ts, LANES), lambda b, i: (b, i, COL_SMALL)),
                  full((1, 3 * GROUP_W)), full((1, LANES)), full((LANES, GROUP_W)), full((1, GROUP_W)),
                  full((LANES, GROUP_W)), full((1, GROUP_W)), full((3, GROUP_W))],
        out_specs=[out_spec] * 6,
        out_shape=[out_sds] * 6,
        scratch_shapes=[pltpu.VMEM((1, 3 * GROUP_W), F32), pltpu.VMEM((1, LANES), F32)],
        compiler_params=_cparams(("arbitrary", "arbitrary")),
    )(proj, proj, mu_u, mu_s, wup, vec(w0), aup, vec(a0), kkr.astype(F32))


def _neumann_inverse(a):
    n = a.shape[0]
    eye = (lax.broadcasted_iota(jnp.int32, (n, n), 0) == lax.broadcasted_iota(jnp.int32, (n, n), 1)).astype(F32)
    p = eye + a
    x = a
    for _ in range(int(math.log2(n)) - 1):
        x = _dot(x, x, precision=HIGHEST)
        p = p + _dot(p, x, precision=HIGHEST)
    return p


def _rwkv_scan_kernel(r_ref, lw_ref, k_ref, v_ref, aa_ref, b_ref, g_ref, lng_ref, lnb_ref, rk_ref, tri_ref,
                      o_ref, state):
    c = r_ref.shape[1]

    @pl.when(pl.program_id(1) == 0)
    def _():
        state[...] = jnp.zeros_like(state)

    row = lax.broadcasted_iota(jnp.int32, (c, c), 0)
    col = lax.broadcasted_iota(jnp.int32, (c, c), 1)
    strict, incl = row > col, row >= col
    lo = _lane_lo((c, LANES))
    diag_blocks = ((lax.broadcasted_iota(jnp.int32, (LANES, LANES), 0) < HEAD_DIM)
                   == (lax.broadcasted_iota(jnp.int32, (LANES, LANES), 1) < HEAD_DIM))
    tri = tri_ref[...]

    for p in range(N_PAIRS):
        lanes = slice(p * LANES, (p + 1) * LANES)
        r, lw, k, v = r_ref[0, :, lanes], lw_ref[0, :, lanes], k_ref[0, :, lanes], v_ref[0, :, lanes]
        aa, b = aa_ref[0, :, lanes], b_ref[0, :, lanes]
        s_mat = state[p]

        cl = _dot(tri, lw, precision=HIGHEST)
        cl_end = cl[c - 1:c, :]
        cl_mid = cl[c // 2 - 1:c // 2, :]
        w_in = jnp.exp(cl - cl_mid)
        w_ex = jnp.exp(cl - lw - cl_mid)
        w_inv = jnp.exp(cl_mid - cl)
        w_rest = jnp.exp(cl_end - cl)
        r_t, a_t, b_t, k_t = r * w_in, aa * w_ex, b * w_inv, k * w_inv
        r_s, a_s = r * jnp.exp(cl), aa * jnp.exp(cl - lw)

        base_u = _dot_nt(a_s, s_mat, precision=HIGHEST)
        base_y = _dot_nt(r_s, s_mat, precision=HIGHEST)
        qa = jnp.concatenate([a_t, r_t], axis=0)
        u_heads, y_heads = [], []
        for h in range(2):
            qa_h = jnp.where(_lane_lo(qa.shape) == (h == 0), qa, 0.0)
            gb = _dot_nt(qa_h, b_t, precision=HIGHEST)
            gk = _dot_nt(qa_h, k_t, precision=HIGHEST)
            a_ab, a_rb = jnp.where(strict, gb[:c], 0.0), jnp.where(incl, gb[c:], 0.0)
            a_ak, a_rk = jnp.where(strict, gk[:c], 0.0), jnp.where(incl, gk[c:], 0.0)
            t_inv = _neumann_inverse(a_ab)
            u_h = _dot(t_inv, base_u + _dot(a_ak, v, precision=HIGHEST), precision=HIGHEST)
            u_heads.append(u_h)
            y_heads.append((a_rb, a_rk))
        u = jnp.where(lo, u_heads[0], u_heads[1])
        for h in range(2):
            a_rb, a_rk = y_heads[h]
            y_heads[h] = _dot(a_rb, u, precision=HIGHEST) + _dot(a_rk, v, precision=HIGHEST)
        y = base_y + jnp.where(lo, y_heads[0], y_heads[1])

        upd = (_dot(u.T, b * w_rest, precision=HIGHEST) + _dot(v.T, k * w_rest, precision=HIGHEST))
        state[p] = s_mat * jnp.exp(cl_end) + jnp.where(diag_blocks, upd, 0.0)

        mean = _seg_sum(y) * (1.0 / HEAD_DIM)
        yc = y - mean
        var = _seg_sum(yc * yc) * (1.0 / HEAD_DIM)
        out = yc * lax.rsqrt(var + RWKV_LN_EPS) * lng_ref[:, lanes] + lnb_ref[:, lanes]
        out = out + _seg_sum(r * k * rk_ref[:, lanes]) * v
        o_ref[0, :, lanes] = (out * _silu(g_ref[0, :, lanes])).astype(o_ref.dtype)


def _rwkv_scan(proj, prepped, ln_gain, ln_bias, r_k):
    bsz, seq, _ = proj.shape
    c = min(RWKV_CHUNK, seq)
    tri = (lax.broadcasted_iota(jnp.int32, (c, c), 0) >= lax.broadcasted_iota(jnp.int32, (c, c), 1)).astype(F32)
    tok_spec = pl.BlockSpec((1, c, GROUP_W), lambda b, i: (b, i, 0))
    vec_spec = pl.BlockSpec((1, GROUP_W), lambda b, i: (0, 0))
    vec = lambda a: a.astype(F32).reshape(1, GROUP_W)
    return pl.pallas_call(
        _rwkv_scan_kernel,
        grid=(bsz, seq // c),
        in_specs=[tok_spec] * 6
        + [pl.BlockSpec((1, c, GROUP_W), lambda b, i: (b, i, COL_DG * LANES // GROUP_W)),
           vec_spec, vec_spec, vec_spec, pl.BlockSpec((c, c), lambda b, i: (0, 0))],
        out_specs=tok_spec,
        out_shape=jax.ShapeDtypeStruct((bsz, seq, GROUP_W), BF16),
        scratch_shapes=[pltpu.VMEM((N_PAIRS, LANES, LANES), F32)],
        compiler_params=_cparams(("arbitrary", "arbitrary")),
    )(*prepped, proj, vec(ln_gain), vec(ln_bias), vec(r_k), tri)


def _out_kernel(a_ref, b_ref, c_ref, d_ref, w_ref, x_ref, o_ref):
    acc = x_ref[...]
    for i, m_ref in enumerate((a_ref, b_ref, c_ref, d_ref)):
        acc = acc + _dot(m_ref[...], w_ref[i * GROUP_W:(i + 1) * GROUP_W, :])
    o_ref[...] = acc


def _out_proj(groups, w_out, x2d):
    m, d = x2d.shape
    tm = min(512, m)
    g_spec = pl.BlockSpec((tm, GROUP_W), lambda i: (i, 0))
    return pl.pallas_call(
        _out_kernel,
        grid=(m // tm,),
        in_specs=[g_spec] * 4 + [pl.BlockSpec(w_out.shape, lambda i: (0, 0)),
                                 pl.BlockSpec((tm, d), lambda i: (i, 0))],
        out_specs=pl.BlockSpec((tm, d), lambda i: (i, 0)),
        out_shape=jax.ShapeDtypeStruct((m, d), F32),
        compiler_params=_cparams(("arbitrary",)),
    )(*[g.reshape(m, GROUP_W) for g in groups], w_out, x2d)


def _t5_causal_bucket(dist):
    max_exact = NUM_BUCKETS // 2
    d = jnp.maximum(dist, 1).astype(F32)
    large = max_exact + (jnp.log(d / max_exact) / math.log(MAX_DISTANCE / max_exact)
                         * (NUM_BUCKETS - max_exact)).astype(jnp.int32)
    large = jnp.minimum(large, NUM_BUCKETS - 1)
    return jnp.where(dist < max_exact, dist, large)


def _bias_tiles(rel_bias, seq, t):
    assert t >= MAX_DISTANCE
    bias_by_dist = rel_bias.astype(F32)[_t5_causal_bucket(jnp.arange(seq))]
    i = np.arange(t)
    idx = np.clip(np.arange(3)[:, None, None] * t + i[None, :, None] - i[None, None, :], 0, seq - 1)
    return jnp.transpose(bias_by_dist[idx], (3, 0, 1, 2))


def _pack_w_in(w):
    d = w.shape[0]
    att = w[:, :12 * GROUP_W]
    cf = w[:, 12 * GROUP_W:12 * GROUP_W + 8]
    off = 12 * GROUP_W + 8
    rkv = w[:, off:off + 3 * GROUP_W]
    w_lo = w[:, off + 3 * GROUP_W:off + 3 * GROUP_W + RWKV_RANK]
    a_lo = w[:, off + 3 * GROUP_W + RWKV_RANK:off + 3 * GROUP_W + 2 * RWKV_RANK]
    dg = w[:, off + 3 * GROUP_W + 2 * RWKV_RANK:]
    small = (jnp.zeros((d, LANES), w.dtype).at[:, :8].set(cf)
             .at[:, SMALL_WLO:SMALL_WLO + RWKV_RANK].set(w_lo)
             .at[:, SMALL_ALO:SMALL_ALO + RWKV_RANK].set(a_lo))
    return jnp.concatenate([att, rkv, dg, small], axis=1).astype(BF16)


def kernel(x, norm_gain, w_in, w_out, rel_bias, qk_gain, diff_lambda, forget_bias, out_gain, rwkv_mu, rwkv_w_up,
           rwkv_w0, rwkv_a_up, rwkv_a0, rwkv_kkr, rwkv_ln_gain, rwkv_ln_bias):
    bsz, seq, d = x.shape
    tz = _bias_tiles(rel_bias, seq, min(ATT_TILE, seq))
    h = x.reshape(bsz * seq, d)
    for l in range(DEPTH):
        og = out_gain[l].astype(F32)
        proj = _proj(h, norm_gain[l].astype(F32), _pack_w_in(w_in[l])).reshape(bsz, seq, N_PROJ)
        lam_init = 0.8 - 0.6 * math.exp(-0.3 * l)
        oa = _attn_a(proj, tz, qk_gain[l, 0], qk_gain[l, 1], og[:GROUP_W], diff_lambda[l].astype(F32), lam_init)
        ob = _attn_b(proj, og[GROUP_W:2 * GROUP_W])
        cum_f = _fox_cum(proj, forget_bias[l])
        oc = _attn_c(proj, cum_f, qk_gain[l, 2], qk_gain[l, 3], og[2 * GROUP_W:3 * GROUP_W])
        prepped = _rwkv_prep(proj, rwkv_mu[l], rwkv_w_up[l], rwkv_w0[l], rwkv_a_up[l], rwkv_a0[l], rwkv_kkr[l])
        od = _rwkv_scan(proj, prepped, rwkv_ln_gain[l].astype(F32) * og[3 * GROUP_W:], rwkv_ln_bias[l],
                        rwkv_kkr[l, 2])
        h = _out_proj((oa, ob, oc, od), w_out[l].astype(BF16), h)
    return h.reshape(bsz, seq, d)
```

```python
import functools
import math

import numpy as np
import jax
import jax.numpy as jnp
from jax import lax
from jax.experimental import pallas as pl
from jax.experimental.pallas import tpu as pltpu

F32 = jnp.float32
BF16 = jnp.bfloat16
HIGHEST = lax.Precision.HIGHEST

D_MODEL = 1024
DEPTH = 2
GROUP_W = 512
HEAD_DIM = 64
H_A = 4
NUM_BUCKETS = 32
MAX_DISTANCE = 128
RWKV_RANK = 32
NORM_EPS = 1e-6
RWKV_LN_EPS = 64e-5
NEG_INF = -1e30

LANES = 128
N_PAIRS = GROUP_W // LANES
ATT_SCALE = HEAD_DIM ** -0.5

COL_A, COL_B, COL_C = 0, 16, 32
COL_D = 48
COL_DG = 60
COL_SMALL = 64
N_PROJ = (COL_SMALL + 1) * LANES
SMALL_WLO, SMALL_ALO = 32, 64

ATT_TILE = 512
PREP_ROWS = 512
RWKV_CHUNK = 128
RWKV_PIECES = 1
RWKV_PIECES_G = 2
VMEM_LIMIT = 48 * 1024 * 1024


def _cparams(sem):
    return pltpu.CompilerParams(dimension_semantics=sem, vmem_limit_bytes=VMEM_LIMIT)


def _dot(a, b, **kw):
    return jnp.dot(a, b, preferred_element_type=F32, **kw)


def _dot_nt(a, b, **kw):
    return lax.dot_general(a, b, (((1,), (1,)), ((), ())), preferred_element_type=F32, **kw)


def _lane_lo(shape):
    return lax.broadcasted_iota(jnp.int32, shape, len(shape) - 1) < HEAD_DIM


def _seg_sum(x):
    lo = _lane_lo(x.shape)
    s_lo = jnp.sum(jnp.where(lo, x, 0.0), axis=-1, keepdims=True)
    s_hi = jnp.sum(jnp.where(lo, 0.0, x), axis=-1, keepdims=True)
    return jnp.where(lo, s_lo, s_hi)


def _seg_rms(x, gain):
    return x * lax.rsqrt(_seg_sum(x * x) * (1.0 / HEAD_DIM) + NORM_EPS) * gain


def _log_sigmoid(z):
    return jnp.minimum(z, 0.0) - jnp.log1p(jnp.exp(-jnp.abs(z)))


def _silu(g):
    return g / (1.0 + jnp.exp(-g))


def _proj_kernel(x_ref, g_ref, w_ref, o_ref, h_ref):
    @pl.when(pl.program_id(1) == 0)
    def _():
        x = x_ref[...]
        ms = jnp.mean(x * x, axis=-1, keepdims=True)
        h_ref[...] = (x * lax.rsqrt(ms + NORM_EPS) * g_ref[...]).astype(BF16)

    o_ref[...] = _dot(h_ref[...], w_ref[...])


def _proj(x2d, gain, w_packed):
    m, d = x2d.shape
    n = w_packed.shape[1]
    tm = min(1024, m)
    tn = n // 5
    return pl.pallas_call(
        _proj_kernel,
        grid=(m // tm, n // tn),
        in_specs=[pl.BlockSpec((tm, d), lambda i, j: (i, 0)),
                  pl.BlockSpec((1, d), lambda i, j: (0, 0)),
                  pl.BlockSpec((d, tn), lambda i, j: (0, j))],
        out_specs=pl.BlockSpec((tm, tn), lambda i, j: (i, j)),
        out_shape=jax.ShapeDtypeStruct((m, n), F32),
        scratch_shapes=[pltpu.VMEM((tm, d), BF16)],
        compiler_params=_cparams(("arbitrary", "arbitrary")),
    )(x2d, gain.reshape(1, d), w_packed)


def _prep_kv(k_ref, v_ref, kbf, vbf, k_gain):
    seq = k_ref.shape[1]
    rows_per = min(PREP_ROWS, seq)

    def body(c, carry):
        rows = pl.ds(pl.multiple_of(c * rows_per, rows_per), rows_per)
        k = k_ref[0, rows, :]
        if k_gain is not None:
            k = _seg_rms(k, k_gain)
        kbf[rows, :] = k.astype(BF16)
        vbf[rows, :] = v_ref[0, rows, :].astype(BF16)
        return carry

    lax.fori_loop(0, seq // rows_per, body, 0)


def _head_masked(q):
    lo = _lane_lo(q.shape)
    return jnp.where(lo, q, 0.0).astype(BF16), jnp.where(lo, 0.0, q).astype(BF16)


def _softmax_tile(s, m, l, acc, vt):
    m_new = jnp.maximum(m, jnp.max(s, axis=-1, keepdims=True))
    alpha = jnp.exp(m - m_new)
    p = jnp.exp(s - m_new)
    l = alpha * l + jnp.sum(p, axis=-1, keepdims=True)
    acc = alpha * acc + _dot(p.astype(BF16), vt)
    return m_new, l, acc


def _softmax_init(t):
    return (jnp.full((t, 1), NEG_INF, F32), jnp.zeros((t, 1), F32), jnp.zeros((t, LANES), F32))


def _attn_a_kernel(q_ref, k_ref, v_ref, g_ref, tz_ref, qg_ref, kg_ref, og_ref, dl_ref, o_ref, kbf, vbf,
                   *, lam_init):
    qi = pl.program_id(2)
    t = q_ref.shape[1]

    @pl.when(qi == 0)
    def _():
        _prep_kv(k_ref, v_ref, kbf, vbf, kg_ref[...])

    q0, q1 = _head_masked(_seg_rms(q_ref[0], qg_ref[...]) * ATT_SCALE)
    row = lax.broadcasted_iota(jnp.int32, (t, t), 0)
    col = lax.broadcasted_iota(jnp.int32, (t, t), 1)

    def tile(j, carry, diag):
        ks = pl.ds(pl.multiple_of(j * t, t), t)
        kt, vt = kbf[ks, :], vbf[ks, :]
        bias = tz_ref[0, jnp.minimum(qi - j, 2)]
        out = []
        for qc, st in ((q0, carry[:3]), (q1, carry[3:])):
            s = _dot_nt(qc, kt) + bias
            if diag:
                s = jnp.where(row >= col, s, NEG_INF)
            out.extend(_softmax_tile(s, *st, vt))
        return tuple(out)

    carry = lax.fori_loop(0, qi, lambda j, c: tile(j, c, False), _softmax_init(t) * 2)
    _, l0, a0, _, l1, a1 = tile(qi, carry, True)

    dl = dl_ref[...]
    lam = (jnp.exp(jnp.sum(dl[0:1] * dl[1:2], axis=-1, keepdims=True))
           - jnp.exp(jnp.sum(dl[2:3] * dl[3:4], axis=-1, keepdims=True)) + lam_init)
    o = a0 / l0 - lam * (a1 / l1)
    y = o * lax.rsqrt(jnp.mean(o * o, axis=-1, keepdims=True) + NORM_EPS) * og_ref[0] * (1.0 - lam_init)
    o_ref[0] = (y * _silu(g_ref[0])).astype(o_ref.dtype)


def _attn_c_kernel(q_ref, k_ref, v_ref, g_ref, f_ref, qg_ref, kg_ref, og_ref, o_ref, kbf, vbf):
    qi = pl.program_id(2)
    t = q_ref.shape[1]

    @pl.when(qi == 0)
    def _():
        _prep_kv(k_ref, v_ref, kbf, vbf, kg_ref[...])

    q0, q1 = _head_masked(_seg_rms(q_ref[0], qg_ref[...]) * ATT_SCALE)
    row = lax.broadcasted_iota(jnp.int32, (t, t), 0)
    col = lax.broadcasted_iota(jnp.int32, (t, t), 1)

    def tile(j, carry, diag):
        ks = pl.ds(pl.multiple_of(j * t, t), t)
        kt, vt = kbf[ks, :], vbf[ks, :]
        out = []
        for h, (qc, st) in enumerate(((q0, carry[:3]), (q1, carry[3:]))):
            s = _dot_nt(qc, kt) - f_ref[0, 0, h:h + 1, ks]
            if diag:
                s = jnp.where(row >= col, s, NEG_INF)
            out.extend(_softmax_tile(s, *st, vt))
        return tuple(out)

    carry = lax.fori_loop(0, qi, lambda j, c: tile(j, c, False), _softmax_init(t) * 2)
    _, l0, a0, _, l1, a1 = tile(qi, carry, True)
    o = jnp.where(_lane_lo(a0.shape), a0 / l0, a1 / l1)
    o_ref[0] = (_seg_rms(o, og_ref[0]) * _silu(g_ref[0])).astype(o_ref.dtype)


def _attn_b_kernel(q_ref, k_ref, v_ref, g_ref, u_ref, og_ref, o_ref, kbf, vbf):
    qi = pl.program_id(2)
    t = q_ref.shape[1]

    @pl.when(qi == 0)
    def _():
        _prep_kv(k_ref, v_ref, kbf, vbf, None)

    q0, q1 = _head_masked(q_ref[0] * ATT_SCALE)
    row = lax.broadcasted_iota(jnp.int32, (t, t), 0)
    col = lax.broadcasted_iota(jnp.int32, (t, t), 1)
    umat = u_ref[...]

    def tile(j, carry, diag):
        ks = pl.ds(pl.multiple_of(j * t, t), t)
        kt, vt = kbf[ks, :], vbf[ks, :]
        out = []
        for qc, (c, acc) in ((q0, carry[:2]), (q1, carry[2:])):
            z = _dot_nt(qc, kt)
            log_beta = _log_sigmoid(z)
            log_1mb = log_beta - z
            if diag:
                log_1mb = jnp.where(row > col, log_1mb, 0.0)
            hi = log_1mb.astype(BF16)
            lo = (log_1mb - hi.astype(F32)).astype(BF16)
            after = _dot(hi, umat) + _dot(lo, umat)
            a = jnp.exp(log_beta + after + c)
            if diag:
                a = jnp.where(row > col, a, 0.0)
            acc = acc + _dot(a.astype(BF16), vt)
            c = c + jnp.sum(log_1mb, axis=-1, keepdims=True)
            out.extend((c, acc))
        return tuple(out)

    init = (jnp.zeros((t, 1), F32), jnp.zeros((t, LANES), F32)) * 2
    carry = tile(qi, init, True)
    _, a0, _, a1 = lax.fori_loop(0, qi, lambda i, c: tile(qi - 1 - i, c, False), carry)
    o = jnp.where(_lane_lo(a0.shape), a0, a1)
    o_ref[0] = (_seg_rms(o, og_ref[0]) * _silu(g_ref[0])).astype(o_ref.dtype)


def _attn_specs(seq, t, col):
    q_spec = pl.BlockSpec((1, t, LANES), lambda b, p, i: (b, i, col + p))
    k_spec = pl.BlockSpec((1, seq, LANES), lambda b, p, i: (b, 0, col + N_PAIRS + p))
    v_spec = pl.BlockSpec((1, seq, LANES), lambda b, p, i: (b, 0, col + 2 * N_PAIRS + p))
    g_spec = pl.BlockSpec((1, t, LANES), lambda b, p, i: (b, i, col + 3 * N_PAIRS + p))
    return [q_spec, k_spec, v_spec, g_spec]


def _pair_spec():
    return pl.BlockSpec((1, 1, LANES), lambda b, p, i: (p, 0, 0))


def _const_spec(shape):
    return pl.BlockSpec(shape, lambda b, p, i: (0,) * len(shape))


def _attn_call(body, proj, col, extra_specs, extra_args):
    bsz, seq, _ = proj.shape
    t = min(ATT_TILE, seq)
    return pl.pallas_call(
        body,
        grid=(bsz, N_PAIRS, seq // t),
        in_specs=_attn_specs(seq, t, col) + extra_specs,
        out_specs=pl.BlockSpec((1, t, LANES), lambda b, p, i: (b, i, p)),
        out_shape=jax.ShapeDtypeStruct((bsz, seq, GROUP_W), BF16),
        scratch_shapes=[pltpu.VMEM((seq, LANES), BF16), pltpu.VMEM((seq, LANES), BF16)],
        compiler_params=_cparams(("arbitrary", "arbitrary", "arbitrary")),
    )(proj, proj, proj, proj, *extra_args)


def _pair_gain(g64):
    return jnp.tile(g64.astype(F32), 2).reshape(1, LANES)


def _attn_a(proj, tz, q_gain, k_gain, o_gain, dlam, lam_init):
    t = tz.shape[-1]
    specs = [pl.BlockSpec((1, 3, t, t), lambda b, p, i: (p, 0, 0, 0)),
             _const_spec((1, LANES)), _const_spec((1, LANES)), _pair_spec(), _const_spec((4, HEAD_DIM))]
    args = (tz, _pair_gain(q_gain), _pair_gain(k_gain), o_gain.reshape(N_PAIRS, 1, LANES), dlam)
    return _attn_call(functools.partial(_attn_a_kernel, lam_init=lam_init), proj, COL_A, specs, args)


def _attn_b(proj, o_gain):
    t = min(ATT_TILE, proj.shape[1])
    umat = (lax.broadcasted_iota(jnp.int32, (t, t), 0) > lax.broadcasted_iota(jnp.int32, (t, t), 1)).astype(BF16)
    specs = [_const_spec((t, t)), _pair_spec()]
    return _attn_call(_attn_b_kernel, proj, COL_B, specs, (umat, o_gain.reshape(N_PAIRS, 1, LANES)))


def _attn_c(proj, cum_f, q_gain, k_gain, o_gain):
    seq = proj.shape[1]
    specs = [pl.BlockSpec((1, 1, 2, seq), lambda b, p, i: (b, p, 0, 0)),
             _const_spec((1, LANES)), _const_spec((1, LANES)), _pair_spec()]
    args = (cum_f, _pair_gain(q_gain), _pair_gain(k_gain), o_gain.reshape(N_PAIRS, 1, LANES))
    return _attn_call(_attn_c_kernel, proj, COL_C, specs, args)


def _fox_cum_kernel(s_ref, fb_ref, tri_ref, o_ref):
    seq = s_ref.shape[1]
    rows_per = tri_ref.shape[0]
    carry = jnp.zeros((LANES, 1), F32)
    for c in range(seq // rows_per):
        rows = slice(c * rows_per, (c + 1) * rows_per)
        log_f = _log_sigmoid(s_ref[0, rows, :] + fb_ref[...])
        log_f_t = log_f.T
        cum = _dot(log_f_t, tri_ref[...], precision=HIGHEST) + carry
        o_ref[0, :, rows] = cum[:8, :]
        carry = carry + jnp.sum(log_f_t, axis=-1, keepdims=True)


def _fox_cum(proj, forget_bias):
    bsz, seq, _ = proj.shape
    rows_per = min(PREP_ROWS, seq)
    tri = (lax.broadcasted_iota(jnp.int32, (rows_per, rows_per), 0)
           <= lax.broadcasted_iota(jnp.int32, (rows_per, rows_per), 1)).astype(F32)
    fb = jnp.zeros((1, LANES), F32).at[0, :forget_bias.shape[0]].set(forget_bias.astype(F32))
    out = pl.pallas_call(
        _fox_cum_kernel,
        grid=(bsz,),
        in_specs=[pl.BlockSpec((1, seq, LANES), lambda b: (b, 0, COL_SMALL)),
                  pl.BlockSpec((1, LANES), lambda b: (0, 0)),
                  pl.BlockSpec((rows_per, rows_per), lambda b: (0, 0))],
        out_specs=pl.BlockSpec((1, 8, seq), lambda b: (b, 0, 0)),
        out_shape=jax.ShapeDtypeStruct((bsz, 8, seq), F32),
        compiler_params=_cparams(("arbitrary",)),
    )(proj, fb, tri)
    return out.reshape(bsz, N_PAIRS, 2, seq)


def _rwkv_prep_kernel(u_ref, s_ref, mu_ref, mus_ref, wup_ref, w0_ref, aup_ref, a0_ref, kkr_ref,
                      r_o, lw_o, k_o, v_o, aa_o, b_o, prev_u, prev_s):
    @pl.when(pl.program_id(1) == 0)
    def _():
        prev_u[...] = jnp.zeros_like(prev_u)
        prev_s[...] = jnp.zeros_like(prev_s)

    def shift(x, prev, mu):
        first = lax.broadcasted_iota(jnp.int32, x.shape, 0) == 0
        x_prev = jnp.where(first, prev[...], pltpu.roll(x, 1, 0))
        prev[...] = x[x.shape[0] - 1:, :]
        return x + (x_prev - x) * mu

    u = shift(u_ref[0], prev_u, mu_ref[...])
    sm = shift(s_ref[0], prev_s, mus_ref[...])
    r, k, v = u[:, :GROUP_W], u[:, GROUP_W:2 * GROUP_W], u[:, 2 * GROUP_W:]
    w_pre = w0_ref[...] + _dot(jnp.tanh(sm), wup_ref[...], precision=HIGHEST)
    w_log = _log_sigmoid(w_pre) - 0.5
    a = 1.0 / (1.0 + jnp.exp(-(a0_ref[...] + _dot(sm, aup_ref[...], precision=HIGHEST))))
    kk = k * kkr_ref[0:1, :]
    kk_n = jnp.concatenate(
        [kk[:, i * LANES:(i + 1) * LANES]
         / jnp.maximum(jnp.sqrt(_seg_sum(kk[:, i * LANES:(i + 1) * LANES] ** 2)), 1e-12)
         for i in range(N_PAIRS)], axis=-1)
    r_o[0] = r
    lw_o[0] = -jnp.exp(w_log)
    k_o[0] = k * (1.0 + (a - 1.0) * kkr_ref[1:2, :])
    v_o[0] = v
    aa_o[0] = -kk_n
    b_o[0] = kk_n * a


def _rwkv_prep(proj, mu, w_up, w0, a_up, a0, kkr):
    bsz, seq, _ = proj.shape
    ts = min(PREP_ROWS, seq)
    mu = mu.astype(F32)
    mu_u = mu[:3 * GROUP_W].reshape(1, 3 * GROUP_W)
    mu_s = (jnp.zeros((1, LANES), F32)
            .at[0, SMALL_WLO:SMALL_WLO + RWKV_RANK].set(mu[3 * GROUP_W:3 * GROUP_W + RWKV_RANK])
            .at[0, SMALL_ALO:SMALL_ALO + RWKV_RANK].set(mu[3 * GROUP_W + RWKV_RANK:]))
    wup = jnp.zeros((LANES, GROUP_W), F32).at[SMALL_WLO:SMALL_WLO + RWKV_RANK].set(w_up.astype(F32))
    aup = jnp.zeros((LANES, GROUP_W), F32).at[SMALL_ALO:SMALL_ALO + RWKV_RANK].set(a_up.astype(F32))
    vec = lambda a: a.astype(F32).reshape(1, GROUP_W)
    full = lambda shape: pl.BlockSpec(shape, lambda b, i: (0,) * len(shape))
    out_spec = pl.BlockSpec((1, ts, GROUP_W), lambda b, i: (b, i, 0))
    out_sds = jax.ShapeDtypeStruct((bsz, seq, GROUP_W), F32)
    return pl.pallas_call(
        _rwkv_prep_kernel,
        grid=(bsz, seq // ts),
        in_specs=[pl.BlockSpec((1, ts, 3 * GROUP_W), lambda b, i: (b, i, COL_D * LANES // (3 * GROUP_W))),
                  pl.BlockSpec((1, ts, LANES), lambda b, i: (b, i, COL_SMALL)),
                  full((1, 3 * GROUP_W)), full((1, LANES)), full((LANES, GROUP_W)), full((1, GROUP_W)),
                  full((LANES, GROUP_W)), full((1, GROUP_W)), full((3, GROUP_W))],
        out_specs=[out_spec] * 6,
        out_shape=[out_sds] * 6,
        scratch_shapes=[pltpu.VMEM((1, 3 * GROUP_W), F32), pltpu.VMEM((1, LANES), F32)],
        compiler_params=_cparams(("arbitrary", "arbitrary")),
    )(proj, proj, mu_u, mu_s, wup, vec(w0), aup, vec(a0), kkr.astype(F32))


def _pieces(x, n):
    out = []
    for i in range(n):
        p = x.astype(BF16)
        out.append(p)
        if i + 1 < n:
            x = x - p.astype(F32)
    return tuple(out)


def _mm(a, b, nt=False):
    n = max(len(a), len(b))
    dot = _dot_nt if nt else _dot
    acc = None
    for i, ai in enumerate(a):
        for j, bj in enumerate(b):
            if i + j < n:
                term = dot(ai, bj)
                acc = term if acc is None else acc + term
    return acc


def _neumann_inverses(mats):
    n = mats[0].shape[0]
    eye = (lax.broadcasted_iota(jnp.int32, (n, n), 0) == lax.broadcasted_iota(jnp.int32, (n, n), 1)).astype(F32)
    ps = [eye + a for a in mats]
    xs = [_pieces(a, RWKV_PIECES) for a in mats]
    for _ in range(int(math.log2(n)) - 1):
        xs = [_pieces(_mm(x, x), RWKV_PIECES) for x in xs]
        ps = [p + _mm(_pieces(p, RWKV_PIECES), x) for p, x in zip(ps, xs)]
    return ps


def _rwkv_scan_kernel(r_ref, lw_ref, k_ref, v_ref, aa_ref, b_ref, g_ref, lng_ref, lnb_ref, rk_ref, tri_ref,
                      o_ref, state):
    c = r_ref.shape[1]
    n1, ng = RWKV_PIECES, RWKV_PIECES_G

    @pl.when(pl.program_id(1) == 0)
    def _():
        state[...] = jnp.zeros_like(state)

    row = lax.broadcasted_iota(jnp.int32, (c, c), 0)
    col = lax.broadcasted_iota(jnp.int32, (c, c), 1)
    strict, incl = row > col, row >= col
    lo = _lane_lo((c, LANES))
    lo2 = _lane_lo((2 * c, LANES))
    diag_blocks = ((lax.broadcasted_iota(jnp.int32, (LANES, LANES), 0) < HEAD_DIM)
                   == (lax.broadcasted_iota(jnp.int32, (LANES, LANES), 1) < HEAD_DIM))
    tri = (tri_ref[...],)
    pairs = range(N_PAIRS)
    lanes = [slice(p * LANES, (p + 1) * LANES) for p in pairs]

    st = []
    for p in pairs:
        r, lw, k, v = (ref[0, :, lanes[p]] for ref in (r_ref, lw_ref, k_ref, v_ref))
        aa, b = aa_ref[0, :, lanes[p]], b_ref[0, :, lanes[p]]
        cl = _mm(tri, _pieces(lw, 3))
        cl_end = cl[c - 1:c, :]
        cl_mid = cl[c // 2 - 1:c // 2, :]
        w_inv = jnp.exp(cl_mid - cl)
        w_rest = jnp.exp(cl_end - cl)
        qa = jnp.concatenate([aa * jnp.exp(cl - lw - cl_mid), r * jnp.exp(cl - cl_mid)], axis=0)
        bk = _pieces(jnp.concatenate([b * w_inv, k * w_inv], axis=0), ng)
        qa_s = jnp.concatenate([aa * jnp.exp(cl - lw), r * jnp.exp(cl)], axis=0)
        bk_end = _pieces(jnp.concatenate([b * w_rest, k * w_rest], axis=0), n1)
        st.append(dict(r=r, k=k, v=v, qa=qa, bk=bk, qa_s=qa_s, bk_end=bk_end, w_end=jnp.exp(cl_end),
                       v_p=_pieces(v, n1)))

    bases = [_mm(_pieces(st[p]["qa_s"], n1), _pieces(state[p], n1), nt=True) for p in pairs]
    a_ab, a_ak, a_r = [], [], []
    for p in pairs:
        for h in range(2):
            g = _mm(_pieces(jnp.where(lo2 == (h == 0), st[p]["qa"], 0.0), ng), st[p]["bk"], nt=True)
            a_ab.append(jnp.where(strict, g[:c, :c], 0.0))
            a_ak.append(jnp.where(strict, g[:c, c:], 0.0))
            a_r.append(jnp.concatenate([jnp.where(incl, g[c:, :c], 0.0), jnp.where(incl, g[c:, c:], 0.0)], axis=1))
    rhs = [bases[i // 2][:c] + _mm(_pieces(a_ak[i], n1), st[i // 2]["v_p"]) for i in range(2 * N_PAIRS)]

    t_inv = _neumann_inverses(a_ab)
    u_heads = [_mm(_pieces(t_inv[i], n1), _pieces(rhs[i], n1)) for i in range(2 * N_PAIRS)]
    for p in pairs:
        r, k, v = st[p]["r"], st[p]["k"], st[p]["v"]
        u = jnp.where(lo, u_heads[2 * p], u_heads[2 * p + 1])
        uv = _pieces(jnp.concatenate([u, v], axis=0), n1)
        y = bases[p][c:] + jnp.where(lo, _mm(_pieces(a_r[2 * p], n1), uv), _mm(_pieces(a_r[2 * p + 1], n1), uv))
        upd = _mm(_pieces(jnp.concatenate([u.T, v.T], axis=1), n1), st[p]["bk_end"])
        state[p] = state[p] * st[p]["w_end"] + jnp.where(diag_blocks, upd, 0.0)

        mean = _seg_sum(y) * (1.0 / HEAD_DIM)
        yc = y - mean
        var = _seg_sum(yc * yc) * (1.0 / HEAD_DIM)
        out = yc * lax.rsqrt(var + RWKV_LN_EPS) * lng_ref[:, lanes[p]] + lnb_ref[:, lanes[p]]
        out = out + _seg_sum(r * k * rk_ref[:, lanes[p]]) * v
        o_ref[0, :, lanes[p]] = (out * _silu(g_ref[0, :, lanes[p]])).astype(o_ref.dtype)


def _rwkv_scan(proj, prepped, ln_gain, ln_bias, r_k):
    bsz, seq, _ = proj.shape
    c = min(RWKV_CHUNK, seq)
    tri = (lax.broadcasted_iota(jnp.int32, (c, c), 0) >= lax.broadcasted_iota(jnp.int32, (c, c), 1)).astype(BF16)
    tok_spec = pl.BlockSpec((1, c, GROUP_W), lambda b, i: (b, i, 0))
    vec_spec = pl.BlockSpec((1, GROUP_W), lambda b, i: (0, 0))
    vec = lambda a: a.astype(F32).reshape(1, GROUP_W)
    return pl.pallas_call(
        _rwkv_scan_kernel,
        grid=(bsz, seq // c),
        in_specs=[tok_spec] * 6
        + [pl.BlockSpec((1, c, GROUP_W), lambda b, i: (b, i, COL_DG * LANES // GROUP_W)),
           vec_spec, vec_spec, vec_spec, pl.BlockSpec((c, c), lambda b, i: (0, 0))],
        out_specs=tok_spec,
        out_shape=jax.ShapeDtypeStruct((bsz, seq, GROUP_W), BF16),
        scratch_shapes=[pltpu.VMEM((N_PAIRS, LANES, LANES), F32)],
        compiler_params=_cparams(("arbitrary", "arbitrary")),
    )(*prepped, proj, vec(ln_gain), vec(ln_bias), vec(r_k), tri)


def _out_kernel(a_ref, b_ref, c_ref, d_ref, w_ref, x_ref, o_ref):
    acc = x_ref[...]
    for i, m_ref in enumerate((a_ref, b_ref, c_ref, d_ref)):
        acc = acc + _dot(m_ref[...], w_ref[i * GROUP_W:(i + 1) * GROUP_W, :])
    o_ref[...] = acc


def _out_proj(groups, w_out, x2d):
    m, d = x2d.shape
    tm = min(512, m)
    g_spec = pl.BlockSpec((tm, GROUP_W), lambda i: (i, 0))
    return pl.pallas_call(
        _out_kernel,
        grid=(m // tm,),
        in_specs=[g_spec] * 4 + [pl.BlockSpec(w_out.shape, lambda i: (0, 0)),
                                 pl.BlockSpec((tm, d), lambda i: (i, 0))],
        out_specs=pl.BlockSpec((tm, d), lambda i: (i, 0)),
        out_shape=jax.ShapeDtypeStruct((m, d), F32),
        compiler_params=_cparams(("arbitrary",)),
    )(*[g.reshape(m, GROUP_W) for g in groups], w_out, x2d)


def _t5_causal_bucket(dist):
    max_exact = NUM_BUCKETS // 2
    d = jnp.maximum(dist, 1).astype(F32)
    large = max_exact + (jnp.log(d / max_exact) / math.log(MAX_DISTANCE / max_exact)
                         * (NUM_BUCKETS - max_exact)).astype(jnp.int32)
    large = jnp.minimum(large, NUM_BUCKETS - 1)
    return jnp.where(dist < max_exact, dist, large)


def _bias_tiles(rel_bias, seq, t):
    assert t >= MAX_DISTANCE
    bias_by_dist = rel_bias.astype(F32)[_t5_causal_bucket(jnp.arange(seq))]
    i = np.arange(t)
    idx = np.clip(np.arange(3)[:, None, None] * t + i[None, :, None] - i[None, None, :], 0, seq - 1)
    return jnp.transpose(bias_by_dist[idx], (3, 0, 1, 2))


def _pack_w_in(w):
    d = w.shape[0]
    att = w[:, :12 * GROUP_W]
    cf = w[:, 12 * GROUP_W:12 * GROUP_W + 8]
    off = 12 * GROUP_W + 8
    rkv = w[:, off:off + 3 * GROUP_W]
    w_lo = w[:, off + 3 * GROUP_W:off + 3 * GROUP_W + RWKV_RANK]
    a_lo = w[:, off + 3 * GROUP_W + RWKV_RANK:off + 3 * GROUP_W + 2 * RWKV_RANK]
    dg = w[:, off + 3 * GROUP_W + 2 * RWKV_RANK:]
    small = (jnp.zeros((d, LANES), w.dtype).at[:, :8].set(cf)
             .at[:, SMALL_WLO:SMALL_WLO + RWKV_RANK].set(w_lo)
             .at[:, SMALL_ALO:SMALL_ALO + RWKV_RANK].set(a_lo))
    return jnp.concatenate([att, rkv, dg, small], axis=1).astype(BF16)


def kernel(x, norm_gain, w_in, w_out, rel_bias, qk_gain, diff_lambda, forget_bias, out_gain, rwkv_mu, rwkv_w_up,
           rwkv_w0, rwkv_a_up, rwkv_a0, rwkv_kkr, rwkv_ln_gain, rwkv_ln_bias):
    bsz, seq, d = x.shape
    tz = _bias_tiles(rel_bias, seq, min(ATT_TILE, seq))
    h = x.reshape(bsz * seq, d)
    for l in range(DEPTH):
        og = out_gain[l].astype(F32)
        proj = _proj(h, norm_gain[l].astype(F32), _pack_w_in(w_in[l])).reshape(bsz, seq, N_PROJ)
        lam_init = 0.8 - 0.6 * math.exp(-0.3 * l)
        oa = _attn_a(proj, tz, qk_gain[l, 0], qk_gain[l, 1], og[:GROUP_W], diff_lambda[l].astype(F32), lam_init)
        ob = _attn_b(proj, og[GROUP_W:2 * GROUP_W])
        cum_f = _fox_cum(proj, forget_bias[l])
        oc = _attn_c(proj, cum_f, qk_gain[l, 2], qk_gain[l, 3], og[2 * GROUP_W:3 * GROUP_W])
        prepped = _rwkv_prep(proj, rwkv_mu[l], rwkv_w_up[l], rwkv_w0[l], rwkv_a_up[l], rwkv_a0[l], rwkv_kkr[l])
        od = _rwkv_scan(proj, prepped, rwkv_ln_gain[l].astype(F32) * og[3 * GROUP_W:], rwkv_ln_bias[l],
                        rwkv_kkr[l, 2])
        h = _out_proj((oa, ob, oc, od), w_out[l].astype(BF16), h)
    return h.reshape(bsz, seq, d)
```

```python
import functools
import math

import numpy as np
import jax
import jax.numpy as jnp
from jax import lax
from jax.experimental import pallas as pl
from jax.experimental.pallas import tpu as pltpu

F32 = jnp.float32
BF16 = jnp.bfloat16
HIGHEST = lax.Precision.HIGHEST

D_MODEL = 1024
DEPTH = 2
GROUP_W = 512
HEAD_DIM = 64
H_A = 4
NUM_BUCKETS = 32
MAX_DISTANCE = 128
RWKV_RANK = 32
NORM_EPS = 1e-6
RWKV_LN_EPS = 64e-5
NEG_INF = -1e30

LANES = 128
N_PAIRS = GROUP_W // LANES
ATT_SCALE = HEAD_DIM ** -0.5
LOG2E = math.log2(math.e)

COL_A, COL_B, COL_C = 0, 16, 32
COL_D = 48
COL_DG = 60
COL_SMALL = 64
N_PROJ = (COL_SMALL + 1) * LANES
SMALL_WLO, SMALL_ALO = 32, 64

ATT_TILE = 512
STICK_BLOCK = 256
STICK_DEAD_LOG2 = -152.0
PREP_ROWS = 512
RWKV_CHUNK = 128
RWKV_PIECES = 1
RWKV_PIECES_G = 2
VMEM_LIMIT = 48 * 1024 * 1024


def _cparams(sem):
    return pltpu.CompilerParams(dimension_semantics=sem, vmem_limit_bytes=VMEM_LIMIT)


def _dot(a, b, **kw):
    return jnp.dot(a, b, preferred_element_type=F32, **kw)


def _dot_nt(a, b, **kw):
    return lax.dot_general(a, b, (((1,), (1,)), ((), ())), preferred_element_type=F32, **kw)


def _lane_lo(shape):
    return lax.broadcasted_iota(jnp.int32, shape, len(shape) - 1) < HEAD_DIM


def _seg_sum(x):
    lo = _lane_lo(x.shape)
    s_lo = jnp.sum(jnp.where(lo, x, 0.0), axis=-1, keepdims=True)
    s_hi = jnp.sum(jnp.where(lo, 0.0, x), axis=-1, keepdims=True)
    return jnp.where(lo, s_lo, s_hi)


def _seg_rms(x, gain):
    return x * lax.rsqrt(_seg_sum(x * x) * (1.0 / HEAD_DIM) + NORM_EPS) * gain


def _log_sigmoid(z):
    return jnp.minimum(z, 0.0) - jnp.log1p(jnp.exp(-jnp.abs(z)))


def _silu(g):
    return g / (1.0 + jnp.exp(-g))


def _proj_kernel(x_ref, g_ref, w_ref, o_ref, h_ref):
    @pl.when(pl.program_id(1) == 0)
    def _():
        x = x_ref[...]
        ms = jnp.mean(x * x, axis=-1, keepdims=True)
        h_ref[...] = (x * lax.rsqrt(ms + NORM_EPS) * g_ref[...]).astype(BF16)

    o_ref[...] = _dot(h_ref[...], w_ref[...])


def _proj(x2d, gain, w_packed):
    m, d = x2d.shape
    n = w_packed.shape[1]
    tm = min(1024, m)
    tn = n // 5
    return pl.pallas_call(
        _proj_kernel,
        grid=(m // tm, n // tn),
        in_specs=[pl.BlockSpec((tm, d), lambda i, j: (i, 0)),
                  pl.BlockSpec((1, d), lambda i, j: (0, 0)),
                  pl.BlockSpec((d, tn), lambda i, j: (0, j))],
        out_specs=pl.BlockSpec((tm, tn), lambda i, j: (i, j)),
        out_shape=jax.ShapeDtypeStruct((m, n), F32),
        scratch_shapes=[pltpu.VMEM((tm, d), BF16)],
        compiler_params=_cparams(("arbitrary", "arbitrary")),
    )(x2d, gain.reshape(1, d), w_packed)


def _for_row_blocks(seq, fn):
    rows_per = min(PREP_ROWS, seq)

    def body(c, carry):
        fn(pl.ds(pl.multiple_of(c * rows_per, rows_per), rows_per), rows_per)
        return carry

    lax.fori_loop(0, seq // rows_per, body, 0)


def _head_masked(q):
    lo = _lane_lo(q.shape)
    return jnp.where(lo, q, 0.0).astype(BF16), jnp.where(lo, 0.0, q).astype(BF16)


def _online_softmax(scores, carry, values):
    m_new = [jnp.maximum(m, jnp.max(s, axis=-1, keepdims=True)) for s, (m, _) in zip(scores, carry)]
    probs = [jnp.exp2(s - m).astype(BF16) for s, m in zip(scores, m_new)]
    return tuple((mn, jnp.exp2(m - mn) * acc + _dot(p, v))
                 for p, mn, (m, acc), v in zip(probs, m_new, carry, values))


def _softmax_init(t, n):
    return (jnp.full((t, 1), NEG_INF, F32), jnp.zeros((t, n), F32))


def _attn_a_kernel(q_ref, k_ref, v_ref, g_ref, tz_ref, qg_ref, kg_ref, og_ref, dl_ref, o_ref, kbf, vbf,
                   *, lam_init):
    qi = pl.program_id(2)
    t = q_ref.shape[1]

    @pl.when(qi == 0)
    def _():
        def stage(rows, n):
            kbf[rows, :] = _seg_rms(k_ref[0, rows, :], kg_ref[...]).astype(BF16)
            vbf[rows, :LANES] = v_ref[0, rows, :].astype(BF16)
            vbf[rows, LANES:] = jnp.ones((n, LANES), BF16)

        _for_row_blocks(k_ref.shape[1], stage)

    qs = _head_masked(_seg_rms(q_ref[0], qg_ref[...]) * (ATT_SCALE * LOG2E))
    row = lax.broadcasted_iota(jnp.int32, (t, t), 0)
    col = lax.broadcasted_iota(jnp.int32, (t, t), 1)

    def tile(j, carry, diag):
        ks = pl.ds(pl.multiple_of(j * t, t), t)
        kt, vt = kbf[ks, :], vbf[ks, :]
        bias = tz_ref[0, jnp.minimum(qi - j, 2)]
        scores = [_dot_nt(q, kt) + bias for q in qs]
        if diag:
            scores = [jnp.where(row >= col, s, NEG_INF) for s in scores]
        return _online_softmax(scores, carry, (vt, vt))

    carry = lax.fori_loop(0, qi, lambda j, c: tile(j, c, False), (_softmax_init(t, 2 * LANES),) * 2)
    (_, acc0), (_, acc1) = tile(qi, carry, True)

    dl = dl_ref[...]
    lam = (jnp.exp(jnp.sum(dl[0:1] * dl[1:2], axis=-1, keepdims=True))
           - jnp.exp(jnp.sum(dl[2:3] * dl[3:4], axis=-1, keepdims=True)) + lam_init)
    o = acc0[:, :LANES] / acc0[:, LANES:] - lam * (acc1[:, :LANES] / acc1[:, LANES:])
    y = o * lax.rsqrt(jnp.mean(o * o, axis=-1, keepdims=True) + NORM_EPS) * og_ref[0] * (1.0 - lam_init)
    o_ref[0] = (y * _silu(g_ref[0])).astype(o_ref.dtype)


def _attn_c_kernel(q_ref, k_ref, v_ref, g_ref, f_ref, qg_ref, kg_ref, og_ref, o_ref, kbf, vbf0, vbf1):
    qi = pl.program_id(2)
    t = q_ref.shape[1]

    @pl.when(qi == 0)
    def _():
        def stage(rows, n):
            kbf[rows, :] = _seg_rms(k_ref[0, rows, :], kg_ref[...]).astype(BF16)
            v = v_ref[0, rows, :]
            lo = _lane_lo(v.shape)
            vbf0[rows, :] = jnp.where(lo, v, 1.0).astype(BF16)
            vbf1[rows, :] = jnp.where(lo, 1.0, v).astype(BF16)

        _for_row_blocks(k_ref.shape[1], stage)

    qs = _head_masked(_seg_rms(q_ref[0], qg_ref[...]) * (ATT_SCALE * LOG2E))
    row = lax.broadcasted_iota(jnp.int32, (t, t), 0)
    col = lax.broadcasted_iota(jnp.int32, (t, t), 1)

    def tile(j, carry, diag):
        ks = pl.ds(pl.multiple_of(j * t, t), t)
        kt = kbf[ks, :]
        scores = [_dot_nt(q, kt) - f_ref[0, 0, h:h + 1, ks] * LOG2E for h, q in enumerate(qs)]
        if diag:
            scores = [jnp.where(row >= col, s, NEG_INF) for s in scores]
        return _online_softmax(scores, carry, (vbf0[ks, :], vbf1[ks, :]))

    carry = lax.fori_loop(0, qi, lambda j, c: tile(j, c, False), (_softmax_init(t, LANES),) * 2)
    (_, acc0), (_, acc1) = tile(qi, carry, True)
    lo = _lane_lo(acc0.shape)
    denom = jnp.where(lo, pltpu.roll(acc0, HEAD_DIM, 1), pltpu.roll(acc1, HEAD_DIM, 1))
    o = jnp.where(lo, acc0, acc1) / denom
    o_ref[0] = (_seg_rms(o, og_ref[0]) * _silu(g_ref[0])).astype(o_ref.dtype)


def _attn_b_kernel(q_ref, k_ref, v_ref, g_ref, u_ref, og_ref, o_ref, kbf, vbf):
    qi = pl.program_id(2)
    t = q_ref.shape[1]
    blk = u_ref.shape[0]

    @pl.when(qi == 0)
    def _():
        def stage(rows, n):
            kbf[rows, :] = k_ref[0, rows, :].astype(BF16)
            vbf[rows, :] = v_ref[0, rows, :].astype(BF16)

        _for_row_blocks(k_ref.shape[1], stage)

    qs = _head_masked(q_ref[0] * (ATT_SCALE * LOG2E))
    row = lax.broadcasted_iota(jnp.int32, (t, t), 0)
    col = lax.broadcasted_iota(jnp.int32, (t, t), 1)
    umat = u_ref[...]

    def tile(j, carry, diag):
        ks = pl.ds(pl.multiple_of(j * t, t), t)
        kt, vt = kbf[ks, :], vbf[ks, :]
        zs = [_dot_nt(q, kt) for q in qs]
        l1s = [-(jnp.maximum(z, 0.0) + jnp.log2(1.0 + jnp.exp2(-jnp.abs(z)))) for z in zs]
        if diag:
            l1s = [jnp.where(row > col, l1, 0.0) for l1 in l1s]
        tails = [c for c, _ in carry]
        afters = [[] for _ in qs]
        for b in reversed(range(t // blk)):
            for i, l1 in enumerate(l1s):
                x = l1[:, b * blk:(b + 1) * blk]
                hi = x.astype(BF16)
                lo = (x - hi.astype(F32)).astype(BF16)
                afters[i].insert(0, _dot(hi, umat) + _dot(lo, umat) + tails[i])
                tails[i] = tails[i] + jnp.sum(x, axis=-1, keepdims=True)
        out = []
        for z, l1, after, c_new, (_, acc) in zip(zs, l1s, afters, tails, carry):
            a = jnp.exp2(l1 + z + jnp.concatenate(after, axis=1))
            if diag:
                a = jnp.where(row > col, a, 0.0)
            out.append((c_new, acc + _dot(a.astype(BF16), vt)))
        return tuple(out)

    def alive(state):
        i, carry = state
        live = jnp.maximum(jnp.max(carry[0][0]), jnp.max(carry[1][0])) > STICK_DEAD_LOG2
        return jnp.logical_and(i < qi, live)

    init = ((jnp.zeros((t, 1), F32), jnp.zeros((t, LANES), F32)),) * 2
    _, ((_, a0), (_, a1)) = lax.while_loop(alive, lambda st: (st[0] + 1, tile(qi - 1 - st[0], st[1], False)),
                                           (jnp.int32(0), tile(qi, init, True)))
    o = jnp.where(_lane_lo(a0.shape), a0, a1)
    o_ref[0] = (_seg_rms(o, og_ref[0]) * _silu(g_ref[0])).astype(o_ref.dtype)


def _attn_specs(seq, t, col):
    q_spec = pl.BlockSpec((1, t, LANES), lambda b, p, i: (b, i, col + p))
    k_spec = pl.BlockSpec((1, seq, LANES), lambda b, p, i: (b, 0, col + N_PAIRS + p))
    v_spec = pl.BlockSpec((1, seq, LANES), lambda b, p, i: (b, 0, col + 2 * N_PAIRS + p))
    g_spec = pl.BlockSpec((1, t, LANES), lambda b, p, i: (b, i, col + 3 * N_PAIRS + p))
    return [q_spec, k_spec, v_spec, g_spec]


def _pair_spec():
    return pl.BlockSpec((1, 1, LANES), lambda b, p, i: (p, 0, 0))


def _const_spec(shape):
    return pl.BlockSpec(shape, lambda b, p, i: (0,) * len(shape))


def _attn_call(body, proj, col, extra_specs, extra_args, v_lanes=(LANES,)):
    bsz, seq, _ = proj.shape
    t = min(ATT_TILE, seq)
    return pl.pallas_call(
        body,
        grid=(bsz, N_PAIRS, seq // t),
        in_specs=_attn_specs(seq, t, col) + extra_specs,
        out_specs=pl.BlockSpec((1, t, LANES), lambda b, p, i: (b, i, p)),
        out_shape=jax.ShapeDtypeStruct((bsz, seq, GROUP_W), BF16),
        scratch_shapes=[pltpu.VMEM((seq, LANES), BF16)] + [pltpu.VMEM((seq, n), BF16) for n in v_lanes],
        compiler_params=_cparams(("arbitrary", "arbitrary", "arbitrary")),
    )(proj, proj, proj, proj, *extra_args)


def _pair_gain(g64):
    return jnp.tile(g64.astype(F32), 2).reshape(1, LANES)


def _attn_a(proj, tz, q_gain, k_gain, o_gain, dlam, lam_init):
    t = tz.shape[-1]
    specs = [pl.BlockSpec((1, 3, t, t), lambda b, p, i: (p, 0, 0, 0)),
             _const_spec((1, LANES)), _const_spec((1, LANES)), _pair_spec(), _const_spec((4, HEAD_DIM))]
    args = (tz, _pair_gain(q_gain), _pair_gain(k_gain), o_gain.reshape(N_PAIRS, 1, LANES), dlam)
    return _attn_call(functools.partial(_attn_a_kernel, lam_init=lam_init), proj, COL_A, specs, args,
                      v_lanes=(2 * LANES,))


def _attn_b(proj, o_gain):
    t = min(STICK_BLOCK, proj.shape[1])
    umat = (lax.broadcasted_iota(jnp.int32, (t, t), 0) > lax.broadcasted_iota(jnp.int32, (t, t), 1)).astype(BF16)
    specs = [_const_spec((t, t)), _pair_spec()]
    return _attn_call(_attn_b_kernel, proj, COL_B, specs, (umat, o_gain.reshape(N_PAIRS, 1, LANES)))


def _attn_c(proj, cum_f, q_gain, k_gain, o_gain):
    seq = proj.shape[1]
    specs = [pl.BlockSpec((1, 1, 2, seq), lambda b, p, i: (b, p, 0, 0)),
             _const_spec((1, LANES)), _const_spec((1, LANES)), _pair_spec()]
    args = (cum_f, _pair_gain(q_gain), _pair_gain(k_gain), o_gain.reshape(N_PAIRS, 1, LANES))
    return _attn_call(_attn_c_kernel, proj, COL_C, specs, args, v_lanes=(LANES, LANES))


def _fox_cum_kernel(s_ref, fb_ref, tri_ref, o_ref):
    seq = s_ref.shape[1]
    rows_per = tri_ref.shape[0]
    carry = jnp.zeros((LANES, 1), F32)
    for c in range(seq // rows_per):
        rows = slice(c * rows_per, (c + 1) * rows_per)
        log_f = _log_sigmoid(s_ref[0, rows, :] + fb_ref[...])
        log_f_t = log_f.T
        cum = _dot(log_f_t, tri_ref[...], precision=HIGHEST) + carry
        o_ref[0, :, rows] = cum[:8, :]
        carry = carry + jnp.sum(log_f_t, axis=-1, keepdims=True)


def _fox_cum(proj, forget_bias):
    bsz, seq, _ = proj.shape
    rows_per = min(PREP_ROWS, seq)
    tri = (lax.broadcasted_iota(jnp.int32, (rows_per, rows_per), 0)
           <= lax.broadcasted_iota(jnp.int32, (rows_per, rows_per), 1)).astype(F32)
    fb = jnp.zeros((1, LANES), F32).at[0, :forget_bias.shape[0]].set(forget_bias.astype(F32))
    out = pl.pallas_call(
        _fox_cum_kernel,
        grid=(bsz,),
        in_specs=[pl.BlockSpec((1, seq, LANES), lambda b: (b, 0, COL_SMALL)),
                  pl.BlockSpec((1, LANES), lambda b: (0, 0)),
                  pl.BlockSpec((rows_per, rows_per), lambda b: (0, 0))],
        out_specs=pl.BlockSpec((1, 8, seq), lambda b: (b, 0, 0)),
        out_shape=jax.ShapeDtypeStruct((bsz, 8, seq), F32),
        compiler_params=_cparams(("arbitrary",)),
    )(proj, fb, tri)
    return out.reshape(bsz, N_PAIRS, 2, seq)


def _rwkv_prep_kernel(u_ref, s_ref, mu_ref, mus_ref, wup_ref, w0_ref, aup_ref, a0_ref, kkr_ref,
                      r_o, lw_o, k_o, v_o, aa_o, b_o, prev_u, prev_s):
    @pl.when(pl.program_id(1) == 0)
    def _():
        prev_u[...] = jnp.zeros_like(prev_u)
        prev_s[...] = jnp.zeros_like(prev_s)

    def shift(x, prev, mu):
        first = lax.broadcasted_iota(jnp.int32, x.shape, 0) == 0
        x_prev = jnp.where(first, prev[...], pltpu.roll(x, 1, 0))
        prev[...] = x[x.shape[0] - 1:, :]
        return x + (x_prev - x) * mu

    u = shift(u_ref[0], prev_u, mu_ref[...])
    sm = shift(s_ref[0], prev_s, mus_ref[...])
    r, k, v = u[:, :GROUP_W], u[:, GROUP_W:2 * GROUP_W], u[:, 2 * GROUP_W:]
    w_pre = w0_ref[...] + _dot(jnp.tanh(sm), wup_ref[...], precision=HIGHEST)
    w_log = _log_sigmoid(w_pre) - 0.5
    a = 1.0 / (1.0 + jnp.exp(-(a0_ref[...] + _dot(sm, aup_ref[...], precision=HIGHEST))))
    kk = k * kkr_ref[0:1, :]
    kk_n = jnp.concatenate(
        [kk[:, i * LANES:(i + 1) * LANES]
         / jnp.maximum(jnp.sqrt(_seg_sum(kk[:, i * LANES:(i + 1) * LANES] ** 2)), 1e-12)
         for i in range(N_PAIRS)], axis=-1)
    r_o[0] = r
    lw_o[0] = -jnp.exp(w_log)
    k_o[0] = k * (1.0 + (a - 1.0) * kkr_ref[1:2, :])
    v_o[0] = v
    aa_o[0] = -kk_n
    b_o[0] = kk_n * a


def _rwkv_prep(proj, mu, w_up, w0, a_up, a0, kkr):
    bsz, seq, _ = proj.shape
    ts = min(PREP_ROWS, seq)
    mu = mu.astype(F32)
    mu_u = mu[:3 * GROUP_W].reshape(1, 3 * GROUP_W)
    mu_s = (jnp.zeros((1, LANES), F32)
            .at[0, SMALL_WLO:SMALL_WLO + RWKV_RANK].set(mu[3 * GROUP_W:3 * GROUP_W + RWKV_RANK])
            .at[0, SMALL_ALO:SMALL_ALO + RWKV_RANK].set(mu[3 * GROUP_W + RWKV_RANK:]))
    wup = jnp.zeros((LANES, GROUP_W), F32).at[SMALL_WLO:SMALL_WLO + RWKV_RANK].set(w_up.astype(F32))
    aup = jnp.zeros((LANES, GROUP_W), F32).at[SMALL_ALO:SMALL_ALO + RWKV_RANK].set(a_up.astype(F32))
    vec = lambda a: a.astype(F32).reshape(1, GROUP_W)
    full = lambda shape: pl.BlockSpec(shape, lambda b, i: (0,) * len(shape))
    out_spec = pl.BlockSpec((1, ts, GROUP_W), lambda b, i: (b, i, 0))
    out_sds = jax.ShapeDtypeStruct((bsz, seq, GROUP_W), F32)
    return pl.pallas_call(
        _rwkv_prep_kernel,
        grid=(bsz, seq // ts),
        in_specs=[pl.BlockSpec((1, ts, 3 * GROUP_W), lambda b, i: (b, i, COL_D * LANES // (3 * GROUP_W))),
                  pl.BlockSpec((1, ts, LANES), lambda b, i: (b, i, COL_SMALL)),
                  full((1, 3 * GROUP_W)), full((1, LANES)), full((LANES, GROUP_W)), full((1, GROUP_W)),
                  full((LANES, GROUP_W)), full((1, GROUP_W)), full((3, GROUP_W))],
        out_specs=[out_spec] * 6,
        out_shape=[out_sds] * 6,
        scratch_shapes=[pltpu.VMEM((1, 3 * GROUP_W), F32), pltpu.VMEM((1, LANES), F32)],
        compiler_params=_cparams(("arbitrary", "arbitrary")),
    )(proj, proj, mu_u, mu_s, wup, vec(w0), aup, vec(a0), kkr.astype(F32))


def _pieces(x, n):
    out = []
    for i in range(n):
        p = x.astype(BF16)
        out.append(p)
        if i + 1 < n:
            x = x - p.astype(F32)
    return tuple(out)


def _mm(a, b, nt=False):
    n = max(len(a), len(b))
    dot = _dot_nt if nt else _dot
    acc = None
    for i, ai in enumerate(a):
        for j, bj in enumerate(b):
            if i + j < n:
                term = dot(ai, bj)
                acc = term if acc is None else acc + term
    return acc


def _neumann_inverses(mats):
    n = mats[0].shape[0]
    eye = (lax.broadcasted_iota(jnp.int32, (n, n), 0) == lax.broadcasted_iota(jnp.int32, (n, n), 1)).astype(F32)
    ps = [eye + a for a in mats]
    xs = [_pieces(a, RWKV_PIECES) for a in mats]
    for _ in range(int(math.log2(n)) - 1):
        xs = [_pieces(_mm(x, x), RWKV_PIECES) for x in xs]
        ps = [p + _mm(_pieces(p, RWKV_PIECES), x) for p, x in zip(ps, xs)]
    return ps


def _rwkv_scan_kernel(r_ref, lw_ref, k_ref, v_ref, aa_ref, b_ref, g_ref, lng_ref, lnb_ref, rk_ref, tri_ref,
                      o_ref, state):
    c = r_ref.shape[1]
    n1, ng = RWKV_PIECES, RWKV_PIECES_G

    @pl.when(pl.program_id(1) == 0)
    def _():
        state[...] = jnp.zeros_like(state)

    row = lax.broadcasted_iota(jnp.int32, (c, c), 0)
    col = lax.broadcasted_iota(jnp.int32, (c, c), 1)
    strict, incl = row > col, row >= col
    lo = _lane_lo((c, LANES))
    lo2 = _lane_lo((2 * c, LANES))
    diag_blocks = ((lax.broadcasted_iota(jnp.int32, (LANES, LANES), 0) < HEAD_DIM)
                   == (lax.broadcasted_iota(jnp.int32, (LANES, LANES), 1) < HEAD_DIM))
    tri = (tri_ref[...],)
    pairs = range(N_PAIRS)
    lanes = [slice(p * LANES, (p + 1) * LANES) for p in pairs]

    st = []
    for p in pairs:
        r, lw, k, v = (ref[0, :, lanes[p]] for ref in (r_ref, lw_ref, k_ref, v_ref))
        aa, b = aa_ref[0, :, lanes[p]], b_ref[0, :, lanes[p]]
        cl = _mm(tri, _pieces(lw, 3))
        cl_end = cl[c - 1:c, :]
        cl_mid = cl[c // 2 - 1:c // 2, :]
        w_inv = jnp.exp(cl_mid - cl)
        w_rest = jnp.exp(cl_end - cl)
        qa = jnp.concatenate([aa * jnp.exp(cl - lw - cl_mid), r * jnp.exp(cl - cl_mid)], axis=0)
        bk = _pieces(jnp.concatenate([b * w_inv, k * w_inv], axis=0), ng)
        qa_s = jnp.concatenate([aa * jnp.exp(cl - lw), r * jnp.exp(cl)], axis=0)
        bk_end = _pieces(jnp.concatenate([b * w_rest, k * w_rest], axis=0), n1)
        st.append(dict(r=r, k=k, v=v, qa=qa, bk=bk, qa_s=qa_s, bk_end=bk_end, w_end=jnp.exp(cl_end),
                       v_p=_pieces(v, n1)))

    bases = [_mm(_pieces(st[p]["qa_s"], n1), _pieces(state[p], n1), nt=True) for p in pairs]
    a_ab, a_ak, a_r = [], [], []
    for p in pairs:
        for h in range(2):
            g = _mm(_pieces(jnp.where(lo2 == (h == 0), st[p]["qa"], 0.0), ng), st[p]["bk"], nt=True)
            a_ab.append(jnp.where(strict, g[:c, :c], 0.0))
            a_ak.append(jnp.where(strict, g[:c, c:], 0.0))
            a_r.append(jnp.concatenate([jnp.where(incl, g[c:, :c], 0.0), jnp.where(incl, g[c:, c:], 0.0)], axis=1))
    rhs = [bases[i // 2][:c] + _mm(_pieces(a_ak[i], n1), st[i // 2]["v_p"]) for i in range(2 * N_PAIRS)]

    t_inv = _neumann_inverses(a_ab)
    u_heads = [_mm(_pieces(t_inv[i], n1), _pieces(rhs[i], n1)) for i in range(2 * N_PAIRS)]
    for p in pairs:
        r, k, v = st[p]["r"], st[p]["k"], st[p]["v"]
        u = jnp.where(lo, u_heads[2 * p], u_heads[2 * p + 1])
        uv = _pieces(jnp.concatenate([u, v], axis=0), n1)
        y = bases[p][c:] + jnp.where(lo, _mm(_pieces(a_r[2 * p], n1), uv), _mm(_pieces(a_r[2 * p + 1], n1), uv))
        upd = _mm(_pieces(jnp.concatenate([u.T, v.T], axis=1), n1), st[p]["bk_end"])
        state[p] = state[p] * st[p]["w_end"] + jnp.where(diag_blocks, upd, 0.0)

        mean = _seg_sum(y) * (1.0 / HEAD_DIM)
        yc = y - mean
        var = _seg_sum(yc * yc) * (1.0 / HEAD_DIM)
        out = yc * lax.rsqrt(var + RWKV_LN_EPS) * lng_ref[:, lanes[p]] + lnb_ref[:, lanes[p]]
        out = out + _seg_sum(r * k * rk_ref[:, lanes[p]]) * v
        o_ref[0, :, lanes[p]] = (out * _silu(g_ref[0, :, lanes[p]])).astype(o_ref.dtype)


def _rwkv_scan(proj, prepped, ln_gain, ln_bias, r_k):
    bsz, seq, _ = proj.shape
    c = min(RWKV_CHUNK, seq)
    tri = (lax.broadcasted_iota(jnp.int32, (c, c), 0) >= lax.broadcasted_iota(jnp.int32, (c, c), 1)).astype(BF16)
    tok_spec = pl.BlockSpec((1, c, GROUP_W), lambda b, i: (b, i, 0))
    vec_spec = pl.BlockSpec((1, GROUP_W), lambda b, i: (0, 0))
    vec = lambda a: a.astype(F32).reshape(1, GROUP_W)
    return pl.pallas_call(
        _rwkv_scan_kernel,
        grid=(bsz, seq // c),
        in_specs=[tok_spec] * 6
        + [pl.BlockSpec((1, c, GROUP_W), lambda b, i: (b, i, COL_DG * LANES // GROUP_W)),
           vec_spec, vec_spec, vec_spec, pl.BlockSpec((c, c), lambda b, i: (0, 0))],
        out_specs=tok_spec,
        out_shape=jax.ShapeDtypeStruct((bsz, seq, GROUP_W), BF16),
        scratch_shapes=[pltpu.VMEM((N_PAIRS, LANES, LANES), F32)],
        compiler_params=_cparams(("arbitrary", "arbitrary")),
    )(*prepped, proj, vec(ln_gain), vec(ln_bias), vec(r_k), tri)


def _out_kernel(a_ref, b_ref, c_ref, d_ref, w_ref, x_ref, o_ref):
    acc = x_ref[...]
    for i, m_ref in enumerate((a_ref, b_ref, c_ref, d_ref)):
        acc = acc + _dot(m_ref[...], w_ref[i * GROUP_W:(i + 1) * GROUP_W, :])
    o_ref[...] = acc


def _out_proj(groups, w_out, x2d):
    m, d = x2d.shape
    tm = min(512, m)
    g_spec = pl.BlockSpec((tm, GROUP_W), lambda i: (i, 0))
    return pl.pallas_call(
        _out_kernel,
        grid=(m // tm,),
        in_specs=[g_spec] * 4 + [pl.BlockSpec(w_out.shape, lambda i: (0, 0)),
                                 pl.BlockSpec((tm, d), lambda i: (i, 0))],
        out_specs=pl.BlockSpec((tm, d), lambda i: (i, 0)),
        out_shape=jax.ShapeDtypeStruct((m, d), F32),
        compiler_params=_cparams(("arbitrary",)),
    )(*[g.reshape(m, GROUP_W) for g in groups], w_out, x2d)


def _t5_causal_bucket(dist):
    max_exact = NUM_BUCKETS // 2
    d = jnp.maximum(dist, 1).astype(F32)
    large = max_exact + (jnp.log(d / max_exact) / math.log(MAX_DISTANCE / max_exact)
                         * (NUM_BUCKETS - max_exact)).astype(jnp.int32)
    large = jnp.minimum(large, NUM_BUCKETS - 1)
    return jnp.where(dist < max_exact, dist, large)


def _bias_tiles(rel_bias, seq, t):
    assert t >= MAX_DISTANCE
    bias_by_dist = rel_bias.astype(F32)[_t5_causal_bucket(jnp.arange(seq))] * LOG2E
    m = np.arange(2 * t)
    tiles = []
    for delta in range(3):
        dist = np.where(m < t, delta * t - m, delta * t + 2 * t - m)
        v = bias_by_dist[np.clip(dist, 0, seq - 1)].T
        skew = jnp.tile(v, (1, t))[:, :t * (2 * t - 1)].reshape(-1, t, 2 * t - 1)
        tiles.append(skew[:, :, :t])
    return jnp.stack(tiles, axis=1)


def _pack_w_in(w):
    d = w.shape[0]
    att = w[:, :12 * GROUP_W]
    cf = w[:, 12 * GROUP_W:12 * GROUP_W + 8]
    off = 12 * GROUP_W + 8
    rkv = w[:, off:off + 3 * GROUP_W]
    w_lo = w[:, off + 3 * GROUP_W:off + 3 * GROUP_W + RWKV_RANK]
    a_lo = w[:, off + 3 * GROUP_W + RWKV_RANK:off + 3 * GROUP_W + 2 * RWKV_RANK]
    dg = w[:, off + 3 * GROUP_W + 2 * RWKV_RANK:]
    small = (jnp.zeros((d, LANES), w.dtype).at[:, :8].set(cf)
             .at[:, SMALL_WLO:SMALL_WLO + RWKV_RANK].set(w_lo)
             .at[:, SMALL_ALO:SMALL_ALO + RWKV_RANK].set(a_lo))
    return jnp.concatenate([att, rkv, dg, small], axis=1).astype(BF16)


def kernel(x, norm_gain, w_in, w_out, rel_bias, qk_gain, diff_lambda, forget_bias, out_gain, rwkv_mu, rwkv_w_up,
           rwkv_w0, rwkv_a_up, rwkv_a0, rwkv_kkr, rwkv_ln_gain, rwkv_ln_bias):
    bsz, seq, d = x.shape
    tz = _bias_tiles(rel_bias, seq, min(ATT_TILE, seq))
    h = x.reshape(bsz * seq, d)
    for l in range(DEPTH):
        og = out_gain[l].astype(F32)
        proj = _proj(h, norm_gain[l].astype(F32), _pack_w_in(w_in[l])).reshape(bsz, seq, N_PROJ)
        lam_init = 0.8 - 0.6 * math.exp(-0.3 * l)
        oa = _attn_a(proj, tz, qk_gain[l, 0], qk_gain[l, 1], og[:GROUP_W], diff_lambda[l].astype(F32), lam_init)
        ob = _attn_b(proj, og[GROUP_W:2 * GROUP_W])
        cum_f = _fox_cum(proj, forget_bias[l])
        oc = _attn_c(proj, cum_f, qk_gain[l, 2], qk_gain[l, 3], og[2 * GROUP_W:3 * GROUP_W])
        prepped = _rwkv_prep(proj, rwkv_mu[l], rwkv_w_up[l], rwkv_w0[l], rwkv_a_up[l], rwkv_a0[l], rwkv_kkr[l])
        od = _rwkv_scan(proj, prepped, rwkv_ln_gain[l].astype(F32) * og[3 * GROUP_W:], rwkv_ln_bias[l],
                        rwkv_kkr[l, 2])
        h = _out_proj((oa, ob, oc, od), w_out[l].astype(BF16), h)
    return h.reshape(bsz, seq, d)
```

```python
import functools
import math

import numpy as np
import jax
import jax.numpy as jnp
from jax import lax
from jax.experimental import pallas as pl
from jax.experimental.pallas import tpu as pltpu

F32 = jnp.float32
BF16 = jnp.bfloat16
HIGHEST = lax.Precision.HIGHEST

D_MODEL = 1024
DEPTH = 2
GROUP_W = 512
HEAD_DIM = 64
H_A = 4
NUM_BUCKETS = 32
MAX_DISTANCE = 128
RWKV_RANK = 32
NORM_EPS = 1e-6
RWKV_LN_EPS = 64e-5
NEG_INF = -1e30

LANES = 128
N_PAIRS = GROUP_W // LANES
ATT_SCALE = HEAD_DIM ** -0.5
LOG2E = math.log2(math.e)

COL_A, COL_B, COL_C = 0, 16, 32
COL_D = 48
COL_DG = 60
COL_SMALL = 64
N_PROJ = (COL_SMALL + 1) * LANES
SMALL_WLO, SMALL_ALO = 32, 64

ATT_TILE = 512
MAX_SHIFT_LOG2 = 56.0
STICK_BLOCK = 256
STICK_DEAD_LOG2 = -152.0
PREP_ROWS = 512
RWKV_CHUNK = 128
RWKV_PIECES = 1
RWKV_PIECES_G = 2
VMEM_LIMIT = 48 * 1024 * 1024


def _cparams(sem):
    return pltpu.CompilerParams(dimension_semantics=sem, vmem_limit_bytes=VMEM_LIMIT)


def _dot(a, b, **kw):
    return jnp.dot(a, b, preferred_element_type=F32, **kw)


def _dot_nt(a, b, **kw):
    return lax.dot_general(a, b, (((1,), (1,)), ((), ())), preferred_element_type=F32, **kw)


def _lane_lo(shape):
    return lax.broadcasted_iota(jnp.int32, shape, len(shape) - 1) < HEAD_DIM


def _seg_sum(x):
    lo = _lane_lo(x.shape)
    s_lo = jnp.sum(jnp.where(lo, x, 0.0), axis=-1, keepdims=True)
    s_hi = jnp.sum(jnp.where(lo, 0.0, x), axis=-1, keepdims=True)
    return jnp.where(lo, s_lo, s_hi)


def _seg_rms(x, gain):
    return x * lax.rsqrt(_seg_sum(x * x) * (1.0 / HEAD_DIM) + NORM_EPS) * gain


def _log_sigmoid(z):
    return jnp.minimum(z, 0.0) - jnp.log1p(jnp.exp(-jnp.abs(z)))


def _silu(g):
    return g / (1.0 + jnp.exp(-g))


def _proj_kernel(x_ref, g_ref, w_ref, o_ref, h_ref):
    @pl.when(pl.program_id(1) == 0)
    def _():
        x = x_ref[...]
        ms = jnp.mean(x * x, axis=-1, keepdims=True)
        h_ref[...] = (x * lax.rsqrt(ms + NORM_EPS) * g_ref[...]).astype(BF16)

    o_ref[...] = _dot(h_ref[...], w_ref[...])


def _proj(x2d, gain, w_packed):
    m, d = x2d.shape
    n = w_packed.shape[1]
    tm = min(1024, m)
    tn = n // 5
    return pl.pallas_call(
        _proj_kernel,
        grid=(m // tm, n // tn),
        in_specs=[pl.BlockSpec((tm, d), lambda i, j: (i, 0)),
                  pl.BlockSpec((1, d), lambda i, j: (0, 0)),
                  pl.BlockSpec((d, tn), lambda i, j: (0, j))],
        out_specs=pl.BlockSpec((tm, tn), lambda i, j: (i, j)),
        out_shape=jax.ShapeDtypeStruct((m, n), F32),
        scratch_shapes=[pltpu.VMEM((tm, d), BF16)],
        compiler_params=_cparams(("arbitrary", "arbitrary")),
    )(x2d, gain.reshape(1, d), w_packed)


def _for_row_blocks(seq, fn):
    rows_per = min(PREP_ROWS, seq)

    def body(c, carry):
        fn(pl.ds(pl.multiple_of(c * rows_per, rows_per), rows_per), rows_per)
        return carry

    lax.fori_loop(0, seq // rows_per, body, 0)


def _head_masked(q):
    lo = _lane_lo(q.shape)
    return jnp.where(lo, q, 0.0).astype(BF16), jnp.where(lo, 0.0, q).astype(BF16)


def _online_softmax(scores, carry, values):
    m_new = [jnp.maximum(m, jnp.max(s, axis=-1, keepdims=True)) for s, (m, _) in zip(scores, carry)]
    probs = [jnp.exp2(s - m).astype(BF16) for s, m in zip(scores, m_new)]
    return tuple((mn, jnp.exp2(m - mn) * acc + _dot(p, v))
                 for p, mn, (m, acc), v in zip(probs, m_new, carry, values))


def _softmax_init(t, n):
    return (jnp.full((t, 1), NEG_INF, F32), jnp.zeros((t, n), F32))


def _shifted_softmax(scores, carry, values):
    return tuple((m, acc + _dot(jnp.exp2(s).astype(BF16), v)) for s, (m, acc), v in zip(scores, carry, values))


def _attn_a_kernel(q_ref, k_ref, v_ref, g_ref, tz_ref, qg_ref, kg_ref, og_ref, dl_ref, o_ref, kbf, vbf,
                   *, lam_init, bounded):
    softmax = _shifted_softmax if bounded else _online_softmax
    qi = pl.program_id(2)
    t = q_ref.shape[1]

    @pl.when(qi == 0)
    def _():
        def stage(rows, n):
            kbf[rows, :] = _seg_rms(k_ref[0, rows, :], kg_ref[...]).astype(BF16)
            vbf[rows, :LANES] = v_ref[0, rows, :].astype(BF16)
            vbf[rows, LANES:] = jnp.ones((n, LANES), BF16)

        _for_row_blocks(k_ref.shape[1], stage)

    qs = _head_masked(_seg_rms(q_ref[0], qg_ref[...]) * (ATT_SCALE * LOG2E))
    row = lax.broadcasted_iota(jnp.int32, (t, t), 0)
    col = lax.broadcasted_iota(jnp.int32, (t, t), 1)

    def tile(j, carry, diag):
        ks = pl.ds(pl.multiple_of(j * t, t), t)
        kt, vt = kbf[ks, :], vbf[ks, :]
        bias = tz_ref[0, jnp.minimum(qi - j, 2)]
        scores = [_dot_nt(q, kt) + bias for q in qs]
        if diag:
            scores = [jnp.where(row >= col, s, NEG_INF) for s in scores]
        return softmax(scores, carry, (vt, vt))

    carry = lax.fori_loop(0, qi, lambda j, c: tile(j, c, False), (_softmax_init(t, 2 * LANES),) * 2)
    (_, acc0), (_, acc1) = tile(qi, carry, True)

    dl = dl_ref[...]
    lam = (jnp.exp(jnp.sum(dl[0:1] * dl[1:2], axis=-1, keepdims=True))
           - jnp.exp(jnp.sum(dl[2:3] * dl[3:4], axis=-1, keepdims=True)) + lam_init)
    o = acc0[:, :LANES] / acc0[:, LANES:] - lam * (acc1[:, :LANES] / acc1[:, LANES:])
    y = o * lax.rsqrt(jnp.mean(o * o, axis=-1, keepdims=True) + NORM_EPS) * og_ref[0] * (1.0 - lam_init)
    o_ref[0] = (y * _silu(g_ref[0])).astype(o_ref.dtype)


def _attn_c_kernel(q_ref, k_ref, v_ref, g_ref, f_ref, fq_ref, bound_ref, qg_ref, kg_ref, og_ref, o_ref,
                   kbf, vbf0, vbf1, *, bounded):
    softmax = _shifted_softmax if bounded else _online_softmax
    qi = pl.program_id(2)
    t = q_ref.shape[1]

    @pl.when(qi == 0)
    def _():
        def stage(rows, n):
            kbf[rows, :] = _seg_rms(k_ref[0, rows, :], kg_ref[...]).astype(BF16)
            v = v_ref[0, rows, :]
            lo = _lane_lo(v.shape)
            vbf0[rows, :] = jnp.where(lo, v, 1.0).astype(BF16)
            vbf1[rows, :] = jnp.where(lo, 1.0, v).astype(BF16)

        _for_row_blocks(k_ref.shape[1], stage)

    qs = _head_masked(_seg_rms(q_ref[0], qg_ref[...]) * (ATT_SCALE * LOG2E))
    if bounded:
        fqs = [fq_ref[0, 0, :, h * HEAD_DIM:h * HEAD_DIM + 1] * LOG2E - bound_ref[:, :1] for h in range(2)]
    else:
        fqs = [0.0, 0.0]
    row = lax.broadcasted_iota(jnp.int32, (t, t), 0)
    col = lax.broadcasted_iota(jnp.int32, (t, t), 1)

    def tile(j, carry, diag):
        ks = pl.ds(pl.multiple_of(j * t, t), t)
        kt = kbf[ks, :]
        scores = [_dot_nt(q, kt) + (fq - f_ref[0, 0, h:h + 1, ks] * LOG2E) for h, (q, fq) in enumerate(zip(qs, fqs))]
        if diag:
            scores = [jnp.where(row >= col, s, NEG_INF) for s in scores]
        return softmax(scores, carry, (vbf0[ks, :], vbf1[ks, :]))

    carry = lax.fori_loop(0, qi, lambda j, c: tile(j, c, False), (_softmax_init(t, LANES),) * 2)
    (_, acc0), (_, acc1) = tile(qi, carry, True)
    lo = _lane_lo(acc0.shape)
    denom = jnp.where(lo, pltpu.roll(acc0, HEAD_DIM, 1), pltpu.roll(acc1, HEAD_DIM, 1))
    o = jnp.where(lo, acc0, acc1) / denom
    o_ref[0] = (_seg_rms(o, og_ref[0]) * _silu(g_ref[0])).astype(o_ref.dtype)


def _attn_b_kernel(q_ref, k_ref, v_ref, g_ref, u_ref, og_ref, o_ref, kbf, vbf):
    qi = pl.program_id(2)
    t = q_ref.shape[1]
    blk = u_ref.shape[0]

    @pl.when(qi == 0)
    def _():
        def stage(rows, n):
            kbf[rows, :] = k_ref[0, rows, :].astype(BF16)
            vbf[rows, :] = v_ref[0, rows, :].astype(BF16)

        _for_row_blocks(k_ref.shape[1], stage)

    qs = _head_masked(q_ref[0] * (ATT_SCALE * LOG2E))
    row = lax.broadcasted_iota(jnp.int32, (t, t), 0)
    col = lax.broadcasted_iota(jnp.int32, (t, t), 1)
    umat = u_ref[...]

    def tile(j, carry, diag):
        ks = pl.ds(pl.multiple_of(j * t, t), t)
        kt, vt = kbf[ks, :], vbf[ks, :]
        zs = [_dot_nt(q, kt) for q in qs]
        l1s = [-(jnp.maximum(z, 0.0) + jnp.log2(1.0 + jnp.exp2(-jnp.abs(z)))) for z in zs]
        if diag:
            l1s = [jnp.where(row > col, l1, 0.0) for l1 in l1s]
        tails = [c for c, _ in carry]
        afters = [[] for _ in qs]
        for b in reversed(range(t // blk)):
            for i, l1 in enumerate(l1s):
                x = l1[:, b * blk:(b + 1) * blk]
                hi = x.astype(BF16)
                lo = (x - hi.astype(F32)).astype(BF16)
                afters[i].insert(0, _dot(hi, umat) + _dot(lo, umat) + tails[i])
                tails[i] = tails[i] + jnp.sum(x, axis=-1, keepdims=True)
        out = []
        for z, l1, after, c_new, (_, acc) in zip(zs, l1s, afters, tails, carry):
            a = jnp.exp2(l1 + z + jnp.concatenate(after, axis=1))
            if diag:
                a = jnp.where(row > col, a, 0.0)
            out.append((c_new, acc + _dot(a.astype(BF16), vt)))
        return tuple(out)

    def alive(state):
        i, carry = state
        live = jnp.maximum(jnp.max(carry[0][0]), jnp.max(carry[1][0])) > STICK_DEAD_LOG2
        return jnp.logical_and(i < qi, live)

    init = ((jnp.zeros((t, 1), F32), jnp.zeros((t, LANES), F32)),) * 2
    _, ((_, a0), (_, a1)) = lax.while_loop(alive, lambda st: (st[0] + 1, tile(qi - 1 - st[0], st[1], False)),
                                           (jnp.int32(0), tile(qi, init, True)))
    o = jnp.where(_lane_lo(a0.shape), a0, a1)
    o_ref[0] = (_seg_rms(o, og_ref[0]) * _silu(g_ref[0])).astype(o_ref.dtype)


def _attn_specs(seq, t, col):
    q_spec = pl.BlockSpec((1, t, LANES), lambda b, p, i: (b, i, col + p))
    k_spec = pl.BlockSpec((1, seq, LANES), lambda b, p, i: (b, 0, col + N_PAIRS + p))
    v_spec = pl.BlockSpec((1, seq, LANES), lambda b, p, i: (b, 0, col + 2 * N_PAIRS + p))
    g_spec = pl.BlockSpec((1, t, LANES), lambda b, p, i: (b, i, col + 3 * N_PAIRS + p))
    return [q_spec, k_spec, v_spec, g_spec]


def _pair_spec():
    return pl.BlockSpec((1, 1, LANES), lambda b, p, i: (p, 0, 0))


def _const_spec(shape):
    return pl.BlockSpec(shape, lambda b, p, i: (0,) * len(shape))


def _attn_call(body, proj, col, extra_specs, extra_args, v_lanes=(LANES,)):
    bsz, seq, _ = proj.shape
    t = min(ATT_TILE, seq)
    return pl.pallas_call(
        body,
        grid=(bsz, N_PAIRS, seq // t),
        in_specs=_attn_specs(seq, t, col) + extra_specs,
        out_specs=pl.BlockSpec((1, t, LANES), lambda b, p, i: (b, i, p)),
        out_shape=jax.ShapeDtypeStruct((bsz, seq, GROUP_W), BF16),
        scratch_shapes=[pltpu.VMEM((seq, LANES), BF16)] + [pltpu.VMEM((seq, n), BF16) for n in v_lanes],
        compiler_params=_cparams(("arbitrary", "arbitrary", "arbitrary")),
    )(proj, proj, proj, proj, *extra_args)


def _pair_gain(g64):
    return jnp.tile(g64.astype(F32), 2).reshape(1, LANES)


def _score_bound_log2(q_gain, k_gain):
    return (HEAD_DIM * ATT_SCALE * LOG2E * 1.02) * jnp.max(jnp.abs(q_gain)) * jnp.max(jnp.abs(k_gain))


def _attn_a(proj, tz, q_gain, k_gain, o_gain, dlam, lam_init, bounded):
    t = tz.shape[-1]
    specs = [pl.BlockSpec((1, 3, t, t), lambda b, p, i: (p, 0, 0, 0)),
             _const_spec((1, LANES)), _const_spec((1, LANES)), _pair_spec(), _const_spec((4, HEAD_DIM))]
    args = (tz, _pair_gain(q_gain), _pair_gain(k_gain), o_gain.reshape(N_PAIRS, 1, LANES), dlam)
    return _attn_call(functools.partial(_attn_a_kernel, lam_init=lam_init, bounded=bounded), proj, COL_A, specs,
                      args, v_lanes=(2 * LANES,))


def _attn_b(proj, o_gain):
    t = min(STICK_BLOCK, proj.shape[1])
    umat = (lax.broadcasted_iota(jnp.int32, (t, t), 0) > lax.broadcasted_iota(jnp.int32, (t, t), 1)).astype(BF16)
    specs = [_const_spec((t, t)), _pair_spec()]
    return _attn_call(_attn_b_kernel, proj, COL_B, specs, (umat, o_gain.reshape(N_PAIRS, 1, LANES)))


def _attn_c(proj, cum_rows, cum_cols, bound, q_gain, k_gain, o_gain, bounded):
    seq = proj.shape[1]
    t = min(ATT_TILE, seq)
    specs = [pl.BlockSpec((1, 1, 2, seq), lambda b, p, i: (b, p, 0, 0)),
             pl.BlockSpec((1, 1, t, LANES), lambda b, p, i: (b, p, i, 0)),
             _const_spec((1, LANES)), _const_spec((1, LANES)), _const_spec((1, LANES)), _pair_spec()]
    args = (cum_rows, cum_cols, jnp.full((1, LANES), bound, F32), _pair_gain(q_gain), _pair_gain(k_gain),
            o_gain.reshape(N_PAIRS, 1, LANES))
    return _attn_call(functools.partial(_attn_c_kernel, bounded=bounded), proj, COL_C, specs, args,
                      v_lanes=(LANES, LANES))


def _fox_cum_kernel(s_ref, fb_ref, tri_ref, rows_ref, cols_ref):
    seq = s_ref.shape[1]
    rows_per = tri_ref.shape[0]
    carry = jnp.zeros((1, LANES), F32)
    for c in range(seq // rows_per):
        rows = slice(c * rows_per, (c + 1) * rows_per)
        log_f = _log_sigmoid(s_ref[0, rows, :] + fb_ref[...])
        cum = _dot(tri_ref[...], log_f, precision=HIGHEST) + carry
        carry = cum[rows_per - 1:, :]
        rows_ref[0, :, rows] = cum.T[:8, :]
        lo = _lane_lo(cum.shape)
        for p in range(N_PAIRS):
            cols_ref[0, p, rows, :] = jnp.where(lo, cum[:, 2 * p:2 * p + 1], cum[:, 2 * p + 1:2 * p + 2])


def _fox_cum(proj, forget_bias):
    bsz, seq, _ = proj.shape
    rows_per = min(PREP_ROWS, seq)
    tri = (lax.broadcasted_iota(jnp.int32, (rows_per, rows_per), 0)
           >= lax.broadcasted_iota(jnp.int32, (rows_per, rows_per), 1)).astype(F32)
    fb = jnp.zeros((1, LANES), F32).at[0, :forget_bias.shape[0]].set(forget_bias.astype(F32))
    cum_rows, cum_cols = pl.pallas_call(
        _fox_cum_kernel,
        grid=(bsz,),
        in_specs=[pl.BlockSpec((1, seq, LANES), lambda b: (b, 0, COL_SMALL)),
                  pl.BlockSpec((1, LANES), lambda b: (0, 0)),
                  pl.BlockSpec((rows_per, rows_per), lambda b: (0, 0))],
        out_specs=[pl.BlockSpec((1, 8, seq), lambda b: (b, 0, 0)),
                   pl.BlockSpec((1, N_PAIRS, seq, LANES), lambda b: (b, 0, 0, 0))],
        out_shape=[jax.ShapeDtypeStruct((bsz, 8, seq), F32),
                   jax.ShapeDtypeStruct((bsz, N_PAIRS, seq, LANES), F32)],
        compiler_params=_cparams(("arbitrary",)),
    )(proj, fb, tri)
    return cum_rows.reshape(bsz, N_PAIRS, 2, seq), cum_cols


def _rwkv_prep_kernel(u_ref, s_ref, mu_ref, mus_ref, wup_ref, w0_ref, aup_ref, a0_ref, kkr_ref,
                      r_o, lw_o, k_o, v_o, aa_o, b_o, prev_u, prev_s):
    @pl.when(pl.program_id(1) == 0)
    def _():
        prev_u[...] = jnp.zeros_like(prev_u)
        prev_s[...] = jnp.zeros_like(prev_s)

    def shift(x, prev, mu):
        first = lax.broadcasted_iota(jnp.int32, x.shape, 0) == 0
        x_prev = jnp.where(first, prev[...], pltpu.roll(x, 1, 0))
        prev[...] = x[x.shape[0] - 1:, :]
        return x + (x_prev - x) * mu

    u = shift(u_ref[0], prev_u, mu_ref[...])
    sm = shift(s_ref[0], prev_s, mus_ref[...])
    r, k, v = u[:, :GROUP_W], u[:, GROUP_W:2 * GROUP_W], u[:, 2 * GROUP_W:]
    w_pre = w0_ref[...] + _dot(jnp.tanh(sm), wup_ref[...], precision=HIGHEST)
    w_log = _log_sigmoid(w_pre) - 0.5
    a = 1.0 / (1.0 + jnp.exp(-(a0_ref[...] + _dot(sm, aup_ref[...], precision=HIGHEST))))
    kk = k * kkr_ref[0:1, :]
    kk_n = jnp.concatenate(
        [kk[:, i * LANES:(i + 1) * LANES]
         / jnp.maximum(jnp.sqrt(_seg_sum(kk[:, i * LANES:(i + 1) * LANES] ** 2)), 1e-12)
         for i in range(N_PAIRS)], axis=-1)
    r_o[0] = r
    lw_o[0] = -jnp.exp(w_log)
    k_o[0] = k * (1.0 + (a - 1.0) * kkr_ref[1:2, :])
    v_o[0] = v
    aa_o[0] = -kk_n
    b_o[0] = kk_n * a


def _rwkv_prep(proj, mu, w_up, w0, a_up, a0, kkr):
    bsz, seq, _ = proj.shape
    ts = min(PREP_ROWS, seq)
    mu = mu.astype(F32)
    mu_u = mu[:3 * GROUP_W].reshape(1, 3 * GROUP_W)
    mu_s = (jnp.zeros((1, LANES), F32)
            .at[0, SMALL_WLO:SMALL_WLO + RWKV_RANK].set(mu[3 * GROUP_W:3 * GROUP_W + RWKV_RANK])
            .at[0, SMALL_ALO:SMALL_ALO + RWKV_RANK].set(mu[3 * GROUP_W + RWKV_RANK:]))
    wup = jnp.zeros((LANES, GROUP_W), F32).at[SMALL_WLO:SMALL_WLO + RWKV_RANK].set(w_up.astype(F32))
    aup = jnp.zeros((LANES, GROUP_W), F32).at[SMALL_ALO:SMALL_ALO + RWKV_RANK].set(a_up.astype(F32))
    vec = lambda a: a.astype(F32).reshape(1, GROUP_W)
    full = lambda shape: pl.BlockSpec(shape, lambda b, i: (0,) * len(shape))
    out_spec = pl.BlockSpec((1, ts, GROUP_W), lambda b, i: (b, i, 0))
    out_sds = jax.ShapeDtypeStruct((bsz, seq, GROUP_W), F32)
    return pl.pallas_call(
        _rwkv_prep_kernel,
        grid=(bsz, seq // ts),
        in_specs=[pl.BlockSpec((1, ts, 3 * GROUP_W), lambda b, i: (b, i, COL_D * LANES // (3 * GROUP_W))),
                  pl.BlockSpec((1, ts, LANES), lambda b, i: (b, i, COL_SMALL)),
                  full((1, 3 * GROUP_W)), full((1, LANES)), full((LANES, GROUP_W)), full((1, GROUP_W)),
                  full((LANES, GROUP_W)), full((1, GROUP_W)), full((3, GROUP_W))],
        out_specs=[out_spec] * 6,
        out_shape=[out_sds] * 6,
        scratch_shapes=[pltpu.VMEM((1, 3 * GROUP_W), F32), pltpu.VMEM((1, LANES), F32)],
        compiler_params=_cparams(("arbitrary", "arbitrary")),
    )(proj, proj, mu_u, mu_s, wup, vec(w0), aup, vec(a0), kkr.astype(F32))


def _pieces(x, n):
    out = []
    for i in range(n):
        p = x.astype(BF16)
        out.append(p)
        if i + 1 < n:
            x = x - p.astype(F32)
    return tuple(out)


def _mm(a, b, nt=False):
    n = max(len(a), len(b))
    dot = _dot_nt if nt else _dot
    acc = None
    for i, ai in enumerate(a):
        for j, bj in enumerate(b):
            if i + j < n:
                term = dot(ai, bj)
                acc = term if acc is None else acc + term
    return acc


def _neumann_inverses(mats):
    n = mats[0].shape[0]
    eye = (lax.broadcasted_iota(jnp.int32, (n, n), 0) == lax.broadcasted_iota(jnp.int32, (n, n), 1)).astype(F32)
    ps = [eye + a for a in mats]
    xs = [_pieces(a, RWKV_PIECES) for a in mats]
    for _ in range(int(math.log2(n)) - 1):
        xs = [_pieces(_mm(x, x), RWKV_PIECES) for x in xs]
        ps = [p + _mm(_pieces(p, RWKV_PIECES), x) for p, x in zip(ps, xs)]
    return ps


def _rwkv_scan_kernel(r_ref, lw_ref, k_ref, v_ref, aa_ref, b_ref, g_ref, lng_ref, lnb_ref, rk_ref, tri_ref,
                      o_ref, state):
    c = r_ref.shape[1]
    n1, ng = RWKV_PIECES, RWKV_PIECES_G

    @pl.when(pl.program_id(1) == 0)
    def _():
        state[...] = jnp.zeros_like(state)

    row = lax.broadcasted_iota(jnp.int32, (c, c), 0)
    col = lax.broadcasted_iota(jnp.int32, (c, c), 1)
    strict, incl = row > col, row >= col
    lo = _lane_lo((c, LANES))
    lo2 = _lane_lo((2 * c, LANES))
    diag_blocks = ((lax.broadcasted_iota(jnp.int32, (LANES, LANES), 0) < HEAD_DIM)
                   == (lax.broadcasted_iota(jnp.int32, (LANES, LANES), 1) < HEAD_DIM))
    tri = (tri_ref[...],)
    pairs = range(N_PAIRS)
    lanes = [slice(p * LANES, (p + 1) * LANES) for p in pairs]

    st = []
    for p in pairs:
        r, lw, k, v = (ref[0, :, lanes[p]] for ref in (r_ref, lw_ref, k_ref, v_ref))
        aa, b = aa_ref[0, :, lanes[p]], b_ref[0, :, lanes[p]]
        cl = _mm(tri, _pieces(lw, 3))
        cl_end = cl[c - 1:c, :]
        cl_mid = cl[c // 2 - 1:c // 2, :]
        w_inv = jnp.exp(cl_mid - cl)
        w_rest = jnp.exp(cl_end - cl)
        qa = jnp.concatenate([aa * jnp.exp(cl - lw - cl_mid), r * jnp.exp(cl - cl_mid)], axis=0)
        bk = _pieces(jnp.concatenate([b * w_inv, k * w_inv], axis=0), ng)
        qa_s = jnp.concatenate([aa * jnp.exp(cl - lw), r * jnp.exp(cl)], axis=0)
        bk_end = _pieces(jnp.concatenate([b * w_rest, k * w_rest], axis=0), n1)
        st.append(dict(r=r, k=k, v=v, qa=qa, bk=bk, qa_s=qa_s, bk_end=bk_end, w_end=jnp.exp(cl_end),
                       v_p=_pieces(v, n1)))

    bases = [_mm(_pieces(st[p]["qa_s"], n1), _pieces(state[p], n1), nt=True) for p in pairs]
    a_ab, a_ak, a_r = [], [], []
    for p in pairs:
        for h in range(2):
            g = _mm(_pieces(jnp.where(lo2 == (h == 0), st[p]["qa"], 0.0), ng), st[p]["bk"], nt=True)
            a_ab.append(jnp.where(strict, g[:c, :c], 0.0))
            a_ak.append(jnp.where(strict, g[:c, c:], 0.0))
            a_r.append(jnp.concatenate([jnp.where(incl, g[c:, :c], 0.0), jnp.where(incl, g[c:, c:], 0.0)], axis=1))
    rhs = [bases[i // 2][:c] + _mm(_pieces(a_ak[i], n1), st[i // 2]["v_p"]) for i in range(2 * N_PAIRS)]

    t_inv = _neumann_inverses(a_ab)
    u_heads = [_mm(_pieces(t_inv[i], n1), _pieces(rhs[i], n1)) for i in range(2 * N_PAIRS)]
    for p in pairs:
        r, k, v = st[p]["r"], st[p]["k"], st[p]["v"]
        u = jnp.where(lo, u_heads[2 * p], u_heads[2 * p + 1])
        uv = _pieces(jnp.concatenate([u, v], axis=0), n1)
        y = bases[p][c:] + jnp.where(lo, _mm(_pieces(a_r[2 * p], n1), uv), _mm(_pieces(a_r[2 * p + 1], n1), uv))
        upd = _mm(_pieces(jnp.concatenate([u.T, v.T], axis=1), n1), st[p]["bk_end"])
        state[p] = state[p] * st[p]["w_end"] + jnp.where(diag_blocks, upd, 0.0)

        mean = _seg_sum(y) * (1.0 / HEAD_DIM)
        yc = y - mean
        var = _seg_sum(yc * yc) * (1.0 / HEAD_DIM)
        out = yc * lax.rsqrt(var + RWKV_LN_EPS) * lng_ref[:, lanes[p]] + lnb_ref[:, lanes[p]]
        out = out + _seg_sum(r * k * rk_ref[:, lanes[p]]) * v
        o_ref[0, :, lanes[p]] = (out * _silu(g_ref[0, :, lanes[p]])).astype(o_ref.dtype)


def _rwkv_scan(proj, prepped, ln_gain, ln_bias, r_k):
    bsz, seq, _ = proj.shape
    c = min(RWKV_CHUNK, seq)
    tri = (lax.broadcasted_iota(jnp.int32, (c, c), 0) >= lax.broadcasted_iota(jnp.int32, (c, c), 1)).astype(BF16)
    tok_spec = pl.BlockSpec((1, c, GROUP_W), lambda b, i: (b, i, 0))
    vec_spec = pl.BlockSpec((1, GROUP_W), lambda b, i: (0, 0))
    vec = lambda a: a.astype(F32).reshape(1, GROUP_W)
    return pl.pallas_call(
        _rwkv_scan_kernel,
        grid=(bsz, seq // c),
        in_specs=[tok_spec] * 6
        + [pl.BlockSpec((1, c, GROUP_W), lambda b, i: (b, i, COL_DG * LANES // GROUP_W)),
           vec_spec, vec_spec, vec_spec, pl.BlockSpec((c, c), lambda b, i: (0, 0))],
        out_specs=tok_spec,
        out_shape=jax.ShapeDtypeStruct((bsz, seq, GROUP_W), BF16),
        scratch_shapes=[pltpu.VMEM((N_PAIRS, LANES, LANES), F32)],
        compiler_params=_cparams(("arbitrary", "arbitrary")),
    )(*prepped, proj, vec(ln_gain), vec(ln_bias), vec(r_k), tri)


def _out_kernel(a_ref, b_ref, c_ref, d_ref, w_ref, x_ref, o_ref):
    acc = x_ref[...]
    for i, m_ref in enumerate((a_ref, b_ref, c_ref, d_ref)):
        acc = acc + _dot(m_ref[...], w_ref[i * GROUP_W:(i + 1) * GROUP_W, :])
    o_ref[...] = acc


def _out_proj(groups, w_out, x2d):
    m, d = x2d.shape
    tm = min(512, m)
    g_spec = pl.BlockSpec((tm, GROUP_W), lambda i: (i, 0))
    return pl.pallas_call(
        _out_kernel,
        grid=(m // tm,),
        in_specs=[g_spec] * 4 + [pl.BlockSpec(w_out.shape, lambda i: (0, 0)),
                                 pl.BlockSpec((tm, d), lambda i: (i, 0))],
        out_specs=pl.BlockSpec((tm, d), lambda i: (i, 0)),
        out_shape=jax.ShapeDtypeStruct((m, d), F32),
        compiler_params=_cparams(("arbitrary",)),
    )(*[g.reshape(m, GROUP_W) for g in groups], w_out, x2d)


def _t5_causal_bucket(dist):
    max_exact = NUM_BUCKETS // 2
    d = jnp.maximum(dist, 1).astype(F32)
    large = max_exact + (jnp.log(d / max_exact) / math.log(MAX_DISTANCE / max_exact)
                         * (NUM_BUCKETS - max_exact)).astype(jnp.int32)
    large = jnp.minimum(large, NUM_BUCKETS - 1)
    return jnp.where(dist < max_exact, dist, large)


def _bias_tiles(rel_bias, seq, t):
    assert t >= MAX_DISTANCE
    bias_by_dist = rel_bias.astype(F32)[_t5_causal_bucket(jnp.arange(seq))] * LOG2E
    m = np.arange(2 * t)
    tiles = []
    for delta in range(3):
        dist = np.where(m < t, delta * t - m, delta * t + 2 * t - m)
        v = bias_by_dist[np.clip(dist, 0, seq - 1)].T
        skew = jnp.tile(v, (1, t))[:, :t * (2 * t - 1)].reshape(-1, t, 2 * t - 1)
        tiles.append(skew[:, :, :t])
    return jnp.stack(tiles, axis=1)


def _pack_w_in(w):
    d = w.shape[0]
    att = w[:, :12 * GROUP_W]
    cf = w[:, 12 * GROUP_W:12 * GROUP_W + 8]
    off = 12 * GROUP_W + 8
    rkv = w[:, off:off + 3 * GROUP_W]
    w_lo = w[:, off + 3 * GROUP_W:off + 3 * GROUP_W + RWKV_RANK]
    a_lo = w[:, off + 3 * GROUP_W + RWKV_RANK:off + 3 * GROUP_W + 2 * RWKV_RANK]
    dg = w[:, off + 3 * GROUP_W + 2 * RWKV_RANK:]
    small = (jnp.zeros((d, LANES), w.dtype).at[:, :8].set(cf)
             .at[:, SMALL_WLO:SMALL_WLO + RWKV_RANK].set(w_lo)
             .at[:, SMALL_ALO:SMALL_ALO + RWKV_RANK].set(a_lo))
    return jnp.concatenate([att, rkv, dg, small], axis=1).astype(BF16)


def kernel(x, norm_gain, w_in, w_out, rel_bias, qk_gain, diff_lambda, forget_bias, out_gain, rwkv_mu, rwkv_w_up,
           rwkv_w0, rwkv_a_up, rwkv_a0, rwkv_kkr, rwkv_ln_gain, rwkv_ln_bias):
    bsz, seq, d = x.shape
    tz = _bias_tiles(rel_bias, seq, min(ATT_TILE, seq))
    bias_max = jnp.max(jnp.abs(rel_bias.astype(F32))) * LOG2E
    h = x.reshape(bsz * seq, d)
    for l in range(DEPTH):
        og = out_gain[l].astype(F32)
        proj = _proj(h, norm_gain[l].astype(F32), _pack_w_in(w_in[l])).reshape(bsz, seq, N_PROJ)
        lam_init = 0.8 - 0.6 * math.exp(-0.3 * l)
        qg = qk_gain[l].astype(F32)
        bound_a = _score_bound_log2(qg[0], qg[1]) + bias_max
        bound_c = _score_bound_log2(qg[2], qg[3])
        att_a = functools.partial(_attn_a, proj, q_gain=qg[0], k_gain=qg[1], o_gain=og[:GROUP_W],
                                  dlam=diff_lambda[l].astype(F32), lam_init=lam_init)
        oa = lax.cond(bound_a <= MAX_SHIFT_LOG2, lambda: att_a(tz=tz - bound_a, bounded=True),
                      lambda: att_a(tz=tz, bounded=False))
        ob = _attn_b(proj, og[GROUP_W:2 * GROUP_W])
        att_c = functools.partial(_attn_c, proj, *_fox_cum(proj, forget_bias[l]), bound_c, qg[2], qg[3],
                                  og[2 * GROUP_W:3 * GROUP_W])
        oc = lax.cond(bound_c <= MAX_SHIFT_LOG2, lambda: att_c(bounded=True), lambda: att_c(bounded=False))
        prepped = _rwkv_prep(proj, rwkv_mu[l], rwkv_w_up[l], rwkv_w0[l], rwkv_a_up[l], rwkv_a0[l], rwkv_kkr[l])
        od = _rwkv_scan(proj, prepped, rwkv_ln_gain[l].astype(F32) * og[3 * GROUP_W:], rwkv_ln_bias[l],
                        rwkv_kkr[l, 2])
        h = _out_proj((oa, ob, oc, od), w_out[l].astype(BF16), h)
    return h.reshape(bsz, seq, d)
```

```python
import functools
import math

import numpy as np
import jax
import jax.numpy as jnp
from jax import lax
from jax.experimental import pallas as pl
from jax.experimental.pallas import tpu as pltpu

F32 = jnp.float32
BF16 = jnp.bfloat16
HIGHEST = lax.Precision.HIGHEST

D_MODEL = 1024
DEPTH = 2
GROUP_W = 512
HEAD_DIM = 64
H_A = 4
NUM_BUCKETS = 32
MAX_DISTANCE = 128
RWKV_RANK = 32
NORM_EPS = 1e-6
RWKV_LN_EPS = 64e-5
NEG_INF = -1e30

LANES = 128
N_PAIRS = GROUP_W // LANES
ATT_SCALE = HEAD_DIM ** -0.5
LOG2E = math.log2(math.e)

COL_A, COL_B, COL_C = 0, 16, 32
COL_D = 48
COL_DG = 60
COL_SMALL = 64
N_PROJ = (COL_SMALL + 1) * LANES
SMALL_WLO, SMALL_ALO = 32, 64

ATT_TILE = 512
ATT_PAIRS = 2
MAX_SHIFT_LOG2 = 56.0
STICK_BLOCK = 256
STICK_DEAD_LOG2 = -152.0
PREP_ROWS = 512
RWKV_CHUNK = 128
RWKV_ROWS = 2
RWKV_PIECES = 1
RWKV_PIECES_G = 2
VMEM_LIMIT = 56 * 1024 * 1024


def _cparams(sem):
    return pltpu.CompilerParams(dimension_semantics=sem, vmem_limit_bytes=VMEM_LIMIT)


def _dot(a, b, **kw):
    return jnp.dot(a, b, preferred_element_type=F32, **kw)


def _dot_nt(a, b, **kw):
    return lax.dot_general(a, b, (((1,), (1,)), ((), ())), preferred_element_type=F32, **kw)


def _lane_lo(shape):
    return lax.broadcasted_iota(jnp.int32, shape, len(shape) - 1) < HEAD_DIM


def _seg_sum(x):
    lo = _lane_lo(x.shape)
    s_lo = jnp.sum(jnp.where(lo, x, 0.0), axis=-1, keepdims=True)
    s_hi = jnp.sum(jnp.where(lo, 0.0, x), axis=-1, keepdims=True)
    return jnp.where(lo, s_lo, s_hi)


def _seg_rms(x, gain):
    return x * lax.rsqrt(_seg_sum(x * x) * (1.0 / HEAD_DIM) + NORM_EPS) * gain


def _log_sigmoid(z):
    return jnp.minimum(z, 0.0) - jnp.log1p(jnp.exp(-jnp.abs(z)))


def _silu(g):
    return g / (1.0 + jnp.exp(-g))


def _proj_kernel(x_ref, g_ref, w_ref, o_ref, h_ref):
    @pl.when(pl.program_id(1) == 0)
    def _():
        x = x_ref[...]
        ms = jnp.mean(x * x, axis=-1, keepdims=True)
        h_ref[...] = (x * lax.rsqrt(ms + NORM_EPS) * g_ref[...]).astype(BF16)

    o_ref[...] = _dot(h_ref[...], w_ref[...])


def _proj(x2d, gain, w_packed):
    m, d = x2d.shape
    n = w_packed.shape[1]
    tm = min(1024, m)
    tn = n // 5
    return pl.pallas_call(
        _proj_kernel,
        grid=(m // tm, n // tn),
        in_specs=[pl.BlockSpec((tm, d), lambda i, j: (i, 0)),
                  pl.BlockSpec((1, d), lambda i, j: (0, 0)),
                  pl.BlockSpec((d, tn), lambda i, j: (0, j))],
        out_specs=pl.BlockSpec((tm, tn), lambda i, j: (i, j)),
        out_shape=jax.ShapeDtypeStruct((m, n), F32),
        scratch_shapes=[pltpu.VMEM((tm, d), BF16)],
        compiler_params=_cparams(("arbitrary", "arbitrary")),
    )(x2d, gain.reshape(1, d), w_packed)


def _for_row_blocks(seq, fn):
    rows_per = min(PREP_ROWS, seq)

    def body(c, carry):
        fn(pl.ds(pl.multiple_of(c * rows_per, rows_per), rows_per), rows_per)
        return carry

    lax.fori_loop(0, seq // rows_per, body, 0)


def _head_masked(q):
    lo = _lane_lo(q.shape)
    return jnp.where(lo, q, 0.0).astype(BF16), jnp.where(lo, 0.0, q).astype(BF16)


def _online_softmax(scores, carry, values):
    m_new = [jnp.maximum(m, jnp.max(s, axis=-1, keepdims=True)) for s, (m, _) in zip(scores, carry)]
    probs = [jnp.exp2(s - m).astype(BF16) for s, m in zip(scores, m_new)]
    return tuple((mn, jnp.exp2(m - mn) * acc + _dot(p, v))
                 for p, mn, (m, acc), v in zip(probs, m_new, carry, values))


def _softmax_init(t, n):
    return (jnp.full((t, 1), NEG_INF, F32), jnp.zeros((t, n), F32))


def _shifted_softmax(scores, carry, values):
    return tuple((m, acc + _dot(jnp.exp2(s).astype(BF16), v)) for s, (m, acc), v in zip(scores, carry, values))


def _attn_a_kernel(q_ref, k_ref, v_ref, g_ref, tz_ref, qg_ref, kg_ref, og_ref, dl_ref, o_ref, kbf, vbf,
                   *, lam_init, bounded):
    softmax = _shifted_softmax if bounded else _online_softmax
    qi = pl.program_id(2)
    t = q_ref.shape[1]
    heads = range(q_ref.shape[2] // LANES)
    sl = [slice(h * LANES, (h + 1) * LANES) for h in heads]
    vsl = [slice(2 * h * LANES, 2 * (h + 1) * LANES) for h in heads]

    @pl.when(qi == 0)
    def _():
        def stage(rows, n):
            for h in heads:
                kbf[rows, sl[h]] = _seg_rms(k_ref[0, rows, sl[h]], kg_ref[...]).astype(BF16)
                vbf[rows, vsl[h]] = jnp.concatenate([v_ref[0, rows, sl[h]].astype(BF16),
                                                     jnp.ones((n, LANES), BF16)], axis=1)

        _for_row_blocks(k_ref.shape[1], stage)

    qs = [_head_masked(_seg_rms(q_ref[0, :, sl[h]], qg_ref[...]) * (ATT_SCALE * LOG2E)) for h in heads]
    row = lax.broadcasted_iota(jnp.int32, (t, t), 0)
    col = lax.broadcasted_iota(jnp.int32, (t, t), 1)

    def tile(j, carry, diag):
        ks = pl.ds(pl.multiple_of(j * t, t), t)
        delta = jnp.minimum(qi - j, 2)
        scores = [_dot_nt(q, kbf[ks, sl[h]]) + tz_ref[h, delta] for h in heads for q in qs[h]]
        if diag:
            scores = [jnp.where(row >= col, s, NEG_INF) for s in scores]
        return softmax(scores, carry, [vbf[ks, vsl[h]] for h in heads for _ in range(2)])

    carry = lax.fori_loop(0, qi, lambda j, c: tile(j, c, False), (_softmax_init(t, 2 * LANES),) * (2 * len(heads)))
    carry = tile(qi, carry, True)

    dl = dl_ref[...]
    lam = (jnp.exp(jnp.sum(dl[0:1] * dl[1:2], axis=-1, keepdims=True))
           - jnp.exp(jnp.sum(dl[2:3] * dl[3:4], axis=-1, keepdims=True)) + lam_init)
    for h in heads:
        (_, acc0), (_, acc1) = carry[2 * h], carry[2 * h + 1]
        o = acc0[:, :LANES] / acc0[:, LANES:] - lam * (acc1[:, :LANES] / acc1[:, LANES:])
        y = o * lax.rsqrt(jnp.mean(o * o, axis=-1, keepdims=True) + NORM_EPS) * og_ref[0, :, sl[h]] * (1.0 - lam_init)
        o_ref[0, :, sl[h]] = (y * _silu(g_ref[0, :, sl[h]])).astype(o_ref.dtype)


def _attn_c_kernel(q_ref, k_ref, v_ref, g_ref, f_ref, fq_ref, bound_ref, qg_ref, kg_ref, og_ref, o_ref,
                   kbf, vbf0, vbf1, *, bounded):
    softmax = _shifted_softmax if bounded else _online_softmax
    qi = pl.program_id(2)
    t = q_ref.shape[1]
    pairs = range(q_ref.shape[2] // LANES)
    sl = [slice(p * LANES, (p + 1) * LANES) for p in pairs]

    @pl.when(qi == 0)
    def _():
        def stage(rows, n):
            for p in pairs:
                kbf[rows, sl[p]] = _seg_rms(k_ref[0, rows, sl[p]], kg_ref[...]).astype(BF16)
                v = v_ref[0, rows, sl[p]]
                lo = _lane_lo(v.shape)
                vbf0[rows, sl[p]] = jnp.where(lo, v, 1.0).astype(BF16)
                vbf1[rows, sl[p]] = jnp.where(lo, 1.0, v).astype(BF16)

        _for_row_blocks(k_ref.shape[1], stage)

    qs = [_head_masked(_seg_rms(q_ref[0, :, sl[p]], qg_ref[...]) * (ATT_SCALE * LOG2E)) for p in pairs]
    if bounded:
        fqs = [[fq_ref[0, p, :, h * HEAD_DIM:h * HEAD_DIM + 1] * LOG2E - bound_ref[:, :1] for h in range(2)]
               for p in pairs]
    else:
        fqs = [[0.0, 0.0] for _ in pairs]
    row = lax.broadcasted_iota(jnp.int32, (t, t), 0)
    col = lax.broadcasted_iota(jnp.int32, (t, t), 1)

    def tile(j, carry, diag):
        ks = pl.ds(pl.multiple_of(j * t, t), t)
        scores = [_dot_nt(qs[p][h], kbf[ks, sl[p]]) + (fqs[p][h] - f_ref[0, 0, 2 * p + h:2 * p + h + 1, ks] * LOG2E)
                  for p in pairs for h in range(2)]
        if diag:
            scores = [jnp.where(row >= col, s, NEG_INF) for s in scores]
        return softmax(scores, carry, [vb[ks, sl[p]] for p in pairs for vb in (vbf0, vbf1)])

    carry = lax.fori_loop(0, qi, lambda j, c: tile(j, c, False), (_softmax_init(t, LANES),) * (2 * len(pairs)))
    carry = tile(qi, carry, True)
    for p in pairs:
        (_, acc0), (_, acc1) = carry[2 * p], carry[2 * p + 1]
        lo = _lane_lo(acc0.shape)
        denom = jnp.where(lo, pltpu.roll(acc0, HEAD_DIM, 1), pltpu.roll(acc1, HEAD_DIM, 1))
        o = jnp.where(lo, acc0, acc1) / denom
        o_ref[0, :, sl[p]] = (_seg_rms(o, og_ref[0, :, sl[p]]) * _silu(g_ref[0, :, sl[p]])).astype(o_ref.dtype)


def _attn_b_kernel(q_ref, k_ref, v_ref, g_ref, u_ref, og_ref, o_ref, kbf, vbf):
    qi = pl.program_id(2)
    t = q_ref.shape[1]
    blk = u_ref.shape[1]

    @pl.when(qi == 0)
    def _():
        def stage(rows, n):
            kbf[rows, :] = k_ref[0, rows, :].astype(BF16)
            vbf[rows, :] = v_ref[0, rows, :].astype(BF16)

        _for_row_blocks(k_ref.shape[1], stage)

    qs = _head_masked(q_ref[0] * (ATT_SCALE * LOG2E))
    row = lax.broadcasted_iota(jnp.int32, (t, t), 0)
    col = lax.broadcasted_iota(jnp.int32, (t, t), 1)
    umat = u_ref[...]

    def tile(j, carry, diag):
        ks = pl.ds(pl.multiple_of(j * t, t), t)
        kt, vt = kbf[ks, :], vbf[ks, :]
        zs = [_dot_nt(q, kt) for q in qs]
        lbs = [jnp.minimum(z, 0.0) - jnp.log2(1.0 + jnp.exp2(-jnp.abs(z))) for z in zs]
        l1s = [lb - z for lb, z in zip(lbs, zs)]
        if diag:
            l1s = [jnp.where(row > col, l1, 0.0) for l1 in l1s]
        tails = [c for c, _ in carry]
        afters = [[] for _ in qs]
        for b in reversed(range(t // blk)):
            for i, l1 in enumerate(l1s):
                x = l1[:, b * blk:(b + 1) * blk]
                hi = x.astype(BF16)
                lo = (x - hi.astype(F32)).astype(BF16)
                afters[i].insert(0, _dot(jnp.concatenate([hi, lo], axis=1), umat) + tails[i])
                tails[i] = tails[i] + jnp.sum(x, axis=-1, keepdims=True)
        out = []
        for lb, after, c_new, (_, acc) in zip(lbs, afters, tails, carry):
            a = jnp.exp2(lb + jnp.concatenate(after, axis=1))
            if diag:
                a = jnp.where(row > col, a, 0.0)
            out.append((c_new, acc + _dot(a.astype(BF16), vt)))
        return tuple(out)

    def alive(state):
        i, carry = state
        live = jnp.maximum(jnp.max(carry[0][0]), jnp.max(carry[1][0])) > STICK_DEAD_LOG2
        return jnp.logical_and(i < qi, live)

    init = ((jnp.zeros((t, 1), F32), jnp.zeros((t, LANES), F32)),) * 2
    _, ((_, a0), (_, a1)) = lax.while_loop(alive, lambda st: (st[0] + 1, tile(qi - 1 - st[0], st[1], False)),
                                           (jnp.int32(0), tile(qi, init, True)))
    o = jnp.where(_lane_lo(a0.shape), a0, a1)
    o_ref[0] = (_seg_rms(o, og_ref[0]) * _silu(g_ref[0])).astype(o_ref.dtype)


def _attn_specs(seq, t, col, pp):
    w = pp * LANES
    q_spec = pl.BlockSpec((1, t, w), lambda b, p, i: (b, i, col // pp + p))
    k_spec = pl.BlockSpec((1, seq, w), lambda b, p, i: (b, 0, (col + N_PAIRS) // pp + p))
    v_spec = pl.BlockSpec((1, seq, w), lambda b, p, i: (b, 0, (col + 2 * N_PAIRS) // pp + p))
    g_spec = pl.BlockSpec((1, t, w), lambda b, p, i: (b, i, (col + 3 * N_PAIRS) // pp + p))
    return [q_spec, k_spec, v_spec, g_spec]


def _pair_spec(pp):
    return pl.BlockSpec((1, 1, pp * LANES), lambda b, p, i: (p, 0, 0))


def _const_spec(shape):
    return pl.BlockSpec(shape, lambda b, p, i: (0,) * len(shape))


def _attn_call(body, proj, col, extra_specs, extra_args, pp=1, v_lanes=(LANES,)):
    bsz, seq, _ = proj.shape
    t = min(ATT_TILE, seq)
    w = pp * LANES
    return pl.pallas_call(
        body,
        grid=(bsz, N_PAIRS // pp, seq // t),
        in_specs=_attn_specs(seq, t, col, pp) + extra_specs,
        out_specs=pl.BlockSpec((1, t, w), lambda b, p, i: (b, i, p)),
        out_shape=jax.ShapeDtypeStruct((bsz, seq, GROUP_W), BF16),
        scratch_shapes=[pltpu.VMEM((seq, w), BF16)] + [pltpu.VMEM((seq, pp * n), BF16) for n in v_lanes],
        compiler_params=_cparams(("arbitrary", "arbitrary", "arbitrary")),
    )(proj, proj, proj, proj, *extra_args)


def _pair_gain(g64):
    return jnp.tile(g64.astype(F32), 2).reshape(1, LANES)


def _score_bound_log2(q_gain, k_gain):
    return (HEAD_DIM * ATT_SCALE * LOG2E * 1.02) * jnp.max(jnp.abs(q_gain)) * jnp.max(jnp.abs(k_gain))


def _attn_a(proj, tz, q_gain, k_gain, o_gain, dlam, lam_init, bounded):
    t = tz.shape[-1]
    pp = ATT_PAIRS
    specs = [pl.BlockSpec((pp, 3, t, t), lambda b, p, i: (p, 0, 0, 0)),
             _const_spec((1, LANES)), _const_spec((1, LANES)), _pair_spec(pp), _const_spec((4, HEAD_DIM))]
    args = (tz, _pair_gain(q_gain), _pair_gain(k_gain), o_gain.reshape(N_PAIRS // pp, 1, pp * LANES), dlam)
    return _attn_call(functools.partial(_attn_a_kernel, lam_init=lam_init, bounded=bounded), proj, COL_A, specs,
                      args, pp=pp, v_lanes=(2 * LANES,))


def _attn_b(proj, o_gain):
    t = min(STICK_BLOCK, proj.shape[1])
    umat = (lax.broadcasted_iota(jnp.int32, (t, t), 0) > lax.broadcasted_iota(jnp.int32, (t, t), 1)).astype(BF16)
    umat = jnp.concatenate([umat, umat], axis=0)
    specs = [_const_spec((2 * t, t)), _pair_spec(1)]
    return _attn_call(_attn_b_kernel, proj, COL_B, specs, (umat, o_gain.reshape(N_PAIRS, 1, LANES)))


def _attn_c(proj, cum_rows, cum_cols, bound, q_gain, k_gain, o_gain, bounded):
    seq = proj.shape[1]
    t = min(ATT_TILE, seq)
    pp = ATT_PAIRS
    specs = [pl.BlockSpec((1, 1, 2 * pp, seq), lambda b, p, i: (b, p, 0, 0)),
             pl.BlockSpec((1, pp, t, LANES), lambda b, p, i: (b, p, i, 0)),
             _const_spec((1, LANES)), _const_spec((1, LANES)), _const_spec((1, LANES)), _pair_spec(pp)]
    args = (cum_rows.reshape(cum_rows.shape[0], N_PAIRS // pp, 2 * pp, seq), cum_cols,
            jnp.full((1, LANES), bound, F32), _pair_gain(q_gain), _pair_gain(k_gain),
            o_gain.reshape(N_PAIRS // pp, 1, pp * LANES))
    return _attn_call(functools.partial(_attn_c_kernel, bounded=bounded), proj, COL_C, specs, args,
                      pp=pp, v_lanes=(LANES, LANES))


def _fox_cum_kernel(s_ref, fb_ref, tri_ref, rows_ref, cols_ref):
    seq = s_ref.shape[1]
    rows_per = tri_ref.shape[0]
    carry = jnp.zeros((1, LANES), F32)
    for c in range(seq // rows_per):
        rows = slice(c * rows_per, (c + 1) * rows_per)
        log_f = _log_sigmoid(s_ref[0, rows, :] + fb_ref[...])
        cum = _dot(tri_ref[...], log_f, precision=HIGHEST) + carry
        carry = cum[rows_per - 1:, :]
        rows_ref[0, :, rows] = cum.T[:8, :]
        lo = _lane_lo(cum.shape)
        for p in range(N_PAIRS):
            cols_ref[0, p, rows, :] = jnp.where(lo, cum[:, 2 * p:2 * p + 1], cum[:, 2 * p + 1:2 * p + 2])


def _fox_cum(proj, forget_bias):
    bsz, seq, _ = proj.shape
    rows_per = min(PREP_ROWS, seq)
    tri = (lax.broadcasted_iota(jnp.int32, (rows_per, rows_per), 0)
           >= lax.broadcasted_iota(jnp.int32, (rows_per, rows_per), 1)).astype(F32)
    fb = jnp.zeros((1, LANES), F32).at[0, :forget_bias.shape[0]].set(forget_bias.astype(F32))
    cum_rows, cum_cols = pl.pallas_call(
        _fox_cum_kernel,
        grid=(bsz,),
        in_specs=[pl.BlockSpec((1, seq, LANES), lambda b: (b, 0, COL_SMALL)),
                  pl.BlockSpec((1, LANES), lambda b: (0, 0)),
                  pl.BlockSpec((rows_per, rows_per), lambda b: (0, 0))],
        out_specs=[pl.BlockSpec((1, 8, seq), lambda b: (b, 0, 0)),
                   pl.BlockSpec((1, N_PAIRS, seq, LANES), lambda b: (b, 0, 0, 0))],
        out_shape=[jax.ShapeDtypeStruct((bsz, 8, seq), F32),
                   jax.ShapeDtypeStruct((bsz, N_PAIRS, seq, LANES), F32)],
        compiler_params=_cparams(("arbitrary",)),
    )(proj, fb, tri)
    return cum_rows.reshape(bsz, N_PAIRS, 2, seq), cum_cols


def _rwkv_prep_kernel(u_ref, s_ref, mu_ref, mus_ref, wup_ref, w0_ref, aup_ref, a0_ref, kkr_ref,
                      r_o, lw_o, k_o, v_o, aa_o, b_o, prev_u, prev_s):
    @pl.when(pl.program_id(1) == 0)
    def _():
        prev_u[...] = jnp.zeros_like(prev_u)
        prev_s[...] = jnp.zeros_like(prev_s)

    def shift(x, prev, mu):
        first = lax.broadcasted_iota(jnp.int32, x.shape, 0) == 0
        x_prev = jnp.where(first, prev[...], pltpu.roll(x, 1, 0))
        prev[...] = x[x.shape[0] - 1:, :]
        return x + (x_prev - x) * mu

    u = shift(u_ref[0], prev_u, mu_ref[...])
    sm = shift(s_ref[0], prev_s, mus_ref[...])
    r, k, v = u[:, :GROUP_W], u[:, GROUP_W:2 * GROUP_W], u[:, 2 * GROUP_W:]
    w_pre = w0_ref[...] + _dot(jnp.tanh(sm), wup_ref[...], precision=HIGHEST)
    w_log = _log_sigmoid(w_pre) - 0.5
    a = 1.0 / (1.0 + jnp.exp(-(a0_ref[...] + _dot(sm, aup_ref[...], precision=HIGHEST))))
    kk = k * kkr_ref[0:1, :]
    kk_n = jnp.concatenate(
        [kk[:, i * LANES:(i + 1) * LANES]
         / jnp.maximum(jnp.sqrt(_seg_sum(kk[:, i * LANES:(i + 1) * LANES] ** 2)), 1e-12)
         for i in range(N_PAIRS)], axis=-1)
    r_o[0] = r
    lw_o[0] = -jnp.exp(w_log)
    k_o[0] = k * (1.0 + (a - 1.0) * kkr_ref[1:2, :])
    v_o[0] = v
    aa_o[0] = -kk_n
    b_o[0] = kk_n * a


def _rwkv_prep(proj, mu, w_up, w0, a_up, a0, kkr):
    bsz, seq, _ = proj.shape
    ts = min(PREP_ROWS, seq)
    mu = mu.astype(F32)
    mu_u = mu[:3 * GROUP_W].reshape(1, 3 * GROUP_W)
    mu_s = (jnp.zeros((1, LANES), F32)
            .at[0, SMALL_WLO:SMALL_WLO + RWKV_RANK].set(mu[3 * GROUP_W:3 * GROUP_W + RWKV_RANK])
            .at[0, SMALL_ALO:SMALL_ALO + RWKV_RANK].set(mu[3 * GROUP_W + RWKV_RANK:]))
    wup = jnp.zeros((LANES, GROUP_W), F32).at[SMALL_WLO:SMALL_WLO + RWKV_RANK].set(w_up.astype(F32))
    aup = jnp.zeros((LANES, GROUP_W), F32).at[SMALL_ALO:SMALL_ALO + RWKV_RANK].set(a_up.astype(F32))
    vec = lambda a: a.astype(F32).reshape(1, GROUP_W)
    full = lambda shape: pl.BlockSpec(shape, lambda b, i: (0,) * len(shape))
    out_spec = pl.BlockSpec((1, ts, GROUP_W), lambda b, i: (b, i, 0))
    out_sds = jax.ShapeDtypeStruct((bsz, seq, GROUP_W), F32)
    return pl.pallas_call(
        _rwkv_prep_kernel,
        grid=(bsz, seq // ts),
        in_specs=[pl.BlockSpec((1, ts, 3 * GROUP_W), lambda b, i: (b, i, COL_D * LANES // (3 * GROUP_W))),
                  pl.BlockSpec((1, ts, LANES), lambda b, i: (b, i, COL_SMALL)),
                  full((1, 3 * GROUP_W)), full((1, LANES)), full((LANES, GROUP_W)), full((1, GROUP_W)),
                  full((LANES, GROUP_W)), full((1, GROUP_W)), full((3, GROUP_W))],
        out_specs=[out_spec] * 6,
        out_shape=[out_sds] * 6,
        scratch_shapes=[pltpu.VMEM((1, 3 * GROUP_W), F32), pltpu.VMEM((1, LANES), F32)],
        compiler_params=_cparams(("arbitrary", "arbitrary")),
    )(proj, proj, mu_u, mu_s, wup, vec(w0), aup, vec(a0), kkr.astype(F32))


def _pieces(x, n):
    out = []
    for i in range(n):
        p = x.astype(BF16)
        out.append(p)
        if i + 1 < n:
            x = x - p.astype(F32)
    return tuple(out)


def _mm(a, b, nt=False):
    n = max(len(a), len(b))
    dot = _dot_nt if nt else _dot
    acc = None
    for i, ai in enumerate(a):
        for j, bj in enumerate(b):
            if i + j < n:
                term = dot(ai, bj)
                acc = term if acc is None else acc + term
    return acc


def _neumann_inverses(mats):
    n = mats[0].shape[0]
    eye = (lax.broadcasted_iota(jnp.int32, (n, n), 0) == lax.broadcasted_iota(jnp.int32, (n, n), 1)).astype(F32)
    ps = [eye + a for a in mats]
    xs = [_pieces(a, RWKV_PIECES) for a in mats]
    for _ in range(int(math.log2(n)) - 1):
        xs = [_pieces(_mm(x, x), RWKV_PIECES) for x in xs]
        ps = [p + _mm(_pieces(p, RWKV_PIECES), x) for p, x in zip(ps, xs)]
    return ps


def _rwkv_scan_kernel(r_ref, lw_ref, k_ref, v_ref, aa_ref, b_ref, g_ref, lng_ref, lnb_ref, rk_ref, tri_ref,
                      o_ref, state):
    c = r_ref.shape[1]
    n1, ng = RWKV_PIECES, RWKV_PIECES_G

    @pl.when(pl.program_id(1) == 0)
    def _():
        state[...] = jnp.zeros_like(state)

    row = lax.broadcasted_iota(jnp.int32, (c, c), 0)
    col = lax.broadcasted_iota(jnp.int32, (c, c), 1)
    strict, incl = row > col, row >= col
    lo = _lane_lo((c, LANES))
    lo2 = _lane_lo((2 * c, LANES))
    diag_blocks = ((lax.broadcasted_iota(jnp.int32, (LANES, LANES), 0) < HEAD_DIM)
                   == (lax.broadcasted_iota(jnp.int32, (LANES, LANES), 1) < HEAD_DIM))
    tri = (tri_ref[...],)
    units = [(bi, p) for bi in range(r_ref.shape[0]) for p in range(N_PAIRS)]
    lanes = [slice(p * LANES, (p + 1) * LANES) for _, p in units]
    nu = len(units)

    st = []
    for i, (bi, _) in enumerate(units):
        r, lw, k, v = (ref[bi, :, lanes[i]] for ref in (r_ref, lw_ref, k_ref, v_ref))
        aa, b = aa_ref[bi, :, lanes[i]], b_ref[bi, :, lanes[i]]
        cl = _mm(tri, _pieces(lw, 3))
        cl_end = cl[c - 1:c, :]
        cl_mid = cl[c // 2 - 1:c // 2, :]
        w_inv = jnp.exp(cl_mid - cl)
        w_rest = jnp.exp(cl_end - cl)
        qa = jnp.concatenate([aa * jnp.exp(cl - lw - cl_mid), r * jnp.exp(cl - cl_mid)], axis=0)
        bk = _pieces(jnp.concatenate([b * w_inv, k * w_inv], axis=0), ng)
        qa_s = jnp.concatenate([aa * jnp.exp(cl - lw), r * jnp.exp(cl)], axis=0)
        bk_end = _pieces(jnp.concatenate([b * w_rest, k * w_rest], axis=0), n1)
        st.append(dict(r=r, k=k, v=v, qa=qa, bk=bk, qa_s=qa_s, bk_end=bk_end, w_end=jnp.exp(cl_end),
                       v_p=_pieces(v, n1)))

    bases = [_mm(_pieces(st[i]["qa_s"], n1), _pieces(state[i], n1), nt=True) for i in range(nu)]
    a_ab, a_ak, a_r = [], [], []
    for i in range(nu):
        for h in range(2):
            g = _mm(_pieces(jnp.where(lo2 == (h == 0), st[i]["qa"], 0.0), ng), st[i]["bk"], nt=True)
            a_ab.append(jnp.where(strict, g[:c, :c], 0.0))
            a_ak.append(jnp.where(strict, g[:c, c:], 0.0))
            a_r.append(jnp.concatenate([jnp.where(incl, g[c:, :c], 0.0), jnp.where(incl, g[c:, c:], 0.0)], axis=1))
    rhs = [bases[j // 2][:c] + _mm(_pieces(a_ak[j], n1), st[j // 2]["v_p"]) for j in range(2 * nu)]

    t_inv = _neumann_inverses(a_ab)
    u_heads = [_mm(_pieces(t_inv[j], n1), _pieces(rhs[j], n1)) for j in range(2 * nu)]
    for i, (bi, _) in enumerate(units):
        r, k, v = st[i]["r"], st[i]["k"], st[i]["v"]
        u = jnp.where(lo, u_heads[2 * i], u_heads[2 * i + 1])
        uv = _pieces(jnp.concatenate([u, v], axis=0), n1)
        y = bases[i][c:] + jnp.where(lo, _mm(_pieces(a_r[2 * i], n1), uv), _mm(_pieces(a_r[2 * i + 1], n1), uv))
        upd = _mm(_pieces(jnp.concatenate([u.T, v.T], axis=1), n1), st[i]["bk_end"])
        state[i] = state[i] * st[i]["w_end"] + jnp.where(diag_blocks, upd, 0.0)

        mean = _seg_sum(y) * (1.0 / HEAD_DIM)
        yc = y - mean
        var = _seg_sum(yc * yc) * (1.0 / HEAD_DIM)
        out = yc * lax.rsqrt(var + RWKV_LN_EPS) * lng_ref[:, lanes[i]] + lnb_ref[:, lanes[i]]
        out = out + _seg_sum(r * k * rk_ref[:, lanes[i]]) * v
        o_ref[bi, :, lanes[i]] = (out * _silu(g_ref[bi, :, lanes[i]])).astype(o_ref.dtype)


def _rwkv_scan(proj, prepped, ln_gain, ln_bias, r_k):
    bsz, seq, _ = proj.shape
    c = min(RWKV_CHUNK, seq)
    tri = (lax.broadcasted_iota(jnp.int32, (c, c), 0) >= lax.broadcasted_iota(jnp.int32, (c, c), 1)).astype(BF16)
    nb = RWKV_ROWS if bsz % RWKV_ROWS == 0 else 1
    tok_spec = pl.BlockSpec((nb, c, GROUP_W), lambda b, i: (b, i, 0))
    vec_spec = pl.BlockSpec((1, GROUP_W), lambda b, i: (0, 0))
    vec = lambda a: a.astype(F32).reshape(1, GROUP_W)
    return pl.pallas_call(
        _rwkv_scan_kernel,
        grid=(bsz // nb, seq // c),
        in_specs=[tok_spec] * 6
        + [pl.BlockSpec((nb, c, GROUP_W), lambda b, i: (b, i, COL_DG * LANES // GROUP_W)),
           vec_spec, vec_spec, vec_spec, pl.BlockSpec((c, c), lambda b, i: (0, 0))],
        out_specs=tok_spec,
        out_shape=jax.ShapeDtypeStruct((bsz, seq, GROUP_W), BF16),
        scratch_shapes=[pltpu.VMEM((nb * N_PAIRS, LANES, LANES), F32)],
        compiler_params=_cparams(("arbitrary", "arbitrary")),
    )(*prepped, proj, vec(ln_gain), vec(ln_bias), vec(r_k), tri)


def _out_kernel(a_ref, b_ref, c_ref, d_ref, w_ref, x_ref, o_ref):
    acc = x_ref[...]
    for i, m_ref in enumerate((a_ref, b_ref, c_ref, d_ref)):
        acc = acc + _dot(m_ref[...], w_ref[i * GROUP_W:(i + 1) * GROUP_W, :])
    o_ref[...] = acc


def _out_proj(groups, w_out, x2d):
    m, d = x2d.shape
    tm = min(512, m)
    g_spec = pl.BlockSpec((tm, GROUP_W), lambda i: (i, 0))
    return pl.pallas_call(
        _out_kernel,
        grid=(m // tm,),
        in_specs=[g_spec] * 4 + [pl.BlockSpec(w_out.shape, lambda i: (0, 0)),
                                 pl.BlockSpec((tm, d), lambda i: (i, 0))],
        out_specs=pl.BlockSpec((tm, d), lambda i: (i, 0)),
        out_shape=jax.ShapeDtypeStruct((m, d), F32),
        compiler_params=_cparams(("arbitrary",)),
    )(*[g.reshape(m, GROUP_W) for g in groups], w_out, x2d)


def _t5_causal_bucket(dist):
    max_exact = NUM_BUCKETS // 2
    d = jnp.maximum(dist, 1).astype(F32)
    large = max_exact + (jnp.log(d / max_exact) / math.log(MAX_DISTANCE / max_exact)
                         * (NUM_BUCKETS - max_exact)).astype(jnp.int32)
    large = jnp.minimum(large, NUM_BUCKETS - 1)
    return jnp.where(dist < max_exact, dist, large)


def _bias_tiles(rel_bias, seq, t):
    assert t >= MAX_DISTANCE
    bias_by_dist = rel_bias.astype(F32)[_t5_causal_bucket(jnp.arange(seq))] * LOG2E
    m = np.arange(2 * t)
    tiles = []
    for delta in range(3):
        dist = np.where(m < t, delta * t - m, delta * t + 2 * t - m)
        v = bias_by_dist[np.clip(dist, 0, seq - 1)].T
        skew = jnp.tile(v, (1, t))[:, :t * (2 * t - 1)].reshape(-1, t, 2 * t - 1)
        tiles.append(skew[:, :, :t])
    return jnp.stack(tiles, axis=1)


def _pack_w_in(w):
    d = w.shape[0]
    att = w[:, :12 * GROUP_W]
    cf = w[:, 12 * GROUP_W:12 * GROUP_W + 8]
    off = 12 * GROUP_W + 8
    rkv = w[:, off:off + 3 * GROUP_W]
    w_lo = w[:, off + 3 * GROUP_W:off + 3 * GROUP_W + RWKV_RANK]
    a_lo = w[:, off + 3 * GROUP_W + RWKV_RANK:off + 3 * GROUP_W + 2 * RWKV_RANK]
    dg = w[:, off + 3 * GROUP_W + 2 * RWKV_RANK:]
    small = (jnp.zeros((d, LANES), w.dtype).at[:, :8].set(cf)
             .at[:, SMALL_WLO:SMALL_WLO + RWKV_RANK].set(w_lo)
             .at[:, SMALL_ALO:SMALL_ALO + RWKV_RANK].set(a_lo))
    return jnp.concatenate([att, rkv, dg, small], axis=1).astype(BF16)


def kernel(x, norm_gain, w_in, w_out, rel_bias, qk_gain, diff_lambda, forget_bias, out_gain, rwkv_mu, rwkv_w_up,
           rwkv_w0, rwkv_a_up, rwkv_a0, rwkv_kkr, rwkv_ln_gain, rwkv_ln_bias):
    bsz, seq, d = x.shape
    tz = _bias_tiles(rel_bias, seq, min(ATT_TILE, seq))
    bias_max = jnp.max(jnp.abs(rel_bias.astype(F32))) * LOG2E
    h = x.reshape(bsz * seq, d)
    for l in range(DEPTH):
        og = out_gain[l].astype(F32)
        proj = _proj(h, norm_gain[l].astype(F32), _pack_w_in(w_in[l])).reshape(bsz, seq, N_PROJ)
        lam_init = 0.8 - 0.6 * math.exp(-0.3 * l)
        qg = qk_gain[l].astype(F32)
        bound_a = _score_bound_log2(qg[0], qg[1]) + bias_max
        bound_c = _score_bound_log2(qg[2], qg[3])
        att_a = functools.partial(_attn_a, proj, q_gain=qg[0], k_gain=qg[1], o_gain=og[:GROUP_W],
                                  dlam=diff_lambda[l].astype(F32), lam_init=lam_init)
        oa = lax.cond(bound_a <= MAX_SHIFT_LOG2, lambda: att_a(tz=tz - bound_a, bounded=True),
                      lambda: att_a(tz=tz, bounded=False))
        ob = _attn_b(proj, og[GROUP_W:2 * GROUP_W])
        att_c = functools.partial(_attn_c, proj, *_fox_cum(proj, forget_bias[l]), bound_c, qg[2], qg[3],
                                  og[2 * GROUP_W:3 * GROUP_W])
        oc = lax.cond(bound_c <= MAX_SHIFT_LOG2, lambda: att_c(bounded=True), lambda: att_c(bounded=False))
        prepped = _rwkv_prep(proj, rwkv_mu[l], rwkv_w_up[l], rwkv_w0[l], rwkv_a_up[l], rwkv_a0[l], rwkv_kkr[l])
        od = _rwkv_scan(proj, prepped, rwkv_ln_gain[l].astype(F32) * og[3 * GROUP_W:], rwkv_ln_bias[l],
                        rwkv_kkr[l, 2])
        h = _out_proj((oa, ob, oc, od), w_out[l].astype(BF16), h)
    return h.reshape(bsz, seq, d)
```

```python
import functools
import math

import numpy as np
import jax
import jax.numpy as jnp
from jax import lax
from jax.experimental import pallas as pl
from jax.experimental.pallas import tpu as pltpu

F32 = jnp.float32
BF16 = jnp.bfloat16
HIGHEST = lax.Precision.HIGHEST

D_MODEL = 1024
DEPTH = 2
GROUP_W = 512
HEAD_DIM = 64
H_A = 4
NUM_BUCKETS = 32
MAX_DISTANCE = 128
RWKV_RANK = 32
NORM_EPS = 1e-6
RWKV_LN_EPS = 64e-5
NEG_INF = -1e30

LANES = 128
N_PAIRS = GROUP_W // LANES
ATT_SCALE = HEAD_DIM ** -0.5
LOG2E = math.log2(math.e)

COL_A, COL_B, COL_C = 0, 16, 32
COL_D = 48
COL_DG = 60
COL_SMALL = 64
N_PROJ = (COL_SMALL + 1) * LANES
SMALL_WLO, SMALL_ALO = 32, 64

ATT_TILE = 512
ATT_PAIRS = 2
MAX_SHIFT_LOG2 = 56.0
STICK_TILE = 256
STICK_BLOCK = 256
STICK_DEAD_LOG2 = -152.0
PREP_ROWS = 512
RWKV_CHUNK = 128
RWKV_ROWS = 2
RWKV_PIECES = 1
RWKV_PIECES_G = 2
VMEM_LIMIT = 56 * 1024 * 1024


def _cparams(sem):
    return pltpu.CompilerParams(dimension_semantics=sem, vmem_limit_bytes=VMEM_LIMIT)


def _dot(a, b, **kw):
    return jnp.dot(a, b, preferred_element_type=F32, **kw)


def _dot_nt(a, b, **kw):
    return lax.dot_general(a, b, (((1,), (1,)), ((), ())), preferred_element_type=F32, **kw)


def _lane_lo(shape):
    return lax.broadcasted_iota(jnp.int32, shape, len(shape) - 1) < HEAD_DIM


def _seg_sum(x):
    lo = _lane_lo(x.shape)
    s_lo = jnp.sum(jnp.where(lo, x, 0.0), axis=-1, keepdims=True)
    s_hi = jnp.sum(jnp.where(lo, 0.0, x), axis=-1, keepdims=True)
    return jnp.where(lo, s_lo, s_hi)


def _seg_rms(x, gain):
    return x * lax.rsqrt(_seg_sum(x * x) * (1.0 / HEAD_DIM) + NORM_EPS) * gain


def _log_sigmoid(z):
    return jnp.minimum(z, 0.0) - jnp.log1p(jnp.exp(-jnp.abs(z)))


def _silu(g):
    return g / (1.0 + jnp.exp(-g))


def _proj_kernel(x_ref, g_ref, w_ref, o_ref, h_ref):
    @pl.when(pl.program_id(1) == 0)
    def _():
        x = x_ref[...]
        ms = jnp.mean(x * x, axis=-1, keepdims=True)
        h_ref[...] = (x * lax.rsqrt(ms + NORM_EPS) * g_ref[...]).astype(BF16)

    o_ref[...] = _dot(h_ref[...], w_ref[...])


def _proj(x2d, gain, w_packed):
    m, d = x2d.shape
    n = w_packed.shape[1]
    tm = min(1024, m)
    tn = n // 5
    return pl.pallas_call(
        _proj_kernel,
        grid=(m // tm, n // tn),
        in_specs=[pl.BlockSpec((tm, d), lambda i, j: (i, 0)),
                  pl.BlockSpec((1, d), lambda i, j: (0, 0)),
                  pl.BlockSpec((d, tn), lambda i, j: (0, j))],
        out_specs=pl.BlockSpec((tm, tn), lambda i, j: (i, j)),
        out_shape=jax.ShapeDtypeStruct((m, n), F32),
        scratch_shapes=[pltpu.VMEM((tm, d), BF16)],
        compiler_params=_cparams(("arbitrary", "arbitrary")),
    )(x2d, gain.reshape(1, d), w_packed)


def _for_row_blocks(seq, fn):
    rows_per = min(PREP_ROWS, seq)

    def body(c, carry):
        fn(pl.ds(pl.multiple_of(c * rows_per, rows_per), rows_per), rows_per)
        return carry

    lax.fori_loop(0, seq // rows_per, body, 0)


def _head_masked(q):
    lo = _lane_lo(q.shape)
    return jnp.where(lo, q, 0.0).astype(BF16), jnp.where(lo, 0.0, q).astype(BF16)


def _online_softmax(scores, carry, values):
    m_new = [jnp.maximum(m, jnp.max(s, axis=-1, keepdims=True)) for s, (m, _) in zip(scores, carry)]
    probs = [jnp.exp2(s - m).astype(BF16) for s, m in zip(scores, m_new)]
    return tuple((mn, jnp.exp2(m - mn) * acc + _dot(p, v))
                 for p, mn, (m, acc), v in zip(probs, m_new, carry, values))


def _softmax_init(t, n):
    return (jnp.full((t, 1), NEG_INF, F32), jnp.zeros((t, n), F32))


def _shifted_softmax(scores, carry, values):
    return tuple((m, acc + _dot(jnp.exp2(s).astype(BF16), v)) for s, (m, acc), v in zip(scores, carry, values))


def _attn_a_kernel(q_ref, k_ref, v_ref, g_ref, tz_ref, qg_ref, kg_ref, og_ref, dl_ref, o_ref, kbf, vbf,
                   *, lam_init, bounded):
    softmax = _shifted_softmax if bounded else _online_softmax
    qi = pl.program_id(2)
    t = q_ref.shape[1]
    heads = range(q_ref.shape[2] // LANES)
    sl = [slice(h * LANES, (h + 1) * LANES) for h in heads]
    vsl = [slice(2 * h * LANES, 2 * (h + 1) * LANES) for h in heads]

    @pl.when(qi == 0)
    def _():
        def stage(rows, n):
            for h in heads:
                kbf[rows, sl[h]] = _seg_rms(k_ref[0, rows, sl[h]], kg_ref[...]).astype(BF16)
                vbf[rows, vsl[h]] = jnp.concatenate([v_ref[0, rows, sl[h]].astype(BF16),
                                                     jnp.ones((n, LANES), BF16)], axis=1)

        _for_row_blocks(k_ref.shape[1], stage)

    qs = [_head_masked(_seg_rms(q_ref[0, :, sl[h]], qg_ref[...]) * (ATT_SCALE * LOG2E)) for h in heads]
    row = lax.broadcasted_iota(jnp.int32, (t, t), 0)
    col = lax.broadcasted_iota(jnp.int32, (t, t), 1)

    def tile(j, carry, diag):
        ks = pl.ds(pl.multiple_of(j * t, t), t)
        delta = jnp.minimum(qi - j, 2)
        scores = [_dot_nt(q, kbf[ks, sl[h]]) + tz_ref[h, delta] for h in heads for q in qs[h]]
        if diag:
            scores = [jnp.where(row >= col, s, NEG_INF) for s in scores]
        return softmax(scores, carry, [vbf[ks, vsl[h]] for h in heads for _ in range(2)])

    carry = lax.fori_loop(0, qi, lambda j, c: tile(j, c, False), (_softmax_init(t, 2 * LANES),) * (2 * len(heads)))
    carry = tile(qi, carry, True)

    dl = dl_ref[...]
    lam = (jnp.exp(jnp.sum(dl[0:1] * dl[1:2], axis=-1, keepdims=True))
           - jnp.exp(jnp.sum(dl[2:3] * dl[3:4], axis=-1, keepdims=True)) + lam_init)
    for h in heads:
        (_, acc0), (_, acc1) = carry[2 * h], carry[2 * h + 1]
        o = acc0[:, :LANES] / acc0[:, LANES:] - lam * (acc1[:, :LANES] / acc1[:, LANES:])
        y = o * lax.rsqrt(jnp.mean(o * o, axis=-1, keepdims=True) + NORM_EPS) * og_ref[0, :, sl[h]] * (1.0 - lam_init)
        o_ref[0, :, sl[h]] = (y * _silu(g_ref[0, :, sl[h]])).astype(o_ref.dtype)


def _attn_c_kernel(q_ref, k_ref, v_ref, g_ref, f_ref, fq_ref, bound_ref, qg_ref, kg_ref, og_ref, o_ref,
                   kbf, vbf0, vbf1, *, bounded):
    softmax = _shifted_softmax if bounded else _online_softmax
    qi = pl.program_id(2)
    t = q_ref.shape[1]
    pairs = range(q_ref.shape[2] // LANES)
    sl = [slice(p * LANES, (p + 1) * LANES) for p in pairs]

    @pl.when(qi == 0)
    def _():
        def stage(rows, n):
            for p in pairs:
                kbf[rows, sl[p]] = _seg_rms(k_ref[0, rows, sl[p]], kg_ref[...]).astype(BF16)
                v = v_ref[0, rows, sl[p]]
                lo = _lane_lo(v.shape)
                vbf0[rows, sl[p]] = jnp.where(lo, v, 1.0).astype(BF16)
                vbf1[rows, sl[p]] = jnp.where(lo, 1.0, v).astype(BF16)

        _for_row_blocks(k_ref.shape[1], stage)

    qs = [_head_masked(_seg_rms(q_ref[0, :, sl[p]], qg_ref[...]) * (ATT_SCALE * LOG2E)) for p in pairs]
    if bounded:
        fqs = [[fq_ref[0, p, :, h * HEAD_DIM:h * HEAD_DIM + 1] * LOG2E - bound_ref[:, :1] for h in range(2)]
               for p in pairs]
    else:
        fqs = [[0.0, 0.0] for _ in pairs]
    row = lax.broadcasted_iota(jnp.int32, (t, t), 0)
    col = lax.broadcasted_iota(jnp.int32, (t, t), 1)

    def tile(j, carry, diag):
        ks = pl.ds(pl.multiple_of(j * t, t), t)
        scores = [_dot_nt(qs[p][h], kbf[ks, sl[p]]) + (fqs[p][h] - f_ref[0, 0, 2 * p + h:2 * p + h + 1, ks] * LOG2E)
                  for p in pairs for h in range(2)]
        if diag:
            scores = [jnp.where(row >= col, s, NEG_INF) for s in scores]
        return softmax(scores, carry, [vb[ks, sl[p]] for p in pairs for vb in (vbf0, vbf1)])

    carry = lax.fori_loop(0, qi, lambda j, c: tile(j, c, False), (_softmax_init(t, LANES),) * (2 * len(pairs)))
    carry = tile(qi, carry, True)
    for p in pairs:
        (_, acc0), (_, acc1) = carry[2 * p], carry[2 * p + 1]
        lo = _lane_lo(acc0.shape)
        denom = jnp.where(lo, pltpu.roll(acc0, HEAD_DIM, 1), pltpu.roll(acc1, HEAD_DIM, 1))
        o = jnp.where(lo, acc0, acc1) / denom
        o_ref[0, :, sl[p]] = (_seg_rms(o, og_ref[0, :, sl[p]]) * _silu(g_ref[0, :, sl[p]])).astype(o_ref.dtype)


def _attn_b_kernel(q_ref, k_ref, v_ref, g_ref, u_ref, og_ref, o_ref, kbf, vbf):
    qi = pl.program_id(2)
    t = q_ref.shape[1]
    blk = u_ref.shape[1]
    pairs = range(q_ref.shape[2] // LANES)
    sl = [slice(p * LANES, (p + 1) * LANES) for p in pairs]

    @pl.when(qi == 0)
    def _():
        def stage(rows, n):
            kbf[rows, :] = k_ref[0, rows, :].astype(BF16)
            vbf[rows, :] = v_ref[0, rows, :].astype(BF16)

        _for_row_blocks(k_ref.shape[1], stage)

    qs = [q for p in pairs for q in _head_masked(q_ref[0, :, sl[p]] * (ATT_SCALE * LOG2E))]
    row = lax.broadcasted_iota(jnp.int32, (t, t), 0)
    col = lax.broadcasted_iota(jnp.int32, (t, t), 1)
    umat = u_ref[...]

    def tile(j, carry, diag):
        ks = pl.ds(pl.multiple_of(j * t, t), t)
        zs = [_dot_nt(q, kbf[ks, sl[i // 2]]) for i, q in enumerate(qs)]
        lbs = [jnp.minimum(z, 0.0) - jnp.log2(1.0 + jnp.exp2(-jnp.abs(z))) for z in zs]
        l1s = [lb - z for lb, z in zip(lbs, zs)]
        if diag:
            l1s = [jnp.where(row > col, l1, 0.0) for l1 in l1s]
        tails = [c for c, _ in carry]
        afters = [[] for _ in qs]
        for b in reversed(range(t // blk)):
            for i, l1 in enumerate(l1s):
                x = l1[:, b * blk:(b + 1) * blk]
                hi = x.astype(BF16)
                lo = (x - hi.astype(F32)).astype(BF16)
                afters[i].insert(0, _dot(jnp.concatenate([hi, lo], axis=1), umat) + tails[i])
                tails[i] = tails[i] + jnp.sum(x, axis=-1, keepdims=True)
        out = []
        for i, (lb, after, c_new, (_, acc)) in enumerate(zip(lbs, afters, tails, carry)):
            a = jnp.exp2(lb + (after[0] if len(after) == 1 else jnp.concatenate(after, axis=1)))
            if diag:
                a = jnp.where(row > col, a, 0.0)
            out.append((c_new, acc + _dot(a.astype(BF16), vbf[ks, sl[i // 2]])))
        return tuple(out)

    def alive(state):
        i, carry = state
        live = functools.reduce(jnp.maximum, [jnp.max(c) for c, _ in carry]) > STICK_DEAD_LOG2
        return jnp.logical_and(i < qi, live)

    init = ((jnp.zeros((t, 1), F32), jnp.zeros((t, LANES), F32)),) * len(qs)
    _, carry = lax.while_loop(alive, lambda st: (st[0] + 1, tile(qi - 1 - st[0], st[1], False)),
                              (jnp.int32(0), tile(qi, init, True)))
    for p in pairs:
        o = jnp.where(_lane_lo((t, LANES)), carry[2 * p][1], carry[2 * p + 1][1])
        o_ref[0, :, sl[p]] = (_seg_rms(o, og_ref[0, :, sl[p]]) * _silu(g_ref[0, :, sl[p]])).astype(o_ref.dtype)


def _attn_specs(seq, t, col, pp):
    w = pp * LANES
    q_spec = pl.BlockSpec((1, t, w), lambda b, p, i: (b, i, col // pp + p))
    k_spec = pl.BlockSpec((1, seq, w), lambda b, p, i: (b, 0, (col + N_PAIRS) // pp + p))
    v_spec = pl.BlockSpec((1, seq, w), lambda b, p, i: (b, 0, (col + 2 * N_PAIRS) // pp + p))
    g_spec = pl.BlockSpec((1, t, w), lambda b, p, i: (b, i, (col + 3 * N_PAIRS) // pp + p))
    return [q_spec, k_spec, v_spec, g_spec]


def _pair_spec(pp):
    return pl.BlockSpec((1, 1, pp * LANES), lambda b, p, i: (p, 0, 0))


def _const_spec(shape):
    return pl.BlockSpec(shape, lambda b, p, i: (0,) * len(shape))


def _attn_call(body, proj, col, extra_specs, extra_args, pp, tile=ATT_TILE, v_lanes=(LANES,)):
    bsz, seq, _ = proj.shape
    t = min(tile, seq)
    w = pp * LANES
    return pl.pallas_call(
        body,
        grid=(bsz, N_PAIRS // pp, seq // t),
        in_specs=_attn_specs(seq, t, col, pp) + extra_specs,
        out_specs=pl.BlockSpec((1, t, w), lambda b, p, i: (b, i, p)),
        out_shape=jax.ShapeDtypeStruct((bsz, seq, GROUP_W), BF16),
        scratch_shapes=[pltpu.VMEM((seq, w), BF16)] + [pltpu.VMEM((seq, pp * n), BF16) for n in v_lanes],
        compiler_params=_cparams(("arbitrary", "arbitrary", "arbitrary")),
    )(proj, proj, proj, proj, *extra_args)


def _pair_gain(g64):
    return jnp.tile(g64.astype(F32), 2).reshape(1, LANES)


def _score_bound_log2(q_gain, k_gain):
    return (HEAD_DIM * ATT_SCALE * LOG2E * 1.02) * jnp.max(jnp.abs(q_gain)) * jnp.max(jnp.abs(k_gain))


def _attn_a(proj, tz, q_gain, k_gain, o_gain, dlam, lam_init, bounded):
    t = tz.shape[-1]
    pp = ATT_PAIRS
    specs = [pl.BlockSpec((pp, 3, t, t), lambda b, p, i: (p, 0, 0, 0)),
             _const_spec((1, LANES)), _const_spec((1, LANES)), _pair_spec(pp), _const_spec((4, HEAD_DIM))]
    args = (tz, _pair_gain(q_gain), _pair_gain(k_gain), o_gain.reshape(N_PAIRS // pp, 1, pp * LANES), dlam)
    return _attn_call(functools.partial(_attn_a_kernel, lam_init=lam_init, bounded=bounded), proj, COL_A, specs,
                      args, pp=pp, v_lanes=(2 * LANES,))


def _attn_b(proj, o_gain):
    t = min(STICK_BLOCK, STICK_TILE, proj.shape[1])
    umat = (lax.broadcasted_iota(jnp.int32, (t, t), 0) > lax.broadcasted_iota(jnp.int32, (t, t), 1)).astype(BF16)
    umat = jnp.concatenate([umat, umat], axis=0)
    pp = ATT_PAIRS
    specs = [_const_spec((2 * t, t)), _pair_spec(pp)]
    return _attn_call(_attn_b_kernel, proj, COL_B, specs, (umat, o_gain.reshape(N_PAIRS // pp, 1, pp * LANES)),
                      pp=pp, tile=STICK_TILE)


def _attn_c(proj, cum_rows, cum_cols, bound, q_gain, k_gain, o_gain, bounded):
    seq = proj.shape[1]
    t = min(ATT_TILE, seq)
    pp = ATT_PAIRS
    specs = [pl.BlockSpec((1, 1, 2 * pp, seq), lambda b, p, i: (b, p, 0, 0)),
             pl.BlockSpec((1, pp, t, LANES), lambda b, p, i: (b, p, i, 0)),
             _const_spec((1, LANES)), _const_spec((1, LANES)), _const_spec((1, LANES)), _pair_spec(pp)]
    args = (cum_rows.reshape(cum_rows.shape[0], N_PAIRS // pp, 2 * pp, seq), cum_cols,
            jnp.full((1, LANES), bound, F32), _pair_gain(q_gain), _pair_gain(k_gain),
            o_gain.reshape(N_PAIRS // pp, 1, pp * LANES))
    return _attn_call(functools.partial(_attn_c_kernel, bounded=bounded), proj, COL_C, specs, args,
                      pp=pp, v_lanes=(LANES, LANES))


def _fox_cum_kernel(s_ref, fb_ref, tri_ref, rows_ref, cols_ref):
    seq = s_ref.shape[1]
    rows_per = tri_ref.shape[0]
    carry = jnp.zeros((1, LANES), F32)
    for c in range(seq // rows_per):
        rows = slice(c * rows_per, (c + 1) * rows_per)
        log_f = _log_sigmoid(s_ref[0, rows, :] + fb_ref[...])
        cum = _dot(tri_ref[...], log_f, precision=HIGHEST) + carry
        carry = cum[rows_per - 1:, :]
        rows_ref[0, :, rows] = cum.T[:8, :]
        lo = _lane_lo(cum.shape)
        for p in range(N_PAIRS):
            cols_ref[0, p, rows, :] = jnp.where(lo, cum[:, 2 * p:2 * p + 1], cum[:, 2 * p + 1:2 * p + 2])


def _fox_cum(proj, forget_bias):
    bsz, seq, _ = proj.shape
    rows_per = min(PREP_ROWS, seq)
    tri = (lax.broadcasted_iota(jnp.int32, (rows_per, rows_per), 0)
           >= lax.broadcasted_iota(jnp.int32, (rows_per, rows_per), 1)).astype(F32)
    fb = jnp.zeros((1, LANES), F32).at[0, :forget_bias.shape[0]].set(forget_bias.astype(F32))
    cum_rows, cum_cols = pl.pallas_call(
        _fox_cum_kernel,
        grid=(bsz,),
        in_specs=[pl.BlockSpec((1, seq, LANES), lambda b: (b, 0, COL_SMALL)),
                  pl.BlockSpec((1, LANES), lambda b: (0, 0)),
                  pl.BlockSpec((rows_per, rows_per), lambda b: (0, 0))],
        out_specs=[pl.BlockSpec((1, 8, seq), lambda b: (b, 0, 0)),
                   pl.BlockSpec((1, N_PAIRS, seq, LANES), lambda b: (b, 0, 0, 0))],
        out_shape=[jax.ShapeDtypeStruct((bsz, 8, seq), F32),
                   jax.ShapeDtypeStruct((bsz, N_PAIRS, seq, LANES), F32)],
        compiler_params=_cparams(("arbitrary",)),
    )(proj, fb, tri)
    return cum_rows.reshape(bsz, N_PAIRS, 2, seq), cum_cols


def _rwkv_prep_kernel(u_ref, s_ref, mu_ref, mus_ref, wup_ref, w0_ref, aup_ref, a0_ref, kkr_ref,
                      r_o, lw_o, k_o, v_o, aa_o, b_o, prev_u, prev_s):
    @pl.when(pl.program_id(1) == 0)
    def _():
        prev_u[...] = jnp.zeros_like(prev_u)
        prev_s[...] = jnp.zeros_like(prev_s)

    def shift(x, prev, mu):
        first = lax.broadcasted_iota(jnp.int32, x.shape, 0) == 0
        x_prev = jnp.where(first, prev[...], pltpu.roll(x, 1, 0))
        prev[...] = x[x.shape[0] - 1:, :]
        return x + (x_prev - x) * mu

    u = shift(u_ref[0], prev_u, mu_ref[...])
    sm = shift(s_ref[0], prev_s, mus_ref[...])
    r, k, v = u[:, :GROUP_W], u[:, GROUP_W:2 * GROUP_W], u[:, 2 * GROUP_W:]
    w_pre = w0_ref[...] + _dot(jnp.tanh(sm), wup_ref[...], precision=HIGHEST)
    w_log = _log_sigmoid(w_pre) - 0.5
    a = 1.0 / (1.0 + jnp.exp(-(a0_ref[...] + _dot(sm, aup_ref[...], precision=HIGHEST))))
    kk = k * kkr_ref[0:1, :]
    kk_n = jnp.concatenate(
        [kk[:, i * LANES:(i + 1) * LANES]
         / jnp.maximum(jnp.sqrt(_seg_sum(kk[:, i * LANES:(i + 1) * LANES] ** 2)), 1e-12)
         for i in range(N_PAIRS)], axis=-1)
    r_o[0] = r
    lw_o[0] = -jnp.exp(w_log)
    k_o[0] = k * (1.0 + (a - 1.0) * kkr_ref[1:2, :])
    v_o[0] = v
    aa_o[0] = -kk_n
    b_o[0] = kk_n * a


def _rwkv_prep(proj, mu, w_up, w0, a_up, a0, kkr):
    bsz, seq, _ = proj.shape
    ts = min(PREP_ROWS, seq)
    mu = mu.astype(F32)
    mu_u = mu[:3 * GROUP_W].reshape(1, 3 * GROUP_W)
    mu_s = (jnp.zeros((1, LANES), F32)
            .at[0, SMALL_WLO:SMALL_WLO + RWKV_RANK].set(mu[3 * GROUP_W:3 * GROUP_W + RWKV_RANK])
            .at[0, SMALL_ALO:SMALL_ALO + RWKV_RANK].set(mu[3 * GROUP_W + RWKV_RANK:]))
    wup = jnp.zeros((LANES, GROUP_W), F32).at[SMALL_WLO:SMALL_WLO + RWKV_RANK].set(w_up.astype(F32))
    aup = jnp.zeros((LANES, GROUP_W), F32).at[SMALL_ALO:SMALL_ALO + RWKV_RANK].set(a_up.astype(F32))
    vec = lambda a: a.astype(F32).reshape(1, GROUP_W)
    full = lambda shape: pl.BlockSpec(shape, lambda b, i: (0,) * len(shape))
    out_spec = pl.BlockSpec((1, ts, GROUP_W), lambda b, i: (b, i, 0))
    out_sds = jax.ShapeDtypeStruct((bsz, seq, GROUP_W), F32)
    return pl.pallas_call(
        _rwkv_prep_kernel,
        grid=(bsz, seq // ts),
        in_specs=[pl.BlockSpec((1, ts, 3 * GROUP_W), lambda b, i: (b, i, COL_D * LANES // (3 * GROUP_W))),
                  pl.BlockSpec((1, ts, LANES), lambda b, i: (b, i, COL_SMALL)),
                  full((1, 3 * GROUP_W)), full((1, LANES)), full((LANES, GROUP_W)), full((1, GROUP_W)),
                  full((LANES, GROUP_W)), full((1, GROUP_W)), full((3, GROUP_W))],
        out_specs=[out_spec] * 6,
        out_shape=[out_sds] * 6,
        scratch_shapes=[pltpu.VMEM((1, 3 * GROUP_W), F32), pltpu.VMEM((1, LANES), F32)],
        compiler_params=_cparams(("arbitrary", "arbitrary")),
    )(proj, proj, mu_u, mu_s, wup, vec(w0), aup, vec(a0), kkr.astype(F32))


def _pieces(x, n):
    out = []
    for i in range(n):
        p = x.astype(BF16)
        out.append(p)
        if i + 1 < n:
            x = x - p.astype(F32)
    return tuple(out)


def _mm(a, b, nt=False):
    n = max(len(a), len(b))
    dot = _dot_nt if nt else _dot
    acc = None
    for i, ai in enumerate(a):
        for j, bj in enumerate(b):
            if i + j < n:
                term = dot(ai, bj)
                acc = term if acc is None else acc + term
    return acc


def _neumann_inverses(mats):
    n = mats[0].shape[0]
    eye = (lax.broadcasted_iota(jnp.int32, (n, n), 0) == lax.broadcasted_iota(jnp.int32, (n, n), 1)).astype(F32)
    ps = [eye + a for a in mats]
    xs = [_pieces(a, RWKV_PIECES) for a in mats]
    for _ in range(int(math.log2(n)) - 1):
        xs = [_pieces(_mm(x, x), RWKV_PIECES) for x in xs]
        ps = [p + _mm(_pieces(p, RWKV_PIECES), x) for p, x in zip(ps, xs)]
    return ps


def _rwkv_scan_kernel(r_ref, lw_ref, k_ref, v_ref, aa_ref, b_ref, g_ref, lng_ref, lnb_ref, rk_ref, tri_ref,
                      o_ref, state):
    c = r_ref.shape[1]
    n1, ng = RWKV_PIECES, RWKV_PIECES_G

    @pl.when(pl.program_id(1) == 0)
    def _():
        state[...] = jnp.zeros_like(state)

    row = lax.broadcasted_iota(jnp.int32, (c, c), 0)
    col = lax.broadcasted_iota(jnp.int32, (c, c), 1)
    strict, incl = row > col, row >= col
    lo = _lane_lo((c, LANES))
    lo2 = _lane_lo((2 * c, LANES))
    diag_blocks = ((lax.broadcasted_iota(jnp.int32, (LANES, LANES), 0) < HEAD_DIM)
                   == (lax.broadcasted_iota(jnp.int32, (LANES, LANES), 1) < HEAD_DIM))
    tri = (tri_ref[...],)
    units = [(bi, p) for bi in range(r_ref.shape[0]) for p in range(N_PAIRS)]
    lanes = [slice(p * LANES, (p + 1) * LANES) for _, p in units]
    nu = len(units)

    st = []
    for i, (bi, _) in enumerate(units):
        r, lw, k, v = (ref[bi, :, lanes[i]] for ref in (r_ref, lw_ref, k_ref, v_ref))
        aa, b = aa_ref[bi, :, lanes[i]], b_ref[bi, :, lanes[i]]
        cl = _mm(tri, _pieces(lw, 3))
        cl_end = cl[c - 1:c, :]
        cl_mid = cl[c // 2 - 1:c // 2, :]
        w_inv = jnp.exp(cl_mid - cl)
        w_rest = jnp.exp(cl_end - cl)
        qa = jnp.concatenate([aa * jnp.exp(cl - lw - cl_mid), r * jnp.exp(cl - cl_mid)], axis=0)
        bk = _pieces(jnp.concatenate([b * w_inv, k * w_inv], axis=0), ng)
        qa_s = jnp.concatenate([aa * jnp.exp(cl - lw), r * jnp.exp(cl)], axis=0)
        bk_end = _pieces(jnp.concatenate([b * w_rest, k * w_rest], axis=0), n1)
        st.append(dict(r=r, k=k, v=v, qa=qa, bk=bk, qa_s=qa_s, bk_end=bk_end, w_end=jnp.exp(cl_end),
                       v_p=_pieces(v, n1)))

    bases = [_mm(_pieces(st[i]["qa_s"], n1), _pieces(state[i], n1), nt=True) for i in range(nu)]
    a_ab, a_ak, a_r = [], [], []
    for i in range(nu):
        for h in range(2):
            g = _mm(_pieces(jnp.where(lo2 == (h == 0), st[i]["qa"], 0.0), ng), st[i]["bk"], nt=True)
            a_ab.append(jnp.where(strict, g[:c, :c], 0.0))
            a_ak.append(jnp.where(strict, g[:c, c:], 0.0))
            a_r.append(jnp.concatenate([jnp.where(incl, g[c:, :c], 0.0), jnp.where(incl, g[c:, c:], 0.0)], axis=1))
    rhs = [bases[j // 2][:c] + _mm(_pieces(a_ak[j], n1), st[j // 2]["v_p"]) for j in range(2 * nu)]

    t_inv = _neumann_inverses(a_ab)
    u_heads = [_mm(_pieces(t_inv[j], n1), _pieces(rhs[j], n1)) for j in range(2 * nu)]
    for i, (bi, _) in enumerate(units):
        r, k, v = st[i]["r"], st[i]["k"], st[i]["v"]
        u = jnp.where(lo, u_heads[2 * i], u_heads[2 * i + 1])
        uv = _pieces(jnp.concatenate([u, v], axis=0), n1)
        y = bases[i][c:] + jnp.where(lo, _mm(_pieces(a_r[2 * i], n1), uv), _mm(_pieces(a_r[2 * i + 1], n1), uv))
        upd = _mm(_pieces(jnp.concatenate([u.T, v.T], axis=1), n1), st[i]["bk_end"])
        state[i] = state[i] * st[i]["w_end"] + jnp.where(diag_blocks, upd, 0.0)

        mean = _seg_sum(y) * (1.0 / HEAD_DIM)
        yc = y - mean
        var = _seg_sum(yc * yc) * (1.0 / HEAD_DIM)
        out = yc * lax.rsqrt(var + RWKV_LN_EPS) * lng_ref[:, lanes[i]] + lnb_ref[:, lanes[i]]
        out = out + _seg_sum(r * k * rk_ref[:, lanes[i]]) * v
        o_ref[bi, :, lanes[i]] = (out * _silu(g_ref[bi, :, lanes[i]])).astype(o_ref.dtype)


def _rwkv_scan(proj, prepped, ln_gain, ln_bias, r_k):
    bsz, seq, _ = proj.shape
    c = min(RWKV_CHUNK, seq)
    tri = (lax.broadcasted_iota(jnp.int32, (c, c), 0) >= lax.broadcasted_iota(jnp.int32, (c, c), 1)).astype(BF16)
    nb = RWKV_ROWS if bsz % RWKV_ROWS == 0 else 1
    tok_spec = pl.BlockSpec((nb, c, GROUP_W), lambda b, i: (b, i, 0))
    vec_spec = pl.BlockSpec((1, GROUP_W), lambda b, i: (0, 0))
    vec = lambda a: a.astype(F32).reshape(1, GROUP_W)
    return pl.pallas_call(
        _rwkv_scan_kernel,
        grid=(bsz // nb, seq // c),
        in_specs=[tok_spec] * 6
        + [pl.BlockSpec((nb, c, GROUP_W), lambda b, i: (b, i, COL_DG * LANES // GROUP_W)),
           vec_spec, vec_spec, vec_spec, pl.BlockSpec((c, c), lambda b, i: (0, 0))],
        out_specs=tok_spec,
        out_shape=jax.ShapeDtypeStruct((bsz, seq, GROUP_W), BF16),
        scratch_shapes=[pltpu.VMEM((nb * N_PAIRS, LANES, LANES), F32)],
        compiler_params=_cparams(("arbitrary", "arbitrary")),
    )(*prepped, proj, vec(ln_gain), vec(ln_bias), vec(r_k), tri)


def _out_kernel(a_ref, b_ref, c_ref, d_ref, w_ref, x_ref, o_ref):
    acc = x_ref[...]
    for i, m_ref in enumerate((a_ref, b_ref, c_ref, d_ref)):
        acc = acc + _dot(m_ref[...], w_ref[i * GROUP_W:(i + 1) * GROUP_W, :])
    o_ref[...] = acc


def _out_proj(groups, w_out, x2d):
    m, d = x2d.shape
    tm = min(512, m)
    g_spec = pl.BlockSpec((tm, GROUP_W), lambda i: (i, 0))
    return pl.pallas_call(
        _out_kernel,
        grid=(m // tm,),
        in_specs=[g_spec] * 4 + [pl.BlockSpec(w_out.shape, lambda i: (0, 0)),
                                 pl.BlockSpec((tm, d), lambda i: (i, 0))],
        out_specs=pl.BlockSpec((tm, d), lambda i: (i, 0)),
        out_shape=jax.ShapeDtypeStruct((m, d), F32),
        compiler_params=_cparams(("arbitrary",)),
    )(*[g.reshape(m, GROUP_W) for g in groups], w_out, x2d)


def _t5_causal_bucket(dist):
    max_exact = NUM_BUCKETS // 2
    d = jnp.maximum(dist, 1).astype(F32)
    large = max_exact + (jnp.log(d / max_exact) / math.log(MAX_DISTANCE / max_exact)
                         * (NUM_BUCKETS - max_exact)).astype(jnp.int32)
    large = jnp.minimum(large, NUM_BUCKETS - 1)
    return jnp.where(dist < max_exact, dist, large)


def _bias_tiles(rel_bias, seq, t):
    assert t >= MAX_DISTANCE
    bias_by_dist = rel_bias.astype(F32)[_t5_causal_bucket(jnp.arange(seq))] * LOG2E
    m = np.arange(2 * t)
    tiles = []
    for delta in range(3):
        dist = np.where(m < t, delta * t - m, delta * t + 2 * t - m)
        v = bias_by_dist[np.clip(dist, 0, seq - 1)].T
        skew = jnp.tile(v, (1, t))[:, :t * (2 * t - 1)].reshape(-1, t, 2 * t - 1)
        tiles.append(skew[:, :, :t])
    return jnp.stack(tiles, axis=1)


def _pack_w_in(w):
    d = w.shape[0]
    att = w[:, :12 * GROUP_W]
    cf = w[:, 12 * GROUP_W:12 * GROUP_W + 8]
    off = 12 * GROUP_W + 8
    rkv = w[:, off:off + 3 * GROUP_W]
    w_lo = w[:, off + 3 * GROUP_W:off + 3 * GROUP_W + RWKV_RANK]
    a_lo = w[:, off + 3 * GROUP_W + RWKV_RANK:off + 3 * GROUP_W + 2 * RWKV_RANK]
    dg = w[:, off + 3 * GROUP_W + 2 * RWKV_RANK:]
    small = (jnp.zeros((d, LANES), w.dtype).at[:, :8].set(cf)
             .at[:, SMALL_WLO:SMALL_WLO + RWKV_RANK].set(w_lo)
             .at[:, SMALL_ALO:SMALL_ALO + RWKV_RANK].set(a_lo))
    return jnp.concatenate([att, rkv, dg, small], axis=1).astype(BF16)


def kernel(x, norm_gain, w_in, w_out, rel_bias, qk_gain, diff_lambda, forget_bias, out_gain, rwkv_mu, rwkv_w_up,
           rwkv_w0, rwkv_a_up, rwkv_a0, rwkv_kkr, rwkv_ln_gain, rwkv_ln_bias):
    bsz, seq, d = x.shape
    tz = _bias_tiles(rel_bias, seq, min(ATT_TILE, seq))
    bias_max = jnp.max(jnp.abs(rel_bias.astype(F32))) * LOG2E
    h = x.reshape(bsz * seq, d)
    for l in range(DEPTH):
        og = out_gain[l].astype(F32)
        proj = _proj(h, norm_gain[l].astype(F32), _pack_w_in(w_in[l])).reshape(bsz, seq, N_PROJ)
        lam_init = 0.8 - 0.6 * math.exp(-0.3 * l)
        qg = qk_gain[l].astype(F32)
        bound_a = _score_bound_log2(qg[0], qg[1]) + bias_max
        bound_c = _score_bound_log2(qg[2], qg[3])
        att_a = functools.partial(_attn_a, proj, q_gain=qg[0], k_gain=qg[1], o_gain=og[:GROUP_W],
                                  dlam=diff_lambda[l].astype(F32), lam_init=lam_init)
        oa = lax.cond(bound_a <= MAX_SHIFT_LOG2, lambda: att_a(tz=tz - bound_a, bounded=True),
                      lambda: att_a(tz=tz, bounded=False))
        ob = _attn_b(proj, og[GROUP_W:2 * GROUP_W])
        att_c = functools.partial(_attn_c, proj, *_fox_cum(proj, forget_bias[l]), bound_c, qg[2], qg[3],
                                  og[2 * GROUP_W:3 * GROUP_W])
        oc = lax.cond(bound_c <= MAX_SHIFT_LOG2, lambda: att_c(bounded=True), lambda: att_c(bounded=False))
        prepped = _rwkv_prep(proj, rwkv_mu[l], rwkv_w_up[l], rwkv_w0[l], rwkv_a_up[l], rwkv_a0[l], rwkv_kkr[l])
        od = _rwkv_scan(proj, prepped, rwkv_ln_gain[l].astype(F32) * og[3 * GROUP_W:], rwkv_ln_bias[l],
                        rwkv_kkr[l, 2])
        h = _out_proj((oa, ob, oc, od), w_out[l].astype(BF16), h)
    return h.reshape(bsz, seq, d)
```

```python
import functools
import math

import numpy as np
import jax
import jax.numpy as jnp
from jax import lax
from jax.experimental import pallas as pl
from jax.experimental.pallas import tpu as pltpu

F32 = jnp.float32
BF16 = jnp.bfloat16
HIGHEST = lax.Precision.HIGHEST

D_MODEL = 1024
DEPTH = 2
GROUP_W = 512
HEAD_DIM = 64
H_A = 4
NUM_BUCKETS = 32
MAX_DISTANCE = 128
RWKV_RANK = 32
NORM_EPS = 1e-6
RWKV_LN_EPS = 64e-5
NEG_INF = -1e30

LANES = 128
N_PAIRS = GROUP_W // LANES
ATT_SCALE = HEAD_DIM ** -0.5
LOG2E = math.log2(math.e)

COL_A, COL_B, COL_C = 0, 16, 32
COL_D = 48
COL_DG = 60
COL_SMALL = 64
N_PROJ = (COL_SMALL + 2) * LANES
PROJ_COL_STEPS = 3
SMALL_WLO, SMALL_ALO = 32, 64

ATT_TILE = 512
ATT_PAIRS = 2
MAX_SHIFT_LOG2 = 56.0
STICK_TILE = 256
STICK_BLOCK = 256
STICK_DEAD_LOG2 = -152.0
PREP_ROWS = 512
RWKV_CHUNK = 128
RWKV_ROWS = 2
RWKV_PIECES = 1
RWKV_PIECES_G = 2
VMEM_LIMIT = 56 * 1024 * 1024


def _cparams(sem):
    return pltpu.CompilerParams(dimension_semantics=sem, vmem_limit_bytes=VMEM_LIMIT)


def _dot(a, b, **kw):
    return jnp.dot(a, b, preferred_element_type=F32, **kw)


def _dot_nt(a, b, **kw):
    return lax.dot_general(a, b, (((1,), (1,)), ((), ())), preferred_element_type=F32, **kw)


def _lane_lo(shape):
    return lax.broadcasted_iota(jnp.int32, shape, len(shape) - 1) < HEAD_DIM


def _seg_sum(x):
    lo = _lane_lo(x.shape)
    s_lo = jnp.sum(jnp.where(lo, x, 0.0), axis=-1, keepdims=True)
    s_hi = jnp.sum(jnp.where(lo, 0.0, x), axis=-1, keepdims=True)
    return jnp.where(lo, s_lo, s_hi)


def _seg_rms(x, gain):
    return x * lax.rsqrt(_seg_sum(x * x) * (1.0 / HEAD_DIM) + NORM_EPS) * gain


def _log_sigmoid(z):
    return jnp.minimum(z, 0.0) - jnp.log1p(jnp.exp(-jnp.abs(z)))


def _silu(g):
    return g / (1.0 + jnp.exp(-g))


def _proj_kernel(x_ref, g_ref, w_ref, o_ref, h_ref):
    @pl.when(pl.program_id(1) == 0)
    def _():
        x = x_ref[...]
        ms = jnp.mean(x * x, axis=-1, keepdims=True)
        h_ref[...] = (x * lax.rsqrt(ms + NORM_EPS) * g_ref[...]).astype(BF16)

    o_ref[...] = _dot(h_ref[...], w_ref[...])


def _proj(x2d, gain, w_packed):
    m, d = x2d.shape
    n = w_packed.shape[1]
    tm = min(1024, m)
    tn = n // PROJ_COL_STEPS
    return pl.pallas_call(
        _proj_kernel,
        grid=(m // tm, n // tn),
        in_specs=[pl.BlockSpec((tm, d), lambda i, j: (i, 0)),
                  pl.BlockSpec((1, d), lambda i, j: (0, 0)),
                  pl.BlockSpec((d, tn), lambda i, j: (0, j))],
        out_specs=pl.BlockSpec((tm, tn), lambda i, j: (i, j)),
        out_shape=jax.ShapeDtypeStruct((m, n), F32),
        scratch_shapes=[pltpu.VMEM((tm, d), BF16)],
        compiler_params=_cparams(("arbitrary", "arbitrary")),
    )(x2d, gain.reshape(1, d), w_packed)


def _for_row_blocks(seq, fn):
    rows_per = min(PREP_ROWS, seq)

    def body(c, carry):
        fn(pl.ds(pl.multiple_of(c * rows_per, rows_per), rows_per), rows_per)
        return carry

    lax.fori_loop(0, seq // rows_per, body, 0)


def _head_masked(q):
    lo = _lane_lo(q.shape)
    return jnp.where(lo, q, 0.0).astype(BF16), jnp.where(lo, 0.0, q).astype(BF16)


def _online_softmax(scores, carry, values):
    m_new = [jnp.maximum(m, jnp.max(s, axis=-1, keepdims=True)) for s, (m, _) in zip(scores, carry)]
    probs = [jnp.exp2(s - m).astype(BF16) for s, m in zip(scores, m_new)]
    return tuple((mn, jnp.exp2(m - mn) * acc + _dot(p, v))
                 for p, mn, (m, acc), v in zip(probs, m_new, carry, values))


def _softmax_init(t, n):
    return (jnp.full((t, 1), NEG_INF, F32), jnp.zeros((t, n), F32))


def _shifted_softmax(scores, carry, values):
    return tuple((m, acc + _dot(jnp.exp2(s).astype(BF16), v)) for s, (m, acc), v in zip(scores, carry, values))


def _causal_tiles(qi, tile, carry):
    carry = lax.fori_loop(0, qi // 2, lambda i, c: tile(2 * i + 1, tile(2 * i, c, False), False), carry)
    carry = lax.fori_loop(2 * (qi // 2), qi, lambda j, c: tile(j, c, False), carry)
    return tile(qi, carry, True)


def _attn_a_kernel(q_ref, k_ref, v_ref, g_ref, tz_ref, qg_ref, kg_ref, og_ref, dl_ref, o_ref, kbf, vbf,
                   *, lam_init, bounded):
    softmax = _shifted_softmax if bounded else _online_softmax
    qi = pl.program_id(2)
    t = q_ref.shape[1]
    heads = range(q_ref.shape[2] // LANES)
    sl = [slice(h * LANES, (h + 1) * LANES) for h in heads]
    vsl = [slice(2 * h * LANES, 2 * (h + 1) * LANES) for h in heads]

    @pl.when(qi == 0)
    def _():
        def stage(rows, n):
            for h in heads:
                kbf[rows, sl[h]] = _seg_rms(k_ref[0, rows, sl[h]], kg_ref[...]).astype(BF16)
                vbf[rows, vsl[h]] = jnp.concatenate([v_ref[0, rows, sl[h]].astype(BF16),
                                                     jnp.ones((n, LANES), BF16)], axis=1)

        _for_row_blocks(k_ref.shape[1], stage)

    qs = [_head_masked(_seg_rms(q_ref[0, :, sl[h]], qg_ref[...]) * (ATT_SCALE * LOG2E)) for h in heads]
    row = lax.broadcasted_iota(jnp.int32, (t, t), 0)
    col = lax.broadcasted_iota(jnp.int32, (t, t), 1)

    def tile(j, carry, diag):
        ks = pl.ds(pl.multiple_of(j * t, t), t)
        delta = jnp.minimum(qi - j, 2)
        scores = [_dot_nt(q, kbf[ks, sl[h]]) + tz_ref[h, delta] for h in heads for q in qs[h]]
        if diag:
            scores = [jnp.where(row >= col, s, NEG_INF) for s in scores]
        return softmax(scores, carry, [vbf[ks, vsl[h]] for h in heads for _ in range(2)])

    carry = _causal_tiles(qi, tile, (_softmax_init(t, 2 * LANES),) * (2 * len(heads)))

    dl = dl_ref[...]
    lam = (jnp.exp(jnp.sum(dl[0:1] * dl[1:2], axis=-1, keepdims=True))
           - jnp.exp(jnp.sum(dl[2:3] * dl[3:4], axis=-1, keepdims=True)) + lam_init)
    for h in heads:
        (_, acc0), (_, acc1) = carry[2 * h], carry[2 * h + 1]
        o = acc0[:, :LANES] / acc0[:, LANES:] - lam * (acc1[:, :LANES] / acc1[:, LANES:])
        y = o * lax.rsqrt(jnp.mean(o * o, axis=-1, keepdims=True) + NORM_EPS) * og_ref[0, :, sl[h]] * (1.0 - lam_init)
        o_ref[0, :, sl[h]] = (y * _silu(g_ref[0, :, sl[h]])).astype(o_ref.dtype)


def _attn_c_kernel(q_ref, k_ref, v_ref, g_ref, f_ref, fq_ref, bound_ref, qg_ref, kg_ref, og_ref, o_ref,
                   kbf, vbf0, vbf1, *, bounded):
    softmax = _shifted_softmax if bounded else _online_softmax
    qi = pl.program_id(2)
    t = q_ref.shape[1]
    pairs = range(q_ref.shape[2] // LANES)
    sl = [slice(p * LANES, (p + 1) * LANES) for p in pairs]

    @pl.when(qi == 0)
    def _():
        def stage(rows, n):
            for p in pairs:
                kbf[rows, sl[p]] = _seg_rms(k_ref[0, rows, sl[p]], kg_ref[...]).astype(BF16)
                v = v_ref[0, rows, sl[p]]
                lo = _lane_lo(v.shape)
                vbf0[rows, sl[p]] = jnp.where(lo, v, 1.0).astype(BF16)
                vbf1[rows, sl[p]] = jnp.where(lo, 1.0, v).astype(BF16)

        _for_row_blocks(k_ref.shape[1], stage)

    qs = [_head_masked(_seg_rms(q_ref[0, :, sl[p]], qg_ref[...]) * (ATT_SCALE * LOG2E)) for p in pairs]
    if bounded:
        fqs = [[fq_ref[0, p, :, h * HEAD_DIM:h * HEAD_DIM + 1] * LOG2E - bound_ref[:, :1] for h in range(2)]
               for p in pairs]
    else:
        fqs = [[0.0, 0.0] for _ in pairs]
    row = lax.broadcasted_iota(jnp.int32, (t, t), 0)
    col = lax.broadcasted_iota(jnp.int32, (t, t), 1)

    def tile(j, carry, diag):
        ks = pl.ds(pl.multiple_of(j * t, t), t)
        scores = [_dot_nt(qs[p][h], kbf[ks, sl[p]]) + (fqs[p][h] - f_ref[0, 0, 2 * p + h:2 * p + h + 1, ks] * LOG2E)
                  for p in pairs for h in range(2)]
        if diag:
            scores = [jnp.where(row >= col, s, NEG_INF) for s in scores]
        return softmax(scores, carry, [vb[ks, sl[p]] for p in pairs for vb in (vbf0, vbf1)])

    carry = _causal_tiles(qi, tile, (_softmax_init(t, LANES),) * (2 * len(pairs)))
    for p in pairs:
        (_, acc0), (_, acc1) = carry[2 * p], carry[2 * p + 1]
        lo = _lane_lo(acc0.shape)
        denom = jnp.where(lo, pltpu.roll(acc0, HEAD_DIM, 1), pltpu.roll(acc1, HEAD_DIM, 1))
        o = jnp.where(lo, acc0, acc1) / denom
        o_ref[0, :, sl[p]] = (_seg_rms(o, og_ref[0, :, sl[p]]) * _silu(g_ref[0, :, sl[p]])).astype(o_ref.dtype)


def _attn_b_kernel(q_ref, k_ref, v_ref, g_ref, u_ref, og_ref, o_ref, kbf, vbf):
    qi = pl.program_id(2)
    t = q_ref.shape[1]
    blk = u_ref.shape[1]
    pairs = range(q_ref.shape[2] // LANES)
    sl = [slice(p * LANES, (p + 1) * LANES) for p in pairs]

    @pl.when(qi == 0)
    def _():
        def stage(rows, n):
            kbf[rows, :] = k_ref[0, rows, :].astype(BF16)
            vbf[rows, :] = v_ref[0, rows, :].astype(BF16)

        _for_row_blocks(k_ref.shape[1], stage)

    qs = [q for p in pairs for q in _head_masked(q_ref[0, :, sl[p]] * (ATT_SCALE * LOG2E))]
    row = lax.broadcasted_iota(jnp.int32, (t, t), 0)
    col = lax.broadcasted_iota(jnp.int32, (t, t), 1)
    umat = u_ref[...]

    def tile(j, carry, diag):
        ks = pl.ds(pl.multiple_of(j * t, t), t)
        zs = [_dot_nt(q, kbf[ks, sl[i // 2]]) for i, q in enumerate(qs)]
        lbs = [jnp.minimum(z, 0.0) - jnp.log2(1.0 + jnp.exp2(-jnp.abs(z))) for z in zs]
        l1s = [lb - z for lb, z in zip(lbs, zs)]
        if diag:
            l1s = [jnp.where(row > col, l1, 0.0) for l1 in l1s]
        tails = [c for c, _ in carry]
        afters = [[] for _ in qs]
        for b in reversed(range(t // blk)):
            for i, l1 in enumerate(l1s):
                x = l1[:, b * blk:(b + 1) * blk]
                hi = x.astype(BF16)
                lo = (x - hi.astype(F32)).astype(BF16)
                afters[i].insert(0, _dot(jnp.concatenate([hi, lo], axis=1), umat) + tails[i])
                tails[i] = tails[i] + jnp.sum(x, axis=-1, keepdims=True)
        out = []
        for i, (lb, after, c_new, (_, acc)) in enumerate(zip(lbs, afters, tails, carry)):
            a = jnp.exp2(lb + (after[0] if len(after) == 1 else jnp.concatenate(after, axis=1)))
            if diag:
                a = jnp.where(row > col, a, 0.0)
            out.append((c_new, acc + _dot(a.astype(BF16), vbf[ks, sl[i // 2]])))
        return tuple(out)

    def alive(state):
        i, carry = state
        live = functools.reduce(jnp.maximum, [jnp.max(c) for c, _ in carry]) > STICK_DEAD_LOG2
        return jnp.logical_and(i < qi, live)

    init = ((jnp.zeros((t, 1), F32), jnp.zeros((t, LANES), F32)),) * len(qs)
    _, carry = lax.while_loop(alive, lambda st: (st[0] + 1, tile(qi - 1 - st[0], st[1], False)),
                              (jnp.int32(0), tile(qi, init, True)))
    for p in pairs:
        o = jnp.where(_lane_lo((t, LANES)), carry[2 * p][1], carry[2 * p + 1][1])
        o_ref[0, :, sl[p]] = (_seg_rms(o, og_ref[0, :, sl[p]]) * _silu(g_ref[0, :, sl[p]])).astype(o_ref.dtype)


def _attn_specs(seq, t, col, pp):
    w = pp * LANES
    q_spec = pl.BlockSpec((1, t, w), lambda b, p, i: (b, i, col // pp + p))
    k_spec = pl.BlockSpec((1, seq, w), lambda b, p, i: (b, 0, (col + N_PAIRS) // pp + p))
    v_spec = pl.BlockSpec((1, seq, w), lambda b, p, i: (b, 0, (col + 2 * N_PAIRS) // pp + p))
    g_spec = pl.BlockSpec((1, t, w), lambda b, p, i: (b, i, (col + 3 * N_PAIRS) // pp + p))
    return [q_spec, k_spec, v_spec, g_spec]


def _pair_spec(pp):
    return pl.BlockSpec((1, 1, pp * LANES), lambda b, p, i: (p, 0, 0))


def _const_spec(shape):
    return pl.BlockSpec(shape, lambda b, p, i: (0,) * len(shape))


def _attn_call(body, proj, col, extra_specs, extra_args, pp, tile=ATT_TILE, v_lanes=(LANES,)):
    bsz, seq, _ = proj.shape
    t = min(tile, seq)
    w = pp * LANES
    return pl.pallas_call(
        body,
        grid=(bsz, N_PAIRS // pp, seq // t),
        in_specs=_attn_specs(seq, t, col, pp) + extra_specs,
        out_specs=pl.BlockSpec((1, t, w), lambda b, p, i: (b, i, p)),
        out_shape=jax.ShapeDtypeStruct((bsz, seq, GROUP_W), BF16),
        scratch_shapes=[pltpu.VMEM((seq, w), BF16)] + [pltpu.VMEM((seq, pp * n), BF16) for n in v_lanes],
        compiler_params=_cparams(("arbitrary", "arbitrary", "arbitrary")),
    )(proj, proj, proj, proj, *extra_args)


def _pair_gain(g64):
    return jnp.tile(g64.astype(F32), 2).reshape(1, LANES)


def _score_bound_log2(q_gain, k_gain):
    return (HEAD_DIM * ATT_SCALE * LOG2E * 1.02) * jnp.max(jnp.abs(q_gain)) * jnp.max(jnp.abs(k_gain))


def _attn_a(proj, tz, q_gain, k_gain, o_gain, dlam, lam_init, bounded):
    t = tz.shape[-1]
    pp = ATT_PAIRS
    specs = [pl.BlockSpec((pp, 3, t, t), lambda b, p, i: (p, 0, 0, 0)),
             _const_spec((1, LANES)), _const_spec((1, LANES)), _pair_spec(pp), _const_spec((4, HEAD_DIM))]
    args = (tz, _pair_gain(q_gain), _pair_gain(k_gain), o_gain.reshape(N_PAIRS // pp, 1, pp * LANES), dlam)
    return _attn_call(functools.partial(_attn_a_kernel, lam_init=lam_init, bounded=bounded), proj, COL_A, specs,
                      args, pp=pp, v_lanes=(2 * LANES,))


def _attn_b(proj, o_gain):
    t = min(STICK_BLOCK, STICK_TILE, proj.shape[1])
    umat = (lax.broadcasted_iota(jnp.int32, (t, t), 0) > lax.broadcasted_iota(jnp.int32, (t, t), 1)).astype(BF16)
    umat = jnp.concatenate([umat, umat], axis=0)
    pp = ATT_PAIRS
    specs = [_const_spec((2 * t, t)), _pair_spec(pp)]
    return _attn_call(_attn_b_kernel, proj, COL_B, specs, (umat, o_gain.reshape(N_PAIRS // pp, 1, pp * LANES)),
                      pp=pp, tile=STICK_TILE)


def _attn_c(proj, cum_rows, cum_cols, bound, q_gain, k_gain, o_gain, bounded):
    seq = proj.shape[1]
    t = min(ATT_TILE, seq)
    pp = ATT_PAIRS
    specs = [pl.BlockSpec((1, 1, 2 * pp, seq), lambda b, p, i: (b, p, 0, 0)),
             pl.BlockSpec((1, pp, t, LANES), lambda b, p, i: (b, p, i, 0)),
             _const_spec((1, LANES)), _const_spec((1, LANES)), _const_spec((1, LANES)), _pair_spec(pp)]
    args = (cum_rows.reshape(cum_rows.shape[0], N_PAIRS // pp, 2 * pp, seq), cum_cols,
            jnp.full((1, LANES), bound, F32), _pair_gain(q_gain), _pair_gain(k_gain),
            o_gain.reshape(N_PAIRS // pp, 1, pp * LANES))
    return _attn_call(functools.partial(_attn_c_kernel, bounded=bounded), proj, COL_C, specs, args,
                      pp=pp, v_lanes=(LANES, LANES))


def _fox_cum_kernel(s_ref, fb_ref, tri_ref, rows_ref, cols_ref):
    seq = s_ref.shape[1]
    rows_per = tri_ref.shape[0]
    carry = jnp.zeros((1, LANES), F32)
    for c in range(seq // rows_per):
        rows = slice(c * rows_per, (c + 1) * rows_per)
        log_f = _log_sigmoid(s_ref[0, rows, :] + fb_ref[...])
        cum = _dot(tri_ref[...], log_f, precision=HIGHEST) + carry
        carry = cum[rows_per - 1:, :]
        rows_ref[0, :, rows] = cum.T[:8, :]
        lo = _lane_lo(cum.shape)
        for p in range(N_PAIRS):
            cols_ref[0, p, rows, :] = jnp.where(lo, cum[:, 2 * p:2 * p + 1], cum[:, 2 * p + 1:2 * p + 2])


def _fox_cum(proj, forget_bias):
    bsz, seq, _ = proj.shape
    rows_per = min(PREP_ROWS, seq)
    tri = (lax.broadcasted_iota(jnp.int32, (rows_per, rows_per), 0)
           >= lax.broadcasted_iota(jnp.int32, (rows_per, rows_per), 1)).astype(F32)
    fb = jnp.zeros((1, LANES), F32).at[0, :forget_bias.shape[0]].set(forget_bias.astype(F32))
    cum_rows, cum_cols = pl.pallas_call(
        _fox_cum_kernel,
        grid=(bsz,),
        in_specs=[pl.BlockSpec((1, seq, LANES), lambda b: (b, 0, COL_SMALL)),
                  pl.BlockSpec((1, LANES), lambda b: (0, 0)),
                  pl.BlockSpec((rows_per, rows_per), lambda b: (0, 0))],
        out_specs=[pl.BlockSpec((1, 8, seq), lambda b: (b, 0, 0)),
                   pl.BlockSpec((1, N_PAIRS, seq, LANES), lambda b: (b, 0, 0, 0))],
        out_shape=[jax.ShapeDtypeStruct((bsz, 8, seq), F32),
                   jax.ShapeDtypeStruct((bsz, N_PAIRS, seq, LANES), F32)],
        compiler_params=_cparams(("arbitrary",)),
    )(proj, fb, tri)
    return cum_rows.reshape(bsz, N_PAIRS, 2, seq), cum_cols


def _rwkv_prep_kernel(u_ref, s_ref, mu_ref, mus_ref, wup_ref, w0_ref, aup_ref, a0_ref, kkr_ref,
                      r_o, lw_o, k_o, v_o, aa_o, b_o, prev_u, prev_s):
    @pl.when(pl.program_id(1) == 0)
    def _():
        prev_u[...] = jnp.zeros_like(prev_u)
        prev_s[...] = jnp.zeros_like(prev_s)

    def shift(x, prev, mu):
        first = lax.broadcasted_iota(jnp.int32, x.shape, 0) == 0
        x_prev = jnp.where(first, prev[...], pltpu.roll(x, 1, 0))
        prev[...] = x[x.shape[0] - 1:, :]
        return x + (x_prev - x) * mu

    u = shift(u_ref[0], prev_u, mu_ref[...])
    sm = shift(s_ref[0], prev_s, mus_ref[...])
    r, k, v = u[:, :GROUP_W], u[:, GROUP_W:2 * GROUP_W], u[:, 2 * GROUP_W:]
    w_pre = w0_ref[...] + _dot(jnp.tanh(sm), wup_ref[...], precision=HIGHEST)
    w_log = _log_sigmoid(w_pre) - 0.5
    a = 1.0 / (1.0 + jnp.exp(-(a0_ref[...] + _dot(sm, aup_ref[...], precision=HIGHEST))))
    kk = k * kkr_ref[0:1, :]
    kk_n = jnp.concatenate(
        [kk[:, i * LANES:(i + 1) * LANES]
         / jnp.maximum(jnp.sqrt(_seg_sum(kk[:, i * LANES:(i + 1) * LANES] ** 2)), 1e-12)
         for i in range(N_PAIRS)], axis=-1)
    r_o[0] = r
    lw_o[0] = -jnp.exp(w_log)
    k_o[0] = k * (1.0 + (a - 1.0) * kkr_ref[1:2, :])
    v_o[0] = v
    aa_o[0] = -kk_n
    b_o[0] = kk_n * a


def _rwkv_prep(proj, mu, w_up, w0, a_up, a0, kkr):
    bsz, seq, _ = proj.shape
    ts = min(PREP_ROWS, seq)
    mu = mu.astype(F32)
    mu_u = mu[:3 * GROUP_W].reshape(1, 3 * GROUP_W)
    mu_s = (jnp.zeros((1, LANES), F32)
            .at[0, SMALL_WLO:SMALL_WLO + RWKV_RANK].set(mu[3 * GROUP_W:3 * GROUP_W + RWKV_RANK])
            .at[0, SMALL_ALO:SMALL_ALO + RWKV_RANK].set(mu[3 * GROUP_W + RWKV_RANK:]))
    wup = jnp.zeros((LANES, GROUP_W), F32).at[SMALL_WLO:SMALL_WLO + RWKV_RANK].set(w_up.astype(F32))
    aup = jnp.zeros((LANES, GROUP_W), F32).at[SMALL_ALO:SMALL_ALO + RWKV_RANK].set(a_up.astype(F32))
    vec = lambda a: a.astype(F32).reshape(1, GROUP_W)
    full = lambda shape: pl.BlockSpec(shape, lambda b, i: (0,) * len(shape))
    out_spec = pl.BlockSpec((1, ts, GROUP_W), lambda b, i: (b, i, 0))
    out_sds = jax.ShapeDtypeStruct((bsz, seq, GROUP_W), F32)
    return pl.pallas_call(
        _rwkv_prep_kernel,
        grid=(bsz, seq // ts),
        in_specs=[pl.BlockSpec((1, ts, 3 * GROUP_W), lambda b, i: (b, i, COL_D * LANES // (3 * GROUP_W))),
                  pl.BlockSpec((1, ts, LANES), lambda b, i: (b, i, COL_SMALL)),
                  full((1, 3 * GROUP_W)), full((1, LANES)), full((LANES, GROUP_W)), full((1, GROUP_W)),
                  full((LANES, GROUP_W)), full((1, GROUP_W)), full((3, GROUP_W))],
        out_specs=[out_spec] * 6,
        out_shape=[out_sds] * 6,
        scratch_shapes=[pltpu.VMEM((1, 3 * GROUP_W), F32), pltpu.VMEM((1, LANES), F32)],
        compiler_params=_cparams(("arbitrary", "arbitrary")),
    )(proj, proj, mu_u, mu_s, wup, vec(w0), aup, vec(a0), kkr.astype(F32))


def _pieces(x, n):
    out = []
    for i in range(n):
        p = x.astype(BF16)
        out.append(p)
        if i + 1 < n:
            x = x - p.astype(F32)
    return tuple(out)


def _mm(a, b, nt=False):
    n = max(len(a), len(b))
    dot = _dot_nt if nt else _dot
    acc = None
    for i, ai in enumerate(a):
        for j, bj in enumerate(b):
            if i + j < n:
                term = dot(ai, bj)
                acc = term if acc is None else acc + term
    return acc


def _neumann_inverses(mats):
    n = mats[0].shape[0]
    eye = (lax.broadcasted_iota(jnp.int32, (n, n), 0) == lax.broadcasted_iota(jnp.int32, (n, n), 1)).astype(F32)
    ps = [eye + a for a in mats]
    xs = [_pieces(a, RWKV_PIECES) for a in mats]
    for _ in range(int(math.log2(n)) - 1):
        xs = [_pieces(_mm(x, x), RWKV_PIECES) for x in xs]
        ps = [p + _mm(_pieces(p, RWKV_PIECES), x) for p, x in zip(ps, xs)]
    return ps


def _rwkv_scan_kernel(r_ref, lw_ref, k_ref, v_ref, aa_ref, b_ref, g_ref, lng_ref, lnb_ref, rk_ref, tri_ref,
                      o_ref, state):
    c = r_ref.shape[1]
    n1, ng = RWKV_PIECES, RWKV_PIECES_G

    @pl.when(pl.program_id(1) == 0)
    def _():
        state[...] = jnp.zeros_like(state)

    row = lax.broadcasted_iota(jnp.int32, (c, c), 0)
    col = lax.broadcasted_iota(jnp.int32, (c, c), 1)
    strict, incl = row > col, row >= col
    lo = _lane_lo((c, LANES))
    lo2 = _lane_lo((2 * c, LANES))
    diag_blocks = ((lax.broadcasted_iota(jnp.int32, (LANES, LANES), 0) < HEAD_DIM)
                   == (lax.broadcasted_iota(jnp.int32, (LANES, LANES), 1) < HEAD_DIM))
    tri = (tri_ref[...],)
    units = [(bi, p) for bi in range(r_ref.shape[0]) for p in range(N_PAIRS)]
    lanes = [slice(p * LANES, (p + 1) * LANES) for _, p in units]
    nu = len(units)

    st = []
    for i, (bi, _) in enumerate(units):
        r, lw, k, v = (ref[bi, :, lanes[i]] for ref in (r_ref, lw_ref, k_ref, v_ref))
        aa, b = aa_ref[bi, :, lanes[i]], b_ref[bi, :, lanes[i]]
        cl = _mm(tri, _pieces(lw, 3))
        cl_end = cl[c - 1:c, :]
        cl_mid = cl[c // 2 - 1:c // 2, :]
        w_inv = jnp.exp(cl_mid - cl)
        w_rest = jnp.exp(cl_end - cl)
        qa = jnp.concatenate([aa * jnp.exp(cl - lw - cl_mid), r * jnp.exp(cl - cl_mid)], axis=0)
        bk = _pieces(jnp.concatenate([b * w_inv, k * w_inv], axis=0), ng)
        qa_s = jnp.concatenate([aa * jnp.exp(cl - lw), r * jnp.exp(cl)], axis=0)
        bk_end = _pieces(jnp.concatenate([b * w_rest, k * w_rest], axis=0), n1)
        st.append(dict(r=r, k=k, v=v, qa=qa, bk=bk, qa_s=qa_s, bk_end=bk_end, w_end=jnp.exp(cl_end),
                       v_p=_pieces(v, n1)))

    bases = [_mm(_pieces(st[i]["qa_s"], n1), _pieces(state[i], n1), nt=True) for i in range(nu)]
    a_ab, a_ak, a_r = [], [], []
    for i in range(nu):
        for h in range(2):
            g = _mm(_pieces(jnp.where(lo2 == (h == 0), st[i]["qa"], 0.0), ng), st[i]["bk"], nt=True)
            a_ab.append(jnp.where(strict, g[:c, :c], 0.0))
            a_ak.append(jnp.where(strict, g[:c, c:], 0.0))
            a_r.append(jnp.concatenate([jnp.where(incl, g[c:, :c], 0.0), jnp.where(incl, g[c:, c:], 0.0)], axis=1))
    rhs = [bases[j // 2][:c] + _mm(_pieces(a_ak[j], n1), st[j // 2]["v_p"]) for j in range(2 * nu)]

    t_inv = _neumann_inverses(a_ab)
    u_heads = [_mm(_pieces(t_inv[j], n1), _pieces(rhs[j], n1)) for j in range(2 * nu)]
    for i, (bi, _) in enumerate(units):
        r, k, v = st[i]["r"], st[i]["k"], st[i]["v"]
        u = jnp.where(lo, u_heads[2 * i], u_heads[2 * i + 1])
        uv = _pieces(jnp.concatenate([u, v], axis=0), n1)
        y = bases[i][c:] + jnp.where(lo, _mm(_pieces(a_r[2 * i], n1), uv), _mm(_pieces(a_r[2 * i + 1], n1), uv))
        upd = _mm(_pieces(jnp.concatenate([u.T, v.T], axis=1), n1), st[i]["bk_end"])
        state[i] = state[i] * st[i]["w_end"] + jnp.where(diag_blocks, upd, 0.0)

        mean = _seg_sum(y) * (1.0 / HEAD_DIM)
        yc = y - mean
        var = _seg_sum(yc * yc) * (1.0 / HEAD_DIM)
        out = yc * lax.rsqrt(var + RWKV_LN_EPS) * lng_ref[:, lanes[i]] + lnb_ref[:, lanes[i]]
        out = out + _seg_sum(r * k * rk_ref[:, lanes[i]]) * v
        o_ref[bi, :, lanes[i]] = (out * _silu(g_ref[bi, :, lanes[i]])).astype(o_ref.dtype)


def _rwkv_scan(proj, prepped, ln_gain, ln_bias, r_k):
    bsz, seq, _ = proj.shape
    c = min(RWKV_CHUNK, seq)
    tri = (lax.broadcasted_iota(jnp.int32, (c, c), 0) >= lax.broadcasted_iota(jnp.int32, (c, c), 1)).astype(BF16)
    nb = RWKV_ROWS if bsz % RWKV_ROWS == 0 else 1
    tok_spec = pl.BlockSpec((nb, c, GROUP_W), lambda b, i: (b, i, 0))
    vec_spec = pl.BlockSpec((1, GROUP_W), lambda b, i: (0, 0))
    vec = lambda a: a.astype(F32).reshape(1, GROUP_W)
    return pl.pallas_call(
        _rwkv_scan_kernel,
        grid=(bsz // nb, seq // c),
        in_specs=[tok_spec] * 6
        + [pl.BlockSpec((nb, c, GROUP_W), lambda b, i: (b, i, COL_DG * LANES // GROUP_W)),
           vec_spec, vec_spec, vec_spec, pl.BlockSpec((c, c), lambda b, i: (0, 0))],
        out_specs=tok_spec,
        out_shape=jax.ShapeDtypeStruct((bsz, seq, GROUP_W), BF16),
        scratch_shapes=[pltpu.VMEM((nb * N_PAIRS, LANES, LANES), F32)],
        compiler_params=_cparams(("arbitrary", "arbitrary")),
    )(*prepped, proj, vec(ln_gain), vec(ln_bias), vec(r_k), tri)


def _out_kernel(a_ref, b_ref, c_ref, d_ref, w_ref, x_ref, o_ref):
    acc = x_ref[...]
    for i, m_ref in enumerate((a_ref, b_ref, c_ref, d_ref)):
        acc = acc + _dot(m_ref[...], w_ref[i * GROUP_W:(i + 1) * GROUP_W, :])
    o_ref[...] = acc


def _out_proj(groups, w_out, x2d):
    m, d = x2d.shape
    tm = min(512, m)
    g_spec = pl.BlockSpec((tm, GROUP_W), lambda i: (i, 0))
    return pl.pallas_call(
        _out_kernel,
        grid=(m // tm,),
        in_specs=[g_spec] * 4 + [pl.BlockSpec(w_out.shape, lambda i: (0, 0)),
                                 pl.BlockSpec((tm, d), lambda i: (i, 0))],
        out_specs=pl.BlockSpec((tm, d), lambda i: (i, 0)),
        out_shape=jax.ShapeDtypeStruct((m, d), F32),
        compiler_params=_cparams(("arbitrary",)),
    )(*[g.reshape(m, GROUP_W) for g in groups], w_out, x2d)


def _t5_causal_bucket(dist):
    max_exact = NUM_BUCKETS // 2
    d = jnp.maximum(dist, 1).astype(F32)
    large = max_exact + (jnp.log(d / max_exact) / math.log(MAX_DISTANCE / max_exact)
                         * (NUM_BUCKETS - max_exact)).astype(jnp.int32)
    large = jnp.minimum(large, NUM_BUCKETS - 1)
    return jnp.where(dist < max_exact, dist, large)


def _bias_tiles(rel_bias, seq, t):
    assert t >= MAX_DISTANCE
    bias_by_dist = rel_bias.astype(F32)[_t5_causal_bucket(jnp.arange(seq))] * LOG2E
    m = np.arange(2 * t)
    tiles = []
    for delta in range(3):
        dist = np.where(m < t, delta * t - m, delta * t + 2 * t - m)
        v = bias_by_dist[np.clip(dist, 0, seq - 1)].T
        skew = jnp.tile(v, (1, t))[:, :t * (2 * t - 1)].reshape(-1, t, 2 * t - 1)
        tiles.append(skew[:, :, :t])
    return jnp.stack(tiles, axis=1)


def _pack_w_in(w):
    d = w.shape[0]
    att = w[:, :12 * GROUP_W]
    cf = w[:, 12 * GROUP_W:12 * GROUP_W + 8]
    off = 12 * GROUP_W + 8
    rkv = w[:, off:off + 3 * GROUP_W]
    w_lo = w[:, off + 3 * GROUP_W:off + 3 * GROUP_W + RWKV_RANK]
    a_lo = w[:, off + 3 * GROUP_W + RWKV_RANK:off + 3 * GROUP_W + 2 * RWKV_RANK]
    dg = w[:, off + 3 * GROUP_W + 2 * RWKV_RANK:]
    small = (jnp.zeros((d, LANES), w.dtype).at[:, :8].set(cf)
             .at[:, SMALL_WLO:SMALL_WLO + RWKV_RANK].set(w_lo)
             .at[:, SMALL_ALO:SMALL_ALO + RWKV_RANK].set(a_lo))
    return jnp.concatenate([att, rkv, dg, small, jnp.zeros((d, LANES), w.dtype)], axis=1).astype(BF16)


def kernel(x, norm_gain, w_in, w_out, rel_bias, qk_gain, diff_lambda, forget_bias, out_gain, rwkv_mu, rwkv_w_up,
           rwkv_w0, rwkv_a_up, rwkv_a0, rwkv_kkr, rwkv_ln_gain, rwkv_ln_bias):
    bsz, seq, d = x.shape
    tz = _bias_tiles(rel_bias, seq, min(ATT_TILE, seq))
    bias_max = jnp.max(jnp.abs(rel_bias.astype(F32))) * LOG2E
    h = x.reshape(bsz * seq, d)
    for l in range(DEPTH):
        og = out_gain[l].astype(F32)
        proj = _proj(h, norm_gain[l].astype(F32), _pack_w_in(w_in[l])).reshape(bsz, seq, N_PROJ)
        lam_init = 0.8 - 0.6 * math.exp(-0.3 * l)
        qg = qk_gain[l].astype(F32)
        bound_a = _score_bound_log2(qg[0], qg[1]) + bias_max
        bound_c = _score_bound_log2(qg[2], qg[3])
        att_a = functools.partial(_attn_a, proj, q_gain=qg[0], k_gain=qg[1], o_gain=og[:GROUP_W],
                                  dlam=diff_lambda[l].astype(F32), lam_init=lam_init)
        oa = lax.cond(bound_a <= MAX_SHIFT_LOG2, lambda: att_a(tz=tz - bound_a, bounded=True),
                      lambda: att_a(tz=tz, bounded=False))
        ob = _attn_b(proj, og[GROUP_W:2 * GROUP_W])
        att_c = functools.partial(_attn_c, proj, *_fox_cum(proj, forget_bias[l]), bound_c, qg[2], qg[3],
                                  og[2 * GROUP_W:3 * GROUP_W])
        oc = lax.cond(bound_c <= MAX_SHIFT_LOG2, lambda: att_c(bounded=True), lambda: att_c(bounded=False))
        prepped = _rwkv_prep(proj, rwkv_mu[l], rwkv_w_up[l], rwkv_w0[l], rwkv_a_up[l], rwkv_a0[l], rwkv_kkr[l])
        od = _rwkv_scan(proj, prepped, rwkv_ln_gain[l].astype(F32) * og[3 * GROUP_W:], rwkv_ln_bias[l],
                        rwkv_kkr[l, 2])
        h = _out_proj((oa, ob, oc, od), w_out[l].astype(BF16), h)
    return h.reshape(bsz, seq, d)
```

```python
import functools
import math

import numpy as np
import jax
import jax.numpy as jnp
from jax import lax
from jax.experimental import pallas as pl
from jax.experimental.pallas import tpu as pltpu

F32 = jnp.float32
BF16 = jnp.bfloat16
HIGHEST = lax.Precision.HIGHEST

D_MODEL = 1024
DEPTH = 2
GROUP_W = 512
HEAD_DIM = 64
H_A = 4
NUM_BUCKETS = 32
MAX_DISTANCE = 128
RWKV_RANK = 32
NORM_EPS = 1e-6
RWKV_LN_EPS = 64e-5
NEG_INF = -1e30

LANES = 128
N_PAIRS = GROUP_W // LANES
ATT_SCALE = HEAD_DIM ** -0.5
LOG2E = math.log2(math.e)

COL_A, COL_B, COL_C = 0, 16, 32
COL_D = 48
COL_DG = 60
COL_SMALL = 64
N_PROJ = (COL_SMALL + 2) * LANES
PROJ_COL_STEPS = 3
SMALL_WLO, SMALL_ALO = 32, 64

ATT_TILE = 512
ATT_PAIRS = 2
MAX_SHIFT_LOG2 = 56.0
STICK_TILE = 256
STICK_BLOCK = 256
STICK_DEAD_LOG2 = -152.0
PREP_ROWS = 512
RWKV_CHUNK = 128
RWKV_ROWS = 2
RWKV_PIECES = 1
RWKV_PIECES_G = 2
VMEM_LIMIT = 56 * 1024 * 1024


def _cparams(sem):
    return pltpu.CompilerParams(dimension_semantics=sem, vmem_limit_bytes=VMEM_LIMIT)


def _dot(a, b, **kw):
    return jnp.dot(a, b, preferred_element_type=F32, **kw)


def _dot_nt(a, b, **kw):
    return lax.dot_general(a, b, (((1,), (1,)), ((), ())), preferred_element_type=F32, **kw)


def _lane_lo(shape):
    return lax.broadcasted_iota(jnp.int32, shape, len(shape) - 1) < HEAD_DIM


def _seg_sum(x):
    lo = _lane_lo(x.shape)
    s_lo = jnp.sum(jnp.where(lo, x, 0.0), axis=-1, keepdims=True)
    s_hi = jnp.sum(jnp.where(lo, 0.0, x), axis=-1, keepdims=True)
    return jnp.where(lo, s_lo, s_hi)


def _seg_rms(x, gain):
    return x * lax.rsqrt(_seg_sum(x * x) * (1.0 / HEAD_DIM) + NORM_EPS) * gain


def _log_sigmoid(z):
    return jnp.minimum(z, 0.0) - jnp.log1p(jnp.exp(-jnp.abs(z)))


def _silu(g):
    return g / (1.0 + jnp.exp(-g))


def _proj_kernel(x_ref, g_ref, w_ref, o_ref, h_ref):
    @pl.when(pl.program_id(1) == 0)
    def _():
        x = x_ref[...]
        ms = jnp.mean(x * x, axis=-1, keepdims=True)
        h_ref[...] = (x * lax.rsqrt(ms + NORM_EPS) * g_ref[...]).astype(BF16)

    o_ref[...] = _dot(h_ref[...], w_ref[...])


def _proj(x2d, gain, w_packed):
    m, d = x2d.shape
    n = w_packed.shape[1]
    tm = min(1024, m)
    tn = n // PROJ_COL_STEPS
    return pl.pallas_call(
        _proj_kernel,
        grid=(m // tm, n // tn),
        in_specs=[pl.BlockSpec((tm, d), lambda i, j: (i, 0)),
                  pl.BlockSpec((1, d), lambda i, j: (0, 0)),
                  pl.BlockSpec((d, tn), lambda i, j: (0, j))],
        out_specs=pl.BlockSpec((tm, tn), lambda i, j: (i, j)),
        out_shape=jax.ShapeDtypeStruct((m, n), F32),
        scratch_shapes=[pltpu.VMEM((tm, d), BF16)],
        compiler_params=_cparams(("arbitrary", "arbitrary")),
    )(x2d, gain.reshape(1, d), w_packed)


def _for_row_blocks(seq, fn):
    rows_per = min(PREP_ROWS, seq)

    def body(c, carry):
        fn(pl.ds(pl.multiple_of(c * rows_per, rows_per), rows_per), rows_per)
        return carry

    lax.fori_loop(0, seq // rows_per, body, 0)


def _head_masked(q):
    lo = _lane_lo(q.shape)
    return jnp.where(lo, q, 0.0).astype(BF16), jnp.where(lo, 0.0, q).astype(BF16)


def _online_softmax(scores, carry, values):
    m_new = [jnp.maximum(m, jnp.max(s, axis=-1, keepdims=True)) for s, (m, _) in zip(scores, carry)]
    probs = [jnp.exp2(s - m).astype(BF16) for s, m in zip(scores, m_new)]
    return tuple((mn, jnp.exp2(m - mn) * acc + _dot(p, v))
                 for p, mn, (m, acc), v in zip(probs, m_new, carry, values))


def _softmax_init(t, n):
    return (jnp.full((t, 1), NEG_INF, F32), jnp.zeros((t, n), F32))


def _shifted_softmax(scores, carry, values):
    return tuple((m, acc + _dot(jnp.exp2(s).astype(BF16), v)) for s, (m, acc), v in zip(scores, carry, values))


def _causal_tiles(qi, tile, carry):
    carry = lax.fori_loop(0, qi // 2, lambda i, c: tile(2 * i + 1, tile(2 * i, c, False), False), carry)
    carry = lax.fori_loop(2 * (qi // 2), qi, lambda j, c: tile(j, c, False), carry)
    return tile(qi, carry, True)


def _attn_a_kernel(q_ref, k_ref, v_ref, g_ref, tz_ref, qg_ref, kg_ref, og_ref, dl_ref, o_ref, kbf, vbf,
                   *, lam_init, bounded):
    softmax = _shifted_softmax if bounded else _online_softmax
    qi = pl.program_id(2)
    t = q_ref.shape[1]
    heads = range(q_ref.shape[2] // LANES)
    sl = [slice(h * LANES, (h + 1) * LANES) for h in heads]
    vsl = [slice(2 * h * LANES, 2 * (h + 1) * LANES) for h in heads]

    @pl.when(qi == 0)
    def _():
        def stage(rows, n):
            for h in heads:
                kbf[rows, sl[h]] = _seg_rms(k_ref[0, rows, sl[h]], kg_ref[...]).astype(BF16)
                vbf[rows, vsl[h]] = jnp.concatenate([v_ref[0, rows, sl[h]].astype(BF16),
                                                     jnp.ones((n, LANES), BF16)], axis=1)

        _for_row_blocks(k_ref.shape[1], stage)

    qs = [_head_masked(_seg_rms(q_ref[0, :, sl[h]], qg_ref[...]) * (ATT_SCALE * LOG2E)) for h in heads]
    row = lax.broadcasted_iota(jnp.int32, (t, t), 0)
    col = lax.broadcasted_iota(jnp.int32, (t, t), 1)

    def tile(j, carry, diag):
        ks = pl.ds(pl.multiple_of(j * t, t), t)
        delta = jnp.minimum(qi - j, 2)
        scores = [_dot_nt(q, kbf[ks, sl[h]]) + tz_ref[h, delta] for h in heads for q in qs[h]]
        if diag:
            scores = [jnp.where(row >= col, s, NEG_INF) for s in scores]
        return softmax(scores, carry, [vbf[ks, vsl[h]] for h in heads for _ in range(2)])

    carry = _causal_tiles(qi, tile, (_softmax_init(t, 2 * LANES),) * (2 * len(heads)))

    dl = dl_ref[...]
    lam = (jnp.exp(jnp.sum(dl[0:1] * dl[1:2], axis=-1, keepdims=True))
           - jnp.exp(jnp.sum(dl[2:3] * dl[3:4], axis=-1, keepdims=True)) + lam_init)
    for h in heads:
        (_, acc0), (_, acc1) = carry[2 * h], carry[2 * h + 1]
        o = acc0[:, :LANES] / acc0[:, LANES:] - lam * (acc1[:, :LANES] / acc1[:, LANES:])
        y = o * lax.rsqrt(jnp.mean(o * o, axis=-1, keepdims=True) + NORM_EPS) * og_ref[0, :, sl[h]] * (1.0 - lam_init)
        o_ref[0, :, sl[h]] = (y * _silu(g_ref[0, :, sl[h]])).astype(o_ref.dtype)


def _attn_c_kernel(q_ref, k_ref, v_ref, g_ref, f_ref, fq_ref, bound_ref, qg_ref, kg_ref, og_ref, o_ref,
                   kbf, vbf0, vbf1, *, bounded):
    softmax = _shifted_softmax if bounded else _online_softmax
    qi = pl.program_id(2)
    t = q_ref.shape[1]
    pairs = range(q_ref.shape[2] // LANES)
    sl = [slice(p * LANES, (p + 1) * LANES) for p in pairs]

    @pl.when(qi == 0)
    def _():
        def stage(rows, n):
            for p in pairs:
                kbf[rows, sl[p]] = _seg_rms(k_ref[0, rows, sl[p]], kg_ref[...]).astype(BF16)
                v = v_ref[0, rows, sl[p]]
                lo = _lane_lo(v.shape)
                vbf0[rows, sl[p]] = jnp.where(lo, v, 1.0).astype(BF16)
                vbf1[rows, sl[p]] = jnp.where(lo, 1.0, v).astype(BF16)

        _for_row_blocks(k_ref.shape[1], stage)

    qs = [_head_masked(_seg_rms(q_ref[0, :, sl[p]], qg_ref[...]) * (ATT_SCALE * LOG2E)) for p in pairs]
    if bounded:
        fqs = [[fq_ref[0, p, :, h * HEAD_DIM:h * HEAD_DIM + 1] * LOG2E - bound_ref[:, :1] for h in range(2)]
               for p in pairs]
    else:
        fqs = [[0.0, 0.0] for _ in pairs]
    row = lax.broadcasted_iota(jnp.int32, (t, t), 0)
    col = lax.broadcasted_iota(jnp.int32, (t, t), 1)

    def tile(j, carry, diag):
        ks = pl.ds(pl.multiple_of(j * t, t), t)
        scores = [_dot_nt(qs[p][h], kbf[ks, sl[p]]) + (fqs[p][h] - f_ref[0, 0, 2 * p + h:2 * p + h + 1, ks] * LOG2E)
                  for p in pairs for h in range(2)]
        if diag:
            scores = [jnp.where(row >= col, s, NEG_INF) for s in scores]
        return softmax(scores, carry, [vb[ks, sl[p]] for p in pairs for vb in (vbf0, vbf1)])

    carry = _causal_tiles(qi, tile, (_softmax_init(t, LANES),) * (2 * len(pairs)))
    for p in pairs:
        (_, acc0), (_, acc1) = carry[2 * p], carry[2 * p + 1]
        lo = _lane_lo(acc0.shape)
        denom = jnp.where(lo, pltpu.roll(acc0, HEAD_DIM, 1), pltpu.roll(acc1, HEAD_DIM, 1))
        o = jnp.where(lo, acc0, acc1) / denom
        o_ref[0, :, sl[p]] = (_seg_rms(o, og_ref[0, :, sl[p]]) * _silu(g_ref[0, :, sl[p]])).astype(o_ref.dtype)


def _attn_b_kernel(q_ref, k_ref, v_ref, g_ref, u_ref, og_ref, o_ref, kbf, vbf):
    qi = pl.program_id(2)
    t = q_ref.shape[1]
    blk = u_ref.shape[1]
    pairs = range(q_ref.shape[2] // LANES)
    sl = [slice(p * LANES, (p + 1) * LANES) for p in pairs]

    @pl.when(qi == 0)
    def _():
        def stage(rows, n):
            kbf[rows, :] = k_ref[0, rows, :].astype(BF16)
            vbf[rows, :] = v_ref[0, rows, :].astype(BF16)

        _for_row_blocks(k_ref.shape[1], stage)

    qs = [q for p in pairs for q in _head_masked(q_ref[0, :, sl[p]] * (ATT_SCALE * LOG2E))]
    row = lax.broadcasted_iota(jnp.int32, (t, t), 0)
    col = lax.broadcasted_iota(jnp.int32, (t, t), 1)
    umat = u_ref[...]

    def tile(j, carry, diag):
        ks = pl.ds(pl.multiple_of(j * t, t), t)
        zs = [_dot_nt(q, kbf[ks, sl[i // 2]]) for i, q in enumerate(qs)]
        lbs = [jnp.minimum(z, 0.0) - jnp.log2(1.0 + jnp.exp2(-jnp.abs(z))) for z in zs]
        l1s = [lb - z for lb, z in zip(lbs, zs)]
        if diag:
            l1s = [jnp.where(row > col, l1, 0.0) for l1 in l1s]
        tails = [c for c, _ in carry]
        afters = [[] for _ in qs]
        for b in reversed(range(t // blk)):
            for i, l1 in enumerate(l1s):
                x = l1[:, b * blk:(b + 1) * blk]
                hi = x.astype(BF16)
                lo = (x - hi.astype(F32)).astype(BF16)
                afters[i].insert(0, _dot(jnp.concatenate([hi, lo], axis=1), umat) + tails[i])
                tails[i] = tails[i] + jnp.sum(x, axis=-1, keepdims=True)
        out = []
        for i, (lb, after, c_new, (_, acc)) in enumerate(zip(lbs, afters, tails, carry)):
            a = jnp.exp2(lb + (after[0] if len(after) == 1 else jnp.concatenate(after, axis=1)))
            if diag:
                a = jnp.where(row > col, a, 0.0)
            out.append((c_new, acc + _dot(a.astype(BF16), vbf[ks, sl[i // 2]])))
        return tuple(out)

    def alive(state):
        i, carry = state
        live = functools.reduce(jnp.maximum, [jnp.max(c) for c, _ in carry]) > STICK_DEAD_LOG2
        return jnp.logical_and(i < qi, live)

    init = ((jnp.zeros((t, 1), F32), jnp.zeros((t, LANES), F32)),) * len(qs)
    _, carry = lax.while_loop(alive, lambda st: (st[0] + 1, tile(qi - 1 - st[0], st[1], False)),
                              (jnp.int32(0), tile(qi, init, True)))
    for p in pairs:
        o = jnp.where(_lane_lo((t, LANES)), carry[2 * p][1], carry[2 * p + 1][1])
        o_ref[0, :, sl[p]] = (_seg_rms(o, og_ref[0, :, sl[p]]) * _silu(g_ref[0, :, sl[p]])).astype(o_ref.dtype)


def _attn_specs(seq, t, col, pp):
    w = pp * LANES
    q_spec = pl.BlockSpec((1, t, w), lambda b, p, i: (b, i, col // pp + p))
    k_spec = pl.BlockSpec((1, seq, w), lambda b, p, i: (b, 0, (col + N_PAIRS) // pp + p))
    v_spec = pl.BlockSpec((1, seq, w), lambda b, p, i: (b, 0, (col + 2 * N_PAIRS) // pp + p))
    g_spec = pl.BlockSpec((1, t, w), lambda b, p, i: (b, i, (col + 3 * N_PAIRS) // pp + p))
    return [q_spec, k_spec, v_spec, g_spec]


def _pair_spec(pp):
    return pl.BlockSpec((1, 1, pp * LANES), lambda b, p, i: (p, 0, 0))


def _const_spec(shape):
    return pl.BlockSpec(shape, lambda b, p, i: (0,) * len(shape))


def _attn_call(body, proj, col, extra_specs, extra_args, pp, tile=ATT_TILE, v_lanes=(LANES,)):
    bsz, seq, _ = proj.shape
    t = min(tile, seq)
    w = pp * LANES
    return pl.pallas_call(
        body,
        grid=(bsz, N_PAIRS // pp, seq // t),
        in_specs=_attn_specs(seq, t, col, pp) + extra_specs,
        out_specs=pl.BlockSpec((1, t, w), lambda b, p, i: (b, i, p)),
        out_shape=jax.ShapeDtypeStruct((bsz, seq, GROUP_W), BF16),
        scratch_shapes=[pltpu.VMEM((seq, w), BF16)] + [pltpu.VMEM((seq, pp * n), BF16) for n in v_lanes],
        compiler_params=_cparams(("arbitrary", "arbitrary", "arbitrary")),
    )(proj, proj, proj, proj, *extra_args)


def _pair_gain(g64):
    return jnp.tile(g64.astype(F32), 2).reshape(1, LANES)


def _score_bound_log2(q_gain, k_gain):
    return (HEAD_DIM * ATT_SCALE * LOG2E * 1.02) * jnp.max(jnp.abs(q_gain)) * jnp.max(jnp.abs(k_gain))


def _attn_a(proj, tz, q_gain, k_gain, o_gain, dlam, lam_init, bounded):
    t = tz.shape[-1]
    pp = ATT_PAIRS
    specs = [pl.BlockSpec((pp, 3, t, t), lambda b, p, i: (p, 0, 0, 0)),
             _const_spec((1, LANES)), _const_spec((1, LANES)), _pair_spec(pp), _const_spec((4, HEAD_DIM))]
    args = (tz, _pair_gain(q_gain), _pair_gain(k_gain), o_gain.reshape(N_PAIRS // pp, 1, pp * LANES), dlam)
    return _attn_call(functools.partial(_attn_a_kernel, lam_init=lam_init, bounded=bounded), proj, COL_A, specs,
                      args, pp=pp, v_lanes=(2 * LANES,))


def _attn_b(proj, o_gain):
    t = min(STICK_BLOCK, STICK_TILE, proj.shape[1])
    umat = (lax.broadcasted_iota(jnp.int32, (t, t), 0) > lax.broadcasted_iota(jnp.int32, (t, t), 1)).astype(BF16)
    umat = jnp.concatenate([umat, umat], axis=0)
    pp = ATT_PAIRS
    specs = [_const_spec((2 * t, t)), _pair_spec(pp)]
    return _attn_call(_attn_b_kernel, proj, COL_B, specs, (umat, o_gain.reshape(N_PAIRS // pp, 1, pp * LANES)),
                      pp=pp, tile=STICK_TILE)


def _attn_c(proj, cum_rows, cum_cols, bound, q_gain, k_gain, o_gain, bounded):
    seq = proj.shape[1]
    t = min(ATT_TILE, seq)
    pp = ATT_PAIRS
    specs = [pl.BlockSpec((1, 1, 2 * pp, seq), lambda b, p, i: (b, p, 0, 0)),
             pl.BlockSpec((1, pp, t, LANES), lambda b, p, i: (b, p, i, 0)),
             _const_spec((1, LANES)), _const_spec((1, LANES)), _const_spec((1, LANES)), _pair_spec(pp)]
    args = (cum_rows.reshape(cum_rows.shape[0], N_PAIRS // pp, 2 * pp, seq), cum_cols,
            jnp.full((1, LANES), bound, F32), _pair_gain(q_gain), _pair_gain(k_gain),
            o_gain.reshape(N_PAIRS // pp, 1, pp * LANES))
    return _attn_call(functools.partial(_attn_c_kernel, bounded=bounded), proj, COL_C, specs, args,
                      pp=pp, v_lanes=(LANES, LANES))


def _fox_cum_kernel(s_ref, fb_ref, tri_ref, rows_ref, cols_ref):
    seq = s_ref.shape[1]
    rows_per = tri_ref.shape[0]
    carry = jnp.zeros((1, LANES), F32)
    for c in range(seq // rows_per):
        rows = slice(c * rows_per, (c + 1) * rows_per)
        log_f = _log_sigmoid(s_ref[0, rows, :] + fb_ref[...])
        cum = _dot(tri_ref[...], log_f, precision=HIGHEST) + carry
        carry = cum[rows_per - 1:, :]
        rows_ref[0, :, rows] = cum.T[:8, :]
        lo = _lane_lo(cum.shape)
        for p in range(N_PAIRS):
            cols_ref[0, p, rows, :] = jnp.where(lo, cum[:, 2 * p:2 * p + 1], cum[:, 2 * p + 1:2 * p + 2])


def _fox_cum(proj, forget_bias):
    bsz, seq, _ = proj.shape
    rows_per = min(PREP_ROWS, seq)
    tri = (lax.broadcasted_iota(jnp.int32, (rows_per, rows_per), 0)
           >= lax.broadcasted_iota(jnp.int32, (rows_per, rows_per), 1)).astype(F32)
    fb = jnp.zeros((1, LANES), F32).at[0, :forget_bias.shape[0]].set(forget_bias.astype(F32))
    cum_rows, cum_cols = pl.pallas_call(
        _fox_cum_kernel,
        grid=(bsz,),
        in_specs=[pl.BlockSpec((1, seq, LANES), lambda b: (b, 0, COL_SMALL)),
                  pl.BlockSpec((1, LANES), lambda b: (0, 0)),
                  pl.BlockSpec((rows_per, rows_per), lambda b: (0, 0))],
        out_specs=[pl.BlockSpec((1, 8, seq), lambda b: (b, 0, 0)),
                   pl.BlockSpec((1, N_PAIRS, seq, LANES), lambda b: (b, 0, 0, 0))],
        out_shape=[jax.ShapeDtypeStruct((bsz, 8, seq), F32),
                   jax.ShapeDtypeStruct((bsz, N_PAIRS, seq, LANES), F32)],
        compiler_params=_cparams(("arbitrary",)),
    )(proj, fb, tri)
    return cum_rows.reshape(bsz, N_PAIRS, 2, seq), cum_cols


def _token_shift(x, prev_row, mu):
    first = lax.broadcasted_iota(jnp.int32, x.shape, 0) == 0
    x_prev = jnp.where(first, prev_row, pltpu.roll(x, 1, 0))
    return x + (x_prev - x) * mu


def _rwkv_prep_kernel(s_ref, mus_ref, wup_ref, w0_ref, aup_ref, a0_ref, lw_o, a_o, prev_s):
    @pl.when(pl.program_id(1) == 0)
    def _():
        prev_s[...] = jnp.zeros_like(prev_s)

    raw = s_ref[0]
    sm = _token_shift(raw, prev_s[...], mus_ref[...])
    prev_s[...] = raw[raw.shape[0] - 1:, :]
    w_pre = w0_ref[...] + _mm(_pieces(jnp.tanh(sm), 2), _pieces(wup_ref[...], 2))
    w_log = _log_sigmoid(w_pre) - 0.5
    lw_o[0] = -jnp.exp(w_log)
    a_o[0] = 1.0 / (1.0 + jnp.exp(-(a0_ref[...] + _mm(_pieces(sm, 2), _pieces(aup_ref[...], 2)))))


def _rwkv_prep(proj, mu, w_up, w0, a_up, a0):
    bsz, seq, _ = proj.shape
    ts = min(PREP_ROWS, seq)
    mu = mu.astype(F32)
    zeros = lambda n: jnp.zeros((n,), F32)
    mu_s = jnp.concatenate([zeros(SMALL_WLO), mu[3 * GROUP_W:3 * GROUP_W + RWKV_RANK],
                            mu[3 * GROUP_W + RWKV_RANK:], zeros(LANES - SMALL_ALO - RWKV_RANK)]).reshape(1, LANES)
    wup = jnp.pad(w_up.astype(F32), ((SMALL_WLO, LANES - SMALL_WLO - RWKV_RANK), (0, 0)))
    aup = jnp.pad(a_up.astype(F32), ((SMALL_ALO, LANES - SMALL_ALO - RWKV_RANK), (0, 0)))
    vec = lambda a: a.astype(F32).reshape(1, GROUP_W)
    full = lambda shape: pl.BlockSpec(shape, lambda b, i: (0,) * len(shape))
    out_spec = pl.BlockSpec((1, ts, GROUP_W), lambda b, i: (b, i, 0))
    out_sds = jax.ShapeDtypeStruct((bsz, seq, GROUP_W), F32)
    return pl.pallas_call(
        _rwkv_prep_kernel,
        grid=(bsz, seq // ts),
        in_specs=[pl.BlockSpec((1, ts, LANES), lambda b, i: (b, i, COL_SMALL)),
                  full((1, LANES)), full((LANES, GROUP_W)), full((1, GROUP_W)),
                  full((LANES, GROUP_W)), full((1, GROUP_W))],
        out_specs=[out_spec] * 2,
        out_shape=[out_sds] * 2,
        scratch_shapes=[pltpu.VMEM((1, LANES), F32)],
        compiler_params=_cparams(("arbitrary", "arbitrary")),
    )(proj, mu_s, wup, vec(w0), aup, vec(a0))


def _pieces(x, n):
    out = []
    for i in range(n):
        p = x.astype(BF16)
        out.append(p)
        if i + 1 < n:
            x = x - p.astype(F32)
    return tuple(out)


def _mm(a, b, nt=False):
    n = max(len(a), len(b))
    dot = _dot_nt if nt else _dot
    acc = None
    for i, ai in enumerate(a):
        for j, bj in enumerate(b):
            if i + j < n:
                term = dot(ai, bj)
                acc = term if acc is None else acc + term
    return acc


def _neumann_inverses(mats):
    n = mats[0].shape[0]
    eye = (lax.broadcasted_iota(jnp.int32, (n, n), 0) == lax.broadcasted_iota(jnp.int32, (n, n), 1)).astype(F32)
    ps = [eye + a for a in mats]
    xs = [_pieces(a, RWKV_PIECES) for a in mats]
    for _ in range(int(math.log2(n)) - 1):
        xs = [_pieces(_mm(x, x), RWKV_PIECES) for x in xs]
        ps = [p + _mm(_pieces(p, RWKV_PIECES), x) for p, x in zip(ps, xs)]
    return ps


def _rwkv_scan_kernel(u_ref, lw_ref, a_ref, g_ref, mu_ref, kkr_ref, lng_ref, lnb_ref, tri_ref, o_ref, state, prev_u):
    c = u_ref.shape[1]
    n1, ng = RWKV_PIECES, RWKV_PIECES_G

    @pl.when(pl.program_id(1) == 0)
    def _():
        state[...] = jnp.zeros_like(state)
        prev_u[...] = jnp.zeros_like(prev_u)

    row = lax.broadcasted_iota(jnp.int32, (c, c), 0)
    col = lax.broadcasted_iota(jnp.int32, (c, c), 1)
    strict, incl = row > col, row >= col
    lo = _lane_lo((c, LANES))
    lo2 = _lane_lo((2 * c, LANES))
    diag_blocks = ((lax.broadcasted_iota(jnp.int32, (LANES, LANES), 0) < HEAD_DIM)
                   == (lax.broadcasted_iota(jnp.int32, (LANES, LANES), 1) < HEAD_DIM))
    tri = (tri_ref[...],)
    units = [(bi, p) for bi in range(u_ref.shape[0]) for p in range(N_PAIRS)]
    lanes = [slice(p * LANES, (p + 1) * LANES) for _, p in units]
    nu = len(units)

    st = []
    for i, (bi, p) in enumerate(units):
        r, k, v = (_token_shift(u_ref[bi, :, sec], prev_u[bi, :, sec], mu_ref[:, sec])
                   for sec in (slice(j * GROUP_W + p * LANES, j * GROUP_W + (p + 1) * LANES) for j in range(3)))
        lw, a = lw_ref[bi, :, lanes[i]], a_ref[bi, :, lanes[i]]
        kk = k * kkr_ref[0:1, lanes[i]]
        kk = kk / jnp.maximum(jnp.sqrt(_seg_sum(kk * kk)), 1e-12)
        k = k * (1.0 + (a - 1.0) * kkr_ref[1:2, lanes[i]])
        aa, b = -kk, kk * a
        cl = _mm(tri, _pieces(lw, 3))
        cl_end = cl[c - 1:c, :]
        cl_mid = cl[c // 2 - 1:c // 2, :]
        w_inv = jnp.exp(cl_mid - cl)
        w_rest = jnp.exp(cl_end - cl)
        qa = jnp.concatenate([aa * jnp.exp(cl - lw - cl_mid), r * jnp.exp(cl - cl_mid)], axis=0)
        bk = _pieces(jnp.concatenate([b * w_inv, k * w_inv], axis=0), ng)
        qa_s = jnp.concatenate([aa * jnp.exp(cl - lw), r * jnp.exp(cl)], axis=0)
        bk_end = _pieces(jnp.concatenate([b * w_rest, k * w_rest], axis=0), n1)
        st.append(dict(r=r, k=k, v=v, qa=qa, bk=bk, qa_s=qa_s, bk_end=bk_end, w_end=jnp.exp(cl_end),
                       v_p=_pieces(v, n1)))

    bases = [_mm(_pieces(st[i]["qa_s"], n1), _pieces(state[i], n1), nt=True) for i in range(nu)]
    a_ab, a_ak, a_r = [], [], []
    for i in range(nu):
        for h in range(2):
            g = _mm(_pieces(jnp.where(lo2 == (h == 0), st[i]["qa"], 0.0), ng), st[i]["bk"], nt=True)
            a_ab.append(jnp.where(strict, g[:c, :c], 0.0))
            a_ak.append(jnp.where(strict, g[:c, c:], 0.0))
            a_r.append(jnp.concatenate([jnp.where(incl, g[c:, :c], 0.0), jnp.where(incl, g[c:, c:], 0.0)], axis=1))
    rhs = [bases[j // 2][:c] + _mm(_pieces(a_ak[j], n1), st[j // 2]["v_p"]) for j in range(2 * nu)]

    t_inv = _neumann_inverses(a_ab)
    u_heads = [_mm(_pieces(t_inv[j], n1), _pieces(rhs[j], n1)) for j in range(2 * nu)]
    for i, (bi, _) in enumerate(units):
        r, k, v = st[i]["r"], st[i]["k"], st[i]["v"]
        u = jnp.where(lo, u_heads[2 * i], u_heads[2 * i + 1])
        uv = _pieces(jnp.concatenate([u, v], axis=0), n1)
        y = bases[i][c:] + jnp.where(lo, _mm(_pieces(a_r[2 * i], n1), uv), _mm(_pieces(a_r[2 * i + 1], n1), uv))
        upd = _mm(_pieces(jnp.concatenate([u.T, v.T], axis=1), n1), st[i]["bk_end"])
        state[i] = state[i] * st[i]["w_end"] + jnp.where(diag_blocks, upd, 0.0)

        mean = _seg_sum(y) * (1.0 / HEAD_DIM)
        yc = y - mean
        var = _seg_sum(yc * yc) * (1.0 / HEAD_DIM)
        out = yc * lax.rsqrt(var + RWKV_LN_EPS) * lng_ref[:, lanes[i]] + lnb_ref[:, lanes[i]]
        out = out + _seg_sum(r * k * kkr_ref[2:3, lanes[i]]) * v
        o_ref[bi, :, lanes[i]] = (out * _silu(g_ref[bi, :, lanes[i]])).astype(o_ref.dtype)
    for bi in range(u_ref.shape[0]):
        prev_u[bi] = u_ref[bi, c - 1:, :]


def _rwkv_scan(proj, log_decay, iclr, mu, kkr, ln_gain, ln_bias):
    bsz, seq, _ = proj.shape
    c = min(RWKV_CHUNK, seq)
    tri = (lax.broadcasted_iota(jnp.int32, (c, c), 0) >= lax.broadcasted_iota(jnp.int32, (c, c), 1)).astype(BF16)
    nb = RWKV_ROWS if bsz % RWKV_ROWS == 0 else 1
    tok_spec = pl.BlockSpec((nb, c, GROUP_W), lambda b, i: (b, i, 0))
    full = lambda shape: pl.BlockSpec(shape, lambda b, i: (0,) * len(shape))
    vec = lambda a: a.astype(F32).reshape(1, GROUP_W)
    return pl.pallas_call(
        _rwkv_scan_kernel,
        grid=(bsz // nb, seq // c),
        in_specs=[pl.BlockSpec((nb, c, 3 * GROUP_W), lambda b, i: (b, i, COL_D * LANES // (3 * GROUP_W))),
                  tok_spec, tok_spec,
                  pl.BlockSpec((nb, c, GROUP_W), lambda b, i: (b, i, COL_DG * LANES // GROUP_W)),
                  full((1, 3 * GROUP_W)), full((3, GROUP_W)), full((1, GROUP_W)), full((1, GROUP_W)), full((c, c))],
        out_specs=tok_spec,
        out_shape=jax.ShapeDtypeStruct((bsz, seq, GROUP_W), BF16),
        scratch_shapes=[pltpu.VMEM((nb * N_PAIRS, LANES, LANES), F32), pltpu.VMEM((nb, 1, 3 * GROUP_W), F32)],
        compiler_params=_cparams(("arbitrary", "arbitrary")),
    )(proj, log_decay, iclr, proj, mu.astype(F32)[:3 * GROUP_W].reshape(1, 3 * GROUP_W), kkr.astype(F32),
      vec(ln_gain), vec(ln_bias), tri)


def _out_kernel(a_ref, b_ref, c_ref, d_ref, w_ref, x_ref, o_ref):
    acc = x_ref[...]
    for i, m_ref in enumerate((a_ref, b_ref, c_ref, d_ref)):
        acc = acc + _dot(m_ref[...], w_ref[i * GROUP_W:(i + 1) * GROUP_W, :])
    o_ref[...] = acc


def _out_proj(groups, w_out, x2d):
    m, d = x2d.shape
    tm = min(512, m)
    g_spec = pl.BlockSpec((tm, GROUP_W), lambda i: (i, 0))
    return pl.pallas_call(
        _out_kernel,
        grid=(m // tm,),
        in_specs=[g_spec] * 4 + [pl.BlockSpec(w_out.shape, lambda i: (0, 0)),
                                 pl.BlockSpec((tm, d), lambda i: (i, 0))],
        out_specs=pl.BlockSpec((tm, d), lambda i: (i, 0)),
        out_shape=jax.ShapeDtypeStruct((m, d), F32),
        compiler_params=_cparams(("arbitrary",)),
    )(*[g.reshape(m, GROUP_W) for g in groups], w_out, x2d)


def _t5_causal_bucket(dist):
    max_exact = NUM_BUCKETS // 2
    d = jnp.maximum(dist, 1).astype(F32)
    large = max_exact + (jnp.log(d / max_exact) / math.log(MAX_DISTANCE / max_exact)
                         * (NUM_BUCKETS - max_exact)).astype(jnp.int32)
    large = jnp.minimum(large, NUM_BUCKETS - 1)
    return jnp.where(dist < max_exact, dist, large)


def _bias_tiles(rel_bias, seq, t):
    assert t >= MAX_DISTANCE
    bias_by_dist = rel_bias.astype(F32)[_t5_causal_bucket(jnp.arange(seq))] * LOG2E
    m = np.arange(2 * t)
    tiles = []
    for delta in range(3):
        dist = np.where(m < t, delta * t - m, delta * t + 2 * t - m)
        v = bias_by_dist[np.clip(dist, 0, seq - 1)].T
        skew = jnp.tile(v, (1, t))[:, :t * (2 * t - 1)].reshape(-1, t, 2 * t - 1)
        tiles.append(skew[:, :, :t])
    return jnp.stack(tiles, axis=1)


def _pack_w_in(w):
    d = w.shape[0]
    att = w[:, :12 * GROUP_W]
    cf = w[:, 12 * GROUP_W:12 * GROUP_W + 8]
    off = 12 * GROUP_W + 8
    rkv = w[:, off:off + 3 * GROUP_W]
    w_lo = w[:, off + 3 * GROUP_W:off + 3 * GROUP_W + RWKV_RANK]
    a_lo = w[:, off + 3 * GROUP_W + RWKV_RANK:off + 3 * GROUP_W + 2 * RWKV_RANK]
    dg = w[:, off + 3 * GROUP_W + 2 * RWKV_RANK:]
    small = (jnp.zeros((d, LANES), w.dtype).at[:, :8].set(cf)
             .at[:, SMALL_WLO:SMALL_WLO + RWKV_RANK].set(w_lo)
             .at[:, SMALL_ALO:SMALL_ALO + RWKV_RANK].set(a_lo))
    return jnp.concatenate([att, rkv, dg, small, jnp.zeros((d, LANES), w.dtype)], axis=1).astype(BF16)


def kernel(x, norm_gain, w_in, w_out, rel_bias, qk_gain, diff_lambda, forget_bias, out_gain, rwkv_mu, rwkv_w_up,
           rwkv_w0, rwkv_a_up, rwkv_a0, rwkv_kkr, rwkv_ln_gain, rwkv_ln_bias):
    bsz, seq, d = x.shape
    tz = _bias_tiles(rel_bias, seq, min(ATT_TILE, seq))
    bias_max = jnp.max(jnp.abs(rel_bias.astype(F32))) * LOG2E
    h = x.reshape(bsz * seq, d)
    for l in range(DEPTH):
        og = out_gain[l].astype(F32)
        proj = _proj(h, norm_gain[l].astype(F32), _pack_w_in(w_in[l])).reshape(bsz, seq, N_PROJ)
        lam_init = 0.8 - 0.6 * math.exp(-0.3 * l)
        qg = qk_gain[l].astype(F32)
        bound_a = _score_bound_log2(qg[0], qg[1]) + bias_max
        bound_c = _score_bound_log2(qg[2], qg[3])
        att_a = functools.partial(_attn_a, proj, q_gain=qg[0], k_gain=qg[1], o_gain=og[:GROUP_W],
                                  dlam=diff_lambda[l].astype(F32), lam_init=lam_init)
        oa = lax.cond(bound_a <= MAX_SHIFT_LOG2, lambda: att_a(tz=tz - bound_a, bounded=True),
                      lambda: att_a(tz=tz, bounded=False))
        ob = _attn_b(proj, og[GROUP_W:2 * GROUP_W])
        att_c = functools.partial(_attn_c, proj, *_fox_cum(proj, forget_bias[l]), bound_c, qg[2], qg[3],
                                  og[2 * GROUP_W:3 * GROUP_W])
        oc = lax.cond(bound_c <= MAX_SHIFT_LOG2, lambda: att_c(bounded=True), lambda: att_c(bounded=False))
        log_decay, iclr = _rwkv_prep(proj, rwkv_mu[l], rwkv_w_up[l], rwkv_w0[l], rwkv_a_up[l], rwkv_a0[l])
        od = _rwkv_scan(proj, log_decay, iclr, rwkv_mu[l], rwkv_kkr[l],
                        rwkv_ln_gain[l].astype(F32) * og[3 * GROUP_W:], rwkv_ln_bias[l])
        h = _out_proj((oa, ob, oc, od), w_out[l].astype(BF16), h)
    return h.reshape(bsz, seq, d)
```

```python
import functools
import math

import numpy as np
import jax
import jax.numpy as jnp
from jax import lax
from jax.experimental import pallas as pl
from jax.experimental.pallas import tpu as pltpu

F32 = jnp.float32
BF16 = jnp.bfloat16
HIGHEST = lax.Precision.HIGHEST

D_MODEL = 1024
DEPTH = 2
GROUP_W = 512
HEAD_DIM = 64
H_A = 4
NUM_BUCKETS = 32
MAX_DISTANCE = 128
RWKV_RANK = 32
NORM_EPS = 1e-6
RWKV_LN_EPS = 64e-5
NEG_INF = -1e30

LANES = 128
N_PAIRS = GROUP_W // LANES
ATT_SCALE = HEAD_DIM ** -0.5
LOG2E = math.log2(math.e)

COL_A, COL_B, COL_C = 0, 16, 32
COL_D = 48
COL_DG = 60
COL_SMALL = 64
N_PROJ = (COL_SMALL + 2) * LANES
PROJ_COL_STEPS = 3
SMALL_WLO, SMALL_ALO = 32, 64

ATT_TILE = 512
ATT_PAIRS = 2
MAX_SHIFT_LOG2 = 56.0
STICK_TILE = 256
STICK_BLOCK = 256
STICK_DEAD_LOG2 = -152.0
PREP_ROWS = 512
RWKV_CHUNK = 128
RWKV_ROWS = 2
RWKV_PIECES = 1
RWKV_PIECES_G = 2
VMEM_LIMIT = 56 * 1024 * 1024


def _cparams(sem):
    return pltpu.CompilerParams(dimension_semantics=sem, vmem_limit_bytes=VMEM_LIMIT)


def _dot(a, b, **kw):
    return jnp.dot(a, b, preferred_element_type=F32, **kw)


def _dot_nt(a, b, **kw):
    return lax.dot_general(a, b, (((1,), (1,)), ((), ())), preferred_element_type=F32, **kw)


def _lane_lo(shape):
    return lax.broadcasted_iota(jnp.int32, shape, len(shape) - 1) < HEAD_DIM


def _seg_sum(x):
    lo = _lane_lo(x.shape)
    s_lo = jnp.sum(jnp.where(lo, x, 0.0), axis=-1, keepdims=True)
    s_hi = jnp.sum(jnp.where(lo, 0.0, x), axis=-1, keepdims=True)
    return jnp.where(lo, s_lo, s_hi)


def _group_mean_mxu(x, group):
    n = x.shape[-1]
    same = (lax.broadcasted_iota(jnp.int32, (n, n), 0) // group) == (lax.broadcasted_iota(jnp.int32, (n, n), 1) // group)
    return _mm(_pieces(x, 3), (same.astype(BF16),)) * (1.0 / group)


def _seg_rms(x, gain, mxu=False):
    ms = _group_mean_mxu(x * x, HEAD_DIM) if mxu else _seg_sum(x * x) * (1.0 / HEAD_DIM)
    return x * lax.rsqrt(ms + NORM_EPS) * gain


def _log_sigmoid(z):
    return jnp.minimum(z, 0.0) - jnp.log1p(jnp.exp(-jnp.abs(z)))


def _silu(g):
    return g / (1.0 + jnp.exp(-g))


def _proj_kernel(x_ref, g_ref, w_ref, o_ref, h_ref):
    @pl.when(pl.program_id(1) == 0)
    def _():
        x = x_ref[...]
        ms = jnp.mean(x * x, axis=-1, keepdims=True)
        h_ref[...] = (x * lax.rsqrt(ms + NORM_EPS) * g_ref[...]).astype(BF16)

    o_ref[...] = _dot(h_ref[...], w_ref[...])


def _proj(x2d, gain, w_packed):
    m, d = x2d.shape
    n = w_packed.shape[1]
    tm = min(1024, m)
    tn = n // PROJ_COL_STEPS
    return pl.pallas_call(
        _proj_kernel,
        grid=(m // tm, n // tn),
        in_specs=[pl.BlockSpec((tm, d), lambda i, j: (i, 0)),
                  pl.BlockSpec((1, d), lambda i, j: (0, 0)),
                  pl.BlockSpec((d, tn), lambda i, j: (0, j))],
        out_specs=pl.BlockSpec((tm, tn), lambda i, j: (i, j)),
        out_shape=jax.ShapeDtypeStruct((m, n), F32),
        scratch_shapes=[pltpu.VMEM((tm, d), BF16)],
        compiler_params=_cparams(("arbitrary", "arbitrary")),
    )(x2d, gain.reshape(1, d), w_packed)


def _for_row_blocks(seq, fn):
    rows_per = min(PREP_ROWS, seq)

    def body(c, carry):
        fn(pl.ds(pl.multiple_of(c * rows_per, rows_per), rows_per), rows_per)
        return carry

    lax.fori_loop(0, seq // rows_per, body, 0)


def _head_masked(q):
    lo = _lane_lo(q.shape)
    return jnp.where(lo, q, 0.0).astype(BF16), jnp.where(lo, 0.0, q).astype(BF16)


def _online_softmax(scores, m_ref, acc_ref, values):
    m_old = [m_ref[i] for i in range(len(scores))]
    m_new = [jnp.maximum(m, jnp.max(s, axis=-1, keepdims=True)) for s, m in zip(scores, m_old)]
    probs = [jnp.exp2(s - m).astype(BF16) for s, m in zip(scores, m_new)]
    for i, (p, mo, mn, v) in enumerate(zip(probs, m_old, m_new, values)):
        acc_ref[i] = jnp.exp2(mo - mn) * acc_ref[i] + _dot(p, v)
        m_ref[i] = mn


def _shifted_softmax(scores, m_ref, acc_ref, values):
    for i, (s, v) in enumerate(zip(scores, values)):
        acc_ref[i] = acc_ref[i] + _dot(jnp.exp2(s).astype(BF16), v)


def _causal_tiles(qi, tile):
    def two(i, c):
        tile(2 * i, False)
        tile(2 * i + 1, False)
        return c

    def one(j, c):
        tile(j, False)
        return c

    lax.fori_loop(0, qi // 2, two, 0)
    lax.fori_loop(2 * (qi // 2), qi, one, 0)
    tile(qi, True)


def _attn_a_kernel(q_ref, k_ref, v_ref, g_ref, tz_ref, qg_ref, kg_ref, og_ref, dl_ref, o_ref, kbf, vbf, m_ref,
                   acc_ref, *, lam_init, bounded):
    softmax = _shifted_softmax if bounded else _online_softmax
    qi = pl.program_id(2)
    t = q_ref.shape[1]
    heads = range(q_ref.shape[2] // LANES)
    sl = [slice(h * LANES, (h + 1) * LANES) for h in heads]
    vsl = [slice(2 * h * LANES, 2 * (h + 1) * LANES) for h in heads]

    @pl.when(qi == 0)
    def _():
        def stage(rows, n):
            for h in heads:
                kbf[rows, sl[h]] = _seg_rms(k_ref[0, rows, sl[h]], kg_ref[...]).astype(BF16)
                vbf[rows, vsl[h]] = jnp.concatenate([v_ref[0, rows, sl[h]].astype(BF16),
                                                     jnp.ones((n, LANES), BF16)], axis=1)

        _for_row_blocks(k_ref.shape[1], stage)

    qs = [_head_masked(_seg_rms(q_ref[0, :, sl[h]], qg_ref[...]) * (ATT_SCALE * LOG2E)) for h in heads]
    row = lax.broadcasted_iota(jnp.int32, (t, t), 0)
    col = lax.broadcasted_iota(jnp.int32, (t, t), 1)

    def tile(j, diag):
        ks = pl.ds(pl.multiple_of(j * t, t), t)
        delta = jnp.minimum(qi - j, 2)
        scores = [_dot_nt(q, kbf[ks, sl[h]]) + tz_ref[h, delta] for h in heads for q in qs[h]]
        if diag:
            scores = [jnp.where(row >= col, s, NEG_INF) for s in scores]
        softmax(scores, m_ref, acc_ref, [vbf[ks, vsl[h]] for h in heads for _ in range(2)])

    m_ref[...] = jnp.full(m_ref.shape, NEG_INF, F32)
    acc_ref[...] = jnp.zeros(acc_ref.shape, F32)
    _causal_tiles(qi, tile)

    dl = dl_ref[...]
    lam = (jnp.exp(jnp.sum(dl[0:1] * dl[1:2], axis=-1, keepdims=True))
           - jnp.exp(jnp.sum(dl[2:3] * dl[3:4], axis=-1, keepdims=True)) + lam_init)
    for h in heads:
        acc0, acc1 = acc_ref[2 * h], acc_ref[2 * h + 1]
        o = acc0[:, :LANES] / acc0[:, LANES:] - lam * (acc1[:, :LANES] / acc1[:, LANES:])
        y = o * lax.rsqrt(jnp.mean(o * o, axis=-1, keepdims=True) + NORM_EPS) * og_ref[0, :, sl[h]] * (1.0 - lam_init)
        o_ref[0, :, sl[h]] = (y * _silu(g_ref[0, :, sl[h]])).astype(o_ref.dtype)


def _attn_c_kernel(q_ref, k_ref, v_ref, g_ref, f_ref, fq_ref, bound_ref, qg_ref, kg_ref, og_ref, o_ref,
                   kbf, vbf0, vbf1, m_ref, acc_ref, *, bounded):
    softmax = _shifted_softmax if bounded else _online_softmax
    qi = pl.program_id(2)
    t = q_ref.shape[1]
    pairs = range(q_ref.shape[2] // LANES)
    sl = [slice(p * LANES, (p + 1) * LANES) for p in pairs]

    @pl.when(qi == 0)
    def _():
        def stage(rows, n):
            for p in pairs:
                kbf[rows, sl[p]] = _seg_rms(k_ref[0, rows, sl[p]], kg_ref[...]).astype(BF16)
                v = v_ref[0, rows, sl[p]]
                lo = _lane_lo(v.shape)
                vbf0[rows, sl[p]] = jnp.where(lo, v, 1.0).astype(BF16)
                vbf1[rows, sl[p]] = jnp.where(lo, 1.0, v).astype(BF16)

        _for_row_blocks(k_ref.shape[1], stage)

    qs = [_head_masked(_seg_rms(q_ref[0, :, sl[p]], qg_ref[...]) * (ATT_SCALE * LOG2E)) for p in pairs]
    if bounded:
        fqs = [[fq_ref[0, p, :, h * HEAD_DIM:h * HEAD_DIM + 1] * LOG2E - bound_ref[:, :1] for h in range(2)]
               for p in pairs]
    else:
        fqs = [[0.0, 0.0] for _ in pairs]
    row = lax.broadcasted_iota(jnp.int32, (t, t), 0)
    col = lax.broadcasted_iota(jnp.int32, (t, t), 1)

    def tile(j, diag):
        ks = pl.ds(pl.multiple_of(j * t, t), t)
        scores = [_dot_nt(qs[p][h], kbf[ks, sl[p]]) + (fqs[p][h] - f_ref[0, 0, 2 * p + h:2 * p + h + 1, ks] * LOG2E)
                  for p in pairs for h in range(2)]
        if diag:
            scores = [jnp.where(row >= col, s, NEG_INF) for s in scores]
        softmax(scores, m_ref, acc_ref, [vb[ks, sl[p]] for p in pairs for vb in (vbf0, vbf1)])

    m_ref[...] = jnp.full(m_ref.shape, NEG_INF, F32)
    acc_ref[...] = jnp.zeros(acc_ref.shape, F32)
    _causal_tiles(qi, tile)
    for p in pairs:
        acc0, acc1 = acc_ref[2 * p], acc_ref[2 * p + 1]
        lo = _lane_lo(acc0.shape)
        denom = jnp.where(lo, pltpu.roll(acc0, HEAD_DIM, 1), pltpu.roll(acc1, HEAD_DIM, 1))
        o = jnp.where(lo, acc0, acc1) / denom
        o_ref[0, :, sl[p]] = (_seg_rms(o, og_ref[0, :, sl[p]], mxu=True) * _silu(g_ref[0, :, sl[p]])).astype(o_ref.dtype)


def _attn_b_kernel(q_ref, k_ref, v_ref, g_ref, u_ref, og_ref, o_ref, kbf, vbf, c_ref, acc_ref):
    qi = pl.program_id(2)
    t = q_ref.shape[1]
    blk = u_ref.shape[1]
    pairs = range(q_ref.shape[2] // LANES)
    sl = [slice(p * LANES, (p + 1) * LANES) for p in pairs]

    @pl.when(qi == 0)
    def _():
        def stage(rows, n):
            kbf[rows, :] = k_ref[0, rows, :].astype(BF16)
            vbf[rows, :] = v_ref[0, rows, :].astype(BF16)

        _for_row_blocks(k_ref.shape[1], stage)

    qs = [q for p in pairs for q in _head_masked(q_ref[0, :, sl[p]] * (ATT_SCALE * LOG2E))]
    row = lax.broadcasted_iota(jnp.int32, (t, t), 0)
    col = lax.broadcasted_iota(jnp.int32, (t, t), 1)
    umat = u_ref[...]

    def tile(j, diag):
        ks = pl.ds(pl.multiple_of(j * t, t), t)
        zs = [_dot_nt(q, kbf[ks, sl[i // 2]]) for i, q in enumerate(qs)]
        lbs = [jnp.minimum(z, 0.0) - jnp.log2(1.0 + jnp.exp2(-jnp.abs(z))) for z in zs]
        l1s = [lb - z for lb, z in zip(lbs, zs)]
        if diag:
            l1s = [jnp.where(row > col, l1, 0.0) for l1 in l1s]
        tails = [c_ref[i] for i in range(len(qs))]
        afters = [[] for _ in qs]
        for b in reversed(range(t // blk)):
            for i, l1 in enumerate(l1s):
                x = l1[:, b * blk:(b + 1) * blk]
                hi = x.astype(BF16)
                lo = (x - hi.astype(F32)).astype(BF16)
                afters[i].insert(0, _dot(jnp.concatenate([hi, lo], axis=1), umat) + tails[i])
                tails[i] = tails[i] + jnp.sum(x, axis=-1, keepdims=True)
        for i, (lb, after, c_new) in enumerate(zip(lbs, afters, tails)):
            a = jnp.exp2(lb + (after[0] if len(after) == 1 else jnp.concatenate(after, axis=1)))
            if diag:
                a = jnp.where(row > col, a, 0.0)
            acc_ref[i] = acc_ref[i] + _dot(a.astype(BF16), vbf[ks, sl[i // 2]])
            c_ref[i] = c_new
        return functools.reduce(jnp.maximum, [jnp.max(c) for c in tails])

    def alive(state):
        return jnp.logical_and(state[0] < qi, state[1] > STICK_DEAD_LOG2)

    c_ref[...] = jnp.zeros(c_ref.shape, F32)
    acc_ref[...] = jnp.zeros(acc_ref.shape, F32)
    lax.while_loop(alive, lambda st: (st[0] + 1, tile(qi - 1 - st[0], False)), (jnp.int32(0), tile(qi, True)))
    for p in pairs:
        o = jnp.where(_lane_lo((t, LANES)), acc_ref[2 * p], acc_ref[2 * p + 1])
        o_ref[0, :, sl[p]] = (_seg_rms(o, og_ref[0, :, sl[p]]) * _silu(g_ref[0, :, sl[p]])).astype(o_ref.dtype)


def _attn_specs(seq, t, col, pp):
    w = pp * LANES
    q_spec = pl.BlockSpec((1, t, w), lambda b, p, i: (b, i, col // pp + p))
    k_spec = pl.BlockSpec((1, seq, w), lambda b, p, i: (b, 0, (col + N_PAIRS) // pp + p))
    v_spec = pl.BlockSpec((1, seq, w), lambda b, p, i: (b, 0, (col + 2 * N_PAIRS) // pp + p))
    g_spec = pl.BlockSpec((1, t, w), lambda b, p, i: (b, i, (col + 3 * N_PAIRS) // pp + p))
    return [q_spec, k_spec, v_spec, g_spec]


def _pair_spec(pp):
    return pl.BlockSpec((1, 1, pp * LANES), lambda b, p, i: (p, 0, 0))


def _const_spec(shape):
    return pl.BlockSpec(shape, lambda b, p, i: (0,) * len(shape))


def _attn_call(body, proj, col, extra_specs, extra_args, pp, tile=ATT_TILE, v_lanes=(LANES,), acc_lanes=None):
    bsz, seq, _ = proj.shape
    t = min(tile, seq)
    w = pp * LANES
    return pl.pallas_call(
        body,
        grid=(bsz, N_PAIRS // pp, seq // t),
        in_specs=_attn_specs(seq, t, col, pp) + extra_specs,
        out_specs=pl.BlockSpec((1, t, w), lambda b, p, i: (b, i, p)),
        out_shape=jax.ShapeDtypeStruct((bsz, seq, GROUP_W), BF16),
        scratch_shapes=[pltpu.VMEM((seq, w), BF16)] + [pltpu.VMEM((seq, pp * n), BF16) for n in v_lanes]
        + ([] if acc_lanes is None else [pltpu.VMEM((2 * pp, t, 1), F32), pltpu.VMEM((2 * pp, t, acc_lanes), F32)]),
        compiler_params=_cparams(("arbitrary", "arbitrary", "arbitrary")),
    )(proj, proj, proj, proj, *extra_args)


def _pair_gain(g64):
    return jnp.tile(g64.astype(F32), 2).reshape(1, LANES)


def _score_bound_log2(q_gain, k_gain):
    return (HEAD_DIM * ATT_SCALE * LOG2E * 1.02) * jnp.max(jnp.abs(q_gain)) * jnp.max(jnp.abs(k_gain))


def _attn_a(proj, tz, q_gain, k_gain, o_gain, dlam, lam_init, bounded):
    t = tz.shape[-1]
    pp = ATT_PAIRS
    specs = [pl.BlockSpec((pp, 3, t, t), lambda b, p, i: (p, 0, 0, 0)),
             _const_spec((1, LANES)), _const_spec((1, LANES)), _pair_spec(pp), _const_spec((4, HEAD_DIM))]
    args = (tz, _pair_gain(q_gain), _pair_gain(k_gain), o_gain.reshape(N_PAIRS // pp, 1, pp * LANES), dlam)
    return _attn_call(functools.partial(_attn_a_kernel, lam_init=lam_init, bounded=bounded), proj, COL_A, specs,
                      args, pp=pp, v_lanes=(2 * LANES,), acc_lanes=2 * LANES)


def _attn_b(proj, o_gain):
    t = min(STICK_BLOCK, STICK_TILE, proj.shape[1])
    umat = (lax.broadcasted_iota(jnp.int32, (t, t), 0) > lax.broadcasted_iota(jnp.int32, (t, t), 1)).astype(BF16)
    umat = jnp.concatenate([umat, umat], axis=0)
    pp = ATT_PAIRS
    specs = [_const_spec((2 * t, t)), _pair_spec(pp)]
    return _attn_call(_attn_b_kernel, proj, COL_B, specs, (umat, o_gain.reshape(N_PAIRS // pp, 1, pp * LANES)),
                      pp=pp, tile=STICK_TILE, acc_lanes=LANES)


def _attn_c(proj, cum_rows, cum_cols, bound, q_gain, k_gain, o_gain, bounded):
    seq = proj.shape[1]
    t = min(ATT_TILE, seq)
    pp = ATT_PAIRS
    specs = [pl.BlockSpec((1, 1, 2 * pp, seq), lambda b, p, i: (b, p, 0, 0)),
             pl.BlockSpec((1, pp, t, LANES), lambda b, p, i: (b, p, i, 0)),
             _const_spec((1, LANES)), _const_spec((1, LANES)), _const_spec((1, LANES)), _pair_spec(pp)]
    args = (cum_rows.reshape(cum_rows.shape[0], N_PAIRS // pp, 2 * pp, seq), cum_cols,
            jnp.full((1, LANES), bound, F32), _pair_gain(q_gain), _pair_gain(k_gain),
            o_gain.reshape(N_PAIRS // pp, 1, pp * LANES))
    return _attn_call(functools.partial(_attn_c_kernel, bounded=bounded), proj, COL_C, specs, args,
                      pp=pp, v_lanes=(LANES, LANES), acc_lanes=LANES)


def _fox_cum_kernel(s_ref, fb_ref, tri_ref, rows_ref, cols_ref):
    seq = s_ref.shape[1]
    rows_per = tri_ref.shape[0]
    carry = jnp.zeros((1, LANES), F32)
    for c in range(seq // rows_per):
        rows = slice(c * rows_per, (c + 1) * rows_per)
        log_f = _log_sigmoid(s_ref[0, rows, :] + fb_ref[...])
        cum = _dot(tri_ref[...], log_f, precision=HIGHEST) + carry
        carry = cum[rows_per - 1:, :]
        rows_ref[0, :, rows] = cum.T[:8, :]
        lo = _lane_lo(cum.shape)
        for p in range(N_PAIRS):
            cols_ref[0, p, rows, :] = jnp.where(lo, cum[:, 2 * p:2 * p + 1], cum[:, 2 * p + 1:2 * p + 2])


def _fox_cum(proj, forget_bias):
    bsz, seq, _ = proj.shape
    rows_per = min(PREP_ROWS, seq)
    tri = (lax.broadcasted_iota(jnp.int32, (rows_per, rows_per), 0)
           >= lax.broadcasted_iota(jnp.int32, (rows_per, rows_per), 1)).astype(F32)
    fb = jnp.zeros((1, LANES), F32).at[0, :forget_bias.shape[0]].set(forget_bias.astype(F32))
    cum_rows, cum_cols = pl.pallas_call(
        _fox_cum_kernel,
        grid=(bsz,),
        in_specs=[pl.BlockSpec((1, seq, LANES), lambda b: (b, 0, COL_SMALL)),
                  pl.BlockSpec((1, LANES), lambda b: (0, 0)),
                  pl.BlockSpec((rows_per, rows_per), lambda b: (0, 0))],
        out_specs=[pl.BlockSpec((1, 8, seq), lambda b: (b, 0, 0)),
                   pl.BlockSpec((1, N_PAIRS, seq, LANES), lambda b: (b, 0, 0, 0))],
        out_shape=[jax.ShapeDtypeStruct((bsz, 8, seq), F32),
                   jax.ShapeDtypeStruct((bsz, N_PAIRS, seq, LANES), F32)],
        compiler_params=_cparams(("arbitrary",)),
    )(proj, fb, tri)
    return cum_rows.reshape(bsz, N_PAIRS, 2, seq), cum_cols


def _token_shift(x, prev_row, mu):
    first = lax.broadcasted_iota(jnp.int32, x.shape, 0) == 0
    x_prev = jnp.where(first, prev_row, pltpu.roll(x, 1, 0))
    return x + (x_prev - x) * mu


def _rwkv_prep_kernel(s_ref, mus_ref, wup_ref, w0_ref, aup_ref, a0_ref, lw_o, a_o, prev_s):
    @pl.when(pl.program_id(1) == 0)
    def _():
        prev_s[...] = jnp.zeros_like(prev_s)

    raw = s_ref[0]
    sm = _token_shift(raw, prev_s[...], mus_ref[...])
    prev_s[...] = raw[raw.shape[0] - 1:, :]
    w_pre = w0_ref[...] + _mm(_pieces(jnp.tanh(sm), 2), _pieces(wup_ref[...], 2))
    w_log = _log_sigmoid(w_pre) - 0.5
    lw_o[0] = -jnp.exp(w_log)
    a_o[0] = 1.0 / (1.0 + jnp.exp(-(a0_ref[...] + _mm(_pieces(sm, 2), _pieces(aup_ref[...], 2)))))


def _rwkv_prep(proj, mu, w_up, w0, a_up, a0):
    bsz, seq, _ = proj.shape
    ts = min(PREP_ROWS, seq)
    mu = mu.astype(F32)
    zeros = lambda n: jnp.zeros((n,), F32)
    mu_s = jnp.concatenate([zeros(SMALL_WLO), mu[3 * GROUP_W:3 * GROUP_W + RWKV_RANK],
                            mu[3 * GROUP_W + RWKV_RANK:], zeros(LANES - SMALL_ALO - RWKV_RANK)]).reshape(1, LANES)
    wup = jnp.pad(w_up.astype(F32), ((SMALL_WLO, LANES - SMALL_WLO - RWKV_RANK), (0, 0)))
    aup = jnp.pad(a_up.astype(F32), ((SMALL_ALO, LANES - SMALL_ALO - RWKV_RANK), (0, 0)))
    vec = lambda a: a.astype(F32).reshape(1, GROUP_W)
    full = lambda shape: pl.BlockSpec(shape, lambda b, i: (0,) * len(shape))
    out_spec = pl.BlockSpec((1, ts, GROUP_W), lambda b, i: (b, i, 0))
    out_sds = jax.ShapeDtypeStruct((bsz, seq, GROUP_W), F32)
    return pl.pallas_call(
        _rwkv_prep_kernel,
        grid=(bsz, seq // ts),
        in_specs=[pl.BlockSpec((1, ts, LANES), lambda b, i: (b, i, COL_SMALL)),
                  full((1, LANES)), full((LANES, GROUP_W)), full((1, GROUP_W)),
                  full((LANES, GROUP_W)), full((1, GROUP_W))],
        out_specs=[out_spec] * 2,
        out_shape=[out_sds] * 2,
        scratch_shapes=[pltpu.VMEM((1, LANES), F32)],
        compiler_params=_cparams(("arbitrary", "arbitrary")),
    )(proj, mu_s, wup, vec(w0), aup, vec(a0))


def _pieces(x, n):
    out = []
    for i in range(n):
        p = x.astype(BF16)
        out.append(p)
        if i + 1 < n:
            x = x - p.astype(F32)
    return tuple(out)


def _mm(a, b, nt=False):
    n = max(len(a), len(b))
    dot = _dot_nt if nt else _dot
    acc = None
    for i, ai in enumerate(a):
        for j, bj in enumerate(b):
            if i + j < n:
                term = dot(ai, bj)
                acc = term if acc is None else acc + term
    return acc


def _neumann_inverses(mats):
    n = mats[0].shape[0]
    eye = (lax.broadcasted_iota(jnp.int32, (n, n), 0) == lax.broadcasted_iota(jnp.int32, (n, n), 1)).astype(F32)
    ps = [eye + a for a in mats]
    xs = [_pieces(a, RWKV_PIECES) for a in mats]
    for _ in range(int(math.log2(n)) - 1):
        xs = [_pieces(_mm(x, x), RWKV_PIECES) for x in xs]
        ps = [p + _mm(_pieces(p, RWKV_PIECES), x) for p, x in zip(ps, xs)]
    return ps


def _rwkv_scan_kernel(u_ref, lw_ref, a_ref, g_ref, mu_ref, kkr_ref, lng_ref, lnb_ref, tri_ref, o_ref, state, prev_u):
    c = u_ref.shape[1]
    n1, ng = RWKV_PIECES, RWKV_PIECES_G

    @pl.when(pl.program_id(1) == 0)
    def _():
        state[...] = jnp.zeros_like(state)
        prev_u[...] = jnp.zeros_like(prev_u)

    row = lax.broadcasted_iota(jnp.int32, (c, c), 0)
    col = lax.broadcasted_iota(jnp.int32, (c, c), 1)
    strict, incl = row > col, row >= col
    lo = _lane_lo((c, LANES))
    lo2 = _lane_lo((2 * c, LANES))
    diag_blocks = ((lax.broadcasted_iota(jnp.int32, (LANES, LANES), 0) < HEAD_DIM)
                   == (lax.broadcasted_iota(jnp.int32, (LANES, LANES), 1) < HEAD_DIM))
    tri = (tri_ref[...],)
    units = [(bi, p) for bi in range(u_ref.shape[0]) for p in range(N_PAIRS)]
    lanes = [slice(p * LANES, (p + 1) * LANES) for _, p in units]
    nu = len(units)

    st = []
    for i, (bi, p) in enumerate(units):
        r, k, v = (_token_shift(u_ref[bi, :, sec], prev_u[bi, :, sec], mu_ref[:, sec])
                   for sec in (slice(j * GROUP_W + p * LANES, j * GROUP_W + (p + 1) * LANES) for j in range(3)))
        lw, a = lw_ref[bi, :, lanes[i]], a_ref[bi, :, lanes[i]]
        kk = k * kkr_ref[0:1, lanes[i]]
        kk = kk / jnp.maximum(jnp.sqrt(_seg_sum(kk * kk)), 1e-12)
        k = k * (1.0 + (a - 1.0) * kkr_ref[1:2, lanes[i]])
        aa, b = -kk, kk * a
        cl = _mm(tri, _pieces(lw, 3))
        cl_end = cl[c - 1:c, :]
        cl_mid = cl[c // 2 - 1:c // 2, :]
        w_inv = jnp.exp(cl_mid - cl)
        w_rest = jnp.exp(cl_end - cl)
        qa = jnp.concatenate([aa * jnp.exp(cl - lw - cl_mid), r * jnp.exp(cl - cl_mid)], axis=0)
        bk = _pieces(jnp.concatenate([b * w_inv, k * w_inv], axis=0), ng)
        qa_s = jnp.concatenate([aa * jnp.exp(cl - lw), r * jnp.exp(cl)], axis=0)
        bk_end = _pieces(jnp.concatenate([b * w_rest, k * w_rest], axis=0), n1)
        st.append(dict(r=r, k=k, v=v, qa=qa, bk=bk, qa_s=qa_s, bk_end=bk_end, w_end=jnp.exp(cl_end),
                       v_p=_pieces(v, n1)))

    bases = [_mm(_pieces(st[i]["qa_s"], n1), _pieces(state[i], n1), nt=True) for i in range(nu)]
    a_ab, a_ak, a_r = [], [], []
    for i in range(nu):
        for h in range(2):
            g = _mm(_pieces(jnp.where(lo2 == (h == 0), st[i]["qa"], 0.0), ng), st[i]["bk"], nt=True)
            a_ab.append(jnp.where(strict, g[:c, :c], 0.0))
            a_ak.append(jnp.where(strict, g[:c, c:], 0.0))
            a_r.append(jnp.concatenate([jnp.where(incl, g[c:, :c], 0.0), jnp.where(incl, g[c:, c:], 0.0)], axis=1))
    rhs = [bases[j // 2][:c] + _mm(_pieces(a_ak[j], n1), st[j // 2]["v_p"]) for j in range(2 * nu)]

    t_inv = _neumann_inverses(a_ab)
    u_heads = [_mm(_pieces(t_inv[j], n1), _pieces(rhs[j], n1)) for j in range(2 * nu)]
    for i, (bi, _) in enumerate(units):
        r, k, v = st[i]["r"], st[i]["k"], st[i]["v"]
        u = jnp.where(lo, u_heads[2 * i], u_heads[2 * i + 1])
        uv = _pieces(jnp.concatenate([u, v], axis=0), n1)
        y = bases[i][c:] + jnp.where(lo, _mm(_pieces(a_r[2 * i], n1), uv), _mm(_pieces(a_r[2 * i + 1], n1), uv))
        upd = _mm(_pieces(jnp.concatenate([u.T, v.T], axis=1), n1), st[i]["bk_end"])
        state[i] = state[i] * st[i]["w_end"] + jnp.where(diag_blocks, upd, 0.0)

        mean = _seg_sum(y) * (1.0 / HEAD_DIM)
        yc = y - mean
        var = _seg_sum(yc * yc) * (1.0 / HEAD_DIM)
        out = yc * lax.rsqrt(var + RWKV_LN_EPS) * lng_ref[:, lanes[i]] + lnb_ref[:, lanes[i]]
        out = out + _seg_sum(r * k * kkr_ref[2:3, lanes[i]]) * v
        o_ref[bi, :, lanes[i]] = (out * _silu(g_ref[bi, :, lanes[i]])).astype(o_ref.dtype)
    for bi in range(u_ref.shape[0]):
        prev_u[bi] = u_ref[bi, c - 1:, :]


def _rwkv_scan(proj, log_decay, iclr, mu, kkr, ln_gain, ln_bias):
    bsz, seq, _ = proj.shape
    c = min(RWKV_CHUNK, seq)
    tri = (lax.broadcasted_iota(jnp.int32, (c, c), 0) >= lax.broadcasted_iota(jnp.int32, (c, c), 1)).astype(BF16)
    nb = RWKV_ROWS if bsz % RWKV_ROWS == 0 else 1
    tok_spec = pl.BlockSpec((nb, c, GROUP_W), lambda b, i: (b, i, 0))
    full = lambda shape: pl.BlockSpec(shape, lambda b, i: (0,) * len(shape))
    vec = lambda a: a.astype(F32).reshape(1, GROUP_W)
    return pl.pallas_call(
        _rwkv_scan_kernel,
        grid=(bsz // nb, seq // c),
        in_specs=[pl.BlockSpec((nb, c, 3 * GROUP_W), lambda b, i: (b, i, COL_D * LANES // (3 * GROUP_W))),
                  tok_spec, tok_spec,
                  pl.BlockSpec((nb, c, GROUP_W), lambda b, i: (b, i, COL_DG * LANES // GROUP_W)),
                  full((1, 3 * GROUP_W)), full((3, GROUP_W)), full((1, GROUP_W)), full((1, GROUP_W)), full((c, c))],
        out_specs=tok_spec,
        out_shape=jax.ShapeDtypeStruct((bsz, seq, GROUP_W), BF16),
        scratch_shapes=[pltpu.VMEM((nb * N_PAIRS, LANES, LANES), F32), pltpu.VMEM((nb, 1, 3 * GROUP_W), F32)],
        compiler_params=_cparams(("arbitrary", "arbitrary")),
    )(proj, log_decay, iclr, proj, mu.astype(F32)[:3 * GROUP_W].reshape(1, 3 * GROUP_W), kkr.astype(F32),
      vec(ln_gain), vec(ln_bias), tri)


def _out_kernel(a_ref, b_ref, c_ref, d_ref, w_ref, x_ref, o_ref):
    acc = x_ref[...]
    for i, m_ref in enumerate((a_ref, b_ref, c_ref, d_ref)):
        acc = acc + _dot(m_ref[...], w_ref[i * GROUP_W:(i + 1) * GROUP_W, :])
    o_ref[...] = acc


def _out_proj(groups, w_out, x2d):
    m, d = x2d.shape
    tm = min(512, m)
    g_spec = pl.BlockSpec((tm, GROUP_W), lambda i: (i, 0))
    return pl.pallas_call(
        _out_kernel,
        grid=(m // tm,),
        in_specs=[g_spec] * 4 + [pl.BlockSpec(w_out.shape, lambda i: (0, 0)),
                                 pl.BlockSpec((tm, d), lambda i: (i, 0))],
        out_specs=pl.BlockSpec((tm, d), lambda i: (i, 0)),
        out_shape=jax.ShapeDtypeStruct((m, d), F32),
        compiler_params=_cparams(("arbitrary",)),
    )(*[g.reshape(m, GROUP_W) for g in groups], w_out, x2d)


def _t5_causal_bucket(dist):
    max_exact = NUM_BUCKETS // 2
    d = jnp.maximum(dist, 1).astype(F32)
    large = max_exact + (jnp.log(d / max_exact) / math.log(MAX_DISTANCE / max_exact)
                         * (NUM_BUCKETS - max_exact)).astype(jnp.int32)
    large = jnp.minimum(large, NUM_BUCKETS - 1)
    return jnp.where(dist < max_exact, dist, large)


def _bias_tiles(rel_bias, seq, t):
    assert t >= MAX_DISTANCE
    bias_by_dist = rel_bias.astype(F32)[_t5_causal_bucket(jnp.arange(seq))] * LOG2E
    m = np.arange(2 * t)
    tiles = []
    for delta in range(3):
        dist = np.where(m < t, delta * t - m, delta * t + 2 * t - m)
        v = bias_by_dist[np.clip(dist, 0, seq - 1)].T
        skew = jnp.tile(v, (1, t))[:, :t * (2 * t - 1)].reshape(-1, t, 2 * t - 1)
        tiles.append(skew[:, :, :t])
    return jnp.stack(tiles, axis=1)


def _pack_w_in(w):
    d = w.shape[0]
    att = w[:, :12 * GROUP_W]
    cf = w[:, 12 * GROUP_W:12 * GROUP_W + 8]
    off = 12 * GROUP_W + 8
    rkv = w[:, off:off + 3 * GROUP_W]
    w_lo = w[:, off + 3 * GROUP_W:off + 3 * GROUP_W + RWKV_RANK]
    a_lo = w[:, off + 3 * GROUP_W + RWKV_RANK:off + 3 * GROUP_W + 2 * RWKV_RANK]
    dg = w[:, off + 3 * GROUP_W + 2 * RWKV_RANK:]
    small = (jnp.zeros((d, LANES), w.dtype).at[:, :8].set(cf)
             .at[:, SMALL_WLO:SMALL_WLO + RWKV_RANK].set(w_lo)
             .at[:, SMALL_ALO:SMALL_ALO + RWKV_RANK].set(a_lo))
    return jnp.concatenate([att, rkv, dg, small, jnp.zeros((d, LANES), w.dtype)], axis=1).astype(BF16)


def kernel(x, norm_gain, w_in, w_out, rel_bias, qk_gain, diff_lambda, forget_bias, out_gain, rwkv_mu, rwkv_w_up,
           rwkv_w0, rwkv_a_up, rwkv_a0, rwkv_kkr, rwkv_ln_gain, rwkv_ln_bias):
    bsz, seq, d = x.shape
    tz = _bias_tiles(rel_bias, seq, min(ATT_TILE, seq))
    bias_max = jnp.max(jnp.abs(rel_bias.astype(F32))) * LOG2E
    h = x.reshape(bsz * seq, d)
    for l in range(DEPTH):
        og = out_gain[l].astype(F32)
        proj = _proj(h, norm_gain[l].astype(F32), _pack_w_in(w_in[l])).reshape(bsz, seq, N_PROJ)
        lam_init = 0.8 - 0.6 * math.exp(-0.3 * l)
        qg = qk_gain[l].astype(F32)
        bound_a = _score_bound_log2(qg[0], qg[1]) + bias_max
        bound_c = _score_bound_log2(qg[2], qg[3])
        att_a = functools.partial(_attn_a, proj, q_gain=qg[0], k_gain=qg[1], o_gain=og[:GROUP_W],
                                  dlam=diff_lambda[l].astype(F32), lam_init=lam_init)
        oa = lax.cond(bound_a <= MAX_SHIFT_LOG2, lambda: att_a(tz=tz - bound_a, bounded=True),
                      lambda: att_a(tz=tz, bounded=False))
        ob = _attn_b(proj, og[GROUP_W:2 * GROUP_W])
        att_c = functools.partial(_attn_c, proj, *_fox_cum(proj, forget_bias[l]), bound_c, qg[2], qg[3],
                                  og[2 * GROUP_W:3 * GROUP_W])
        oc = lax.cond(bound_c <= MAX_SHIFT_LOG2, lambda: att_c(bounded=True), lambda: att_c(bounded=False))
        log_decay, iclr = _rwkv_prep(proj, rwkv_mu[l], rwkv_w_up[l], rwkv_w0[l], rwkv_a_up[l], rwkv_a0[l])
        od = _rwkv_scan(proj, log_decay, iclr, rwkv_mu[l], rwkv_kkr[l],
                        rwkv_ln_gain[l].astype(F32) * og[3 * GROUP_W:], rwkv_ln_bias[l])
        h = _out_proj((oa, ob, oc, od), w_out[l].astype(BF16), h)
    return h.reshape(bsz, seq, d)
```

```python
import functools
import math

import numpy as np
import jax
import jax.numpy as jnp
from jax import lax
from jax.experimental import pallas as pl
from jax.experimental.pallas import tpu as pltpu

F32 = jnp.float32
BF16 = jnp.bfloat16

D_MODEL = 1024
DEPTH = 2
GROUP_W = 512
HEAD_DIM = 64
H_A = 4
NUM_BUCKETS = 32
MAX_DISTANCE = 128
RWKV_RANK = 32
NORM_EPS = 1e-6
RWKV_LN_EPS = 64e-5
NEG_INF = -1e30

LANES = 128
N_PAIRS = GROUP_W // LANES
ATT_SCALE = HEAD_DIM ** -0.5
LOG2E = math.log2(math.e)

COL_A, COL_B, COL_C = 0, 16, 32
COL_D = 48
COL_DG = 60
COL_SMALL = 64
N_PROJ = (COL_SMALL + 2) * LANES
PROJ_COL_STEPS = 3
SMALL_WLO, SMALL_ALO = 32, 64

ATT_TILE = 512
ATT_PAIRS = 2
MAX_SHIFT_LOG2 = 56.0
STICK_TILE = 256
STICK_BLOCK = 256
STICK_DEAD_LOG2 = -152.0
PREP_ROWS = 512
FOX_ROWS = 256
RWKV_CHUNK = 128
RWKV_ROWS = 2
RWKV_PIECES = 1
RWKV_PIECES_G = 2
VMEM_LIMIT = 56 * 1024 * 1024


def _cparams(sem):
    return pltpu.CompilerParams(dimension_semantics=sem, vmem_limit_bytes=VMEM_LIMIT)


def _dot(a, b, **kw):
    return jnp.dot(a, b, preferred_element_type=F32, **kw)


def _dot_nt(a, b, **kw):
    return lax.dot_general(a, b, (((1,), (1,)), ((), ())), preferred_element_type=F32, **kw)


def _lane_lo(shape):
    return lax.broadcasted_iota(jnp.int32, shape, len(shape) - 1) < HEAD_DIM


def _seg_sum(x):
    lo = _lane_lo(x.shape)
    s_lo = jnp.sum(jnp.where(lo, x, 0.0), axis=-1, keepdims=True)
    s_hi = jnp.sum(jnp.where(lo, 0.0, x), axis=-1, keepdims=True)
    return jnp.where(lo, s_lo, s_hi)


def _group_mean_mxu(x, group):
    n = x.shape[-1]
    same = (lax.broadcasted_iota(jnp.int32, (n, n), 0) // group) == (lax.broadcasted_iota(jnp.int32, (n, n), 1) // group)
    return _mm(_pieces(x, 3), (same.astype(BF16),)) * (1.0 / group)


def _seg_rms(x, gain, mxu=False):
    ms = _group_mean_mxu(x * x, HEAD_DIM) if mxu else _seg_sum(x * x) * (1.0 / HEAD_DIM)
    return x * lax.rsqrt(ms + NORM_EPS) * gain


def _log_sigmoid(z):
    return jnp.minimum(z, 0.0) - jnp.log1p(jnp.exp(-jnp.abs(z)))


def _silu(g):
    return g / (1.0 + jnp.exp(-g))


def _proj_kernel(x_ref, g_ref, w_ref, o_ref, h_ref):
    @pl.when(pl.program_id(1) == 0)
    def _():
        x = x_ref[...]
        ms = jnp.mean(x * x, axis=-1, keepdims=True)
        h_ref[...] = (x * lax.rsqrt(ms + NORM_EPS) * g_ref[...]).astype(BF16)

    o_ref[...] = _dot(h_ref[...], w_ref[...])


def _proj(x2d, gain, w_packed):
    m, d = x2d.shape
    n = w_packed.shape[1]
    tm = min(1024, m)
    tn = n // PROJ_COL_STEPS
    return pl.pallas_call(
        _proj_kernel,
        grid=(m // tm, n // tn),
        in_specs=[pl.BlockSpec((tm, d), lambda i, j: (i, 0)),
                  pl.BlockSpec((1, d), lambda i, j: (0, 0)),
                  pl.BlockSpec((d, tn), lambda i, j: (0, j))],
        out_specs=pl.BlockSpec((tm, tn), lambda i, j: (i, j)),
        out_shape=jax.ShapeDtypeStruct((m, n), F32),
        scratch_shapes=[pltpu.VMEM((tm, d), BF16)],
        compiler_params=_cparams(("arbitrary", "arbitrary")),
    )(x2d, gain.reshape(1, d), w_packed)


def _for_row_blocks(seq, fn):
    rows_per = min(PREP_ROWS, seq)

    def body(c, carry):
        fn(pl.ds(pl.multiple_of(c * rows_per, rows_per), rows_per), rows_per)
        return carry

    lax.fori_loop(0, seq // rows_per, body, 0)


def _head_masked(q):
    lo = _lane_lo(q.shape)
    return jnp.where(lo, q, 0.0).astype(BF16), jnp.where(lo, 0.0, q).astype(BF16)


def _online_softmax(scores, m_ref, acc_ref, values, first):
    row_max = [jnp.max(s, axis=-1, keepdims=True) for s in scores]
    m_old = [None if first else m_ref[i] for i in range(len(scores))]
    m_new = [r if first else jnp.maximum(m, r) for r, m in zip(row_max, m_old)]
    probs = [jnp.exp2(s - m).astype(BF16) for s, m in zip(scores, m_new)]
    for i, (p, mo, mn, v) in enumerate(zip(probs, m_old, m_new, values)):
        acc_ref[i] = _dot(p, v) if first else jnp.exp2(mo - mn) * acc_ref[i] + _dot(p, v)
        m_ref[i] = mn


def _shifted_softmax(scores, m_ref, acc_ref, values, first):
    for i, (s, v) in enumerate(zip(scores, values)):
        pv = _dot(jnp.exp2(s).astype(BF16), v)
        acc_ref[i] = pv if first else acc_ref[i] + pv


def _causal_tiles(qi, tile):
    def two(i, c):
        tile(2 * i, False)
        tile(2 * i + 1, False)
        return c

    def one(j, c):
        tile(j, False)
        return c

    tile(qi, True)
    lax.fori_loop(0, qi // 2, two, 0)
    lax.fori_loop(2 * (qi // 2), qi, one, 0)


def _attn_a_kernel(q_ref, k_ref, v_ref, g_ref, tz_ref, qg_ref, kg_ref, og_ref, dl_ref, o_ref, kbf, vbf, m_ref,
                   acc_ref, *, lam_init, bounded):
    softmax = _shifted_softmax if bounded else _online_softmax
    qi = pl.program_id(2)
    t = q_ref.shape[1]
    heads = range(q_ref.shape[2] // LANES)
    sl = [slice(h * LANES, (h + 1) * LANES) for h in heads]
    vsl = [slice(2 * h * LANES, 2 * (h + 1) * LANES) for h in heads]

    @pl.when(qi == 0)
    def _():
        def stage(rows, n):
            for h in heads:
                kbf[rows, sl[h]] = _seg_rms(k_ref[0, rows, sl[h]], kg_ref[...]).astype(BF16)
                vbf[rows, vsl[h]] = jnp.concatenate([v_ref[0, rows, sl[h]].astype(BF16),
                                                     jnp.ones((n, LANES), BF16)], axis=1)

        _for_row_blocks(k_ref.shape[1], stage)

    qs = [_head_masked(_seg_rms(q_ref[0, :, sl[h]], qg_ref[...]) * (ATT_SCALE * LOG2E)) for h in heads]
    row = lax.broadcasted_iota(jnp.int32, (t, t), 0)
    col = lax.broadcasted_iota(jnp.int32, (t, t), 1)

    def tile(j, diag):
        ks = pl.ds(pl.multiple_of(j * t, t), t)
        delta = jnp.minimum(qi - j, 2)
        scores = [_dot_nt(q, kbf[ks, sl[h]]) + tz_ref[h, delta] for h in heads for q in qs[h]]
        if diag:
            scores = [jnp.where(row >= col, s, NEG_INF) for s in scores]
        softmax(scores, m_ref, acc_ref, [vbf[ks, vsl[h]] for h in heads for _ in range(2)], first=diag)

    _causal_tiles(qi, tile)

    dl = dl_ref[...]
    lam = (jnp.exp(jnp.sum(dl[0:1] * dl[1:2], axis=-1, keepdims=True))
           - jnp.exp(jnp.sum(dl[2:3] * dl[3:4], axis=-1, keepdims=True)) + lam_init)
    for h in heads:
        acc0, acc1 = acc_ref[2 * h], acc_ref[2 * h + 1]
        o = acc0[:, :LANES] / acc0[:, LANES:] - lam * (acc1[:, :LANES] / acc1[:, LANES:])
        y = o * lax.rsqrt(jnp.mean(o * o, axis=-1, keepdims=True) + NORM_EPS) * og_ref[0, :, sl[h]] * (1.0 - lam_init)
        o_ref[0, :, sl[h]] = (y * _silu(g_ref[0, :, sl[h]])).astype(o_ref.dtype)


def _attn_c_kernel(q_ref, k_ref, v_ref, g_ref, f_ref, fq_ref, bound_ref, qg_ref, kg_ref, og_ref, o_ref,
                   kbf, vbf0, vbf1, m_ref, acc_ref, *, bounded):
    softmax = _shifted_softmax if bounded else _online_softmax
    qi = pl.program_id(2)
    t = q_ref.shape[1]
    pairs = range(q_ref.shape[2] // LANES)
    sl = [slice(p * LANES, (p + 1) * LANES) for p in pairs]

    @pl.when(qi == 0)
    def _():
        def stage(rows, n):
            for p in pairs:
                kbf[rows, sl[p]] = _seg_rms(k_ref[0, rows, sl[p]], kg_ref[...]).astype(BF16)
                v = v_ref[0, rows, sl[p]]
                lo = _lane_lo(v.shape)
                vbf0[rows, sl[p]] = jnp.where(lo, v, 1.0).astype(BF16)
                vbf1[rows, sl[p]] = jnp.where(lo, 1.0, v).astype(BF16)

        _for_row_blocks(k_ref.shape[1], stage)

    qs = [_head_masked(_seg_rms(q_ref[0, :, sl[p]], qg_ref[...]) * (ATT_SCALE * LOG2E)) for p in pairs]
    if bounded:
        fqs = [[fq_ref[0, p, :, h * HEAD_DIM:h * HEAD_DIM + 1] * LOG2E - bound_ref[:, :1] for h in range(2)]
               for p in pairs]
    else:
        fqs = [[0.0, 0.0] for _ in pairs]
    row = lax.broadcasted_iota(jnp.int32, (t, t), 0)
    col = lax.broadcasted_iota(jnp.int32, (t, t), 1)

    def tile(j, diag):
        ks = pl.ds(pl.multiple_of(j * t, t), t)
        scores = [_dot_nt(qs[p][h], kbf[ks, sl[p]]) + (fqs[p][h] - f_ref[0, 0, 2 * p + h:2 * p + h + 1, ks] * LOG2E)
                  for p in pairs for h in range(2)]
        if diag:
            scores = [jnp.where(row >= col, s, NEG_INF) for s in scores]
        softmax(scores, m_ref, acc_ref, [vb[ks, sl[p]] for p in pairs for vb in (vbf0, vbf1)], first=diag)

    _causal_tiles(qi, tile)
    for p in pairs:
        acc0, acc1 = acc_ref[2 * p], acc_ref[2 * p + 1]
        lo = _lane_lo(acc0.shape)
        denom = jnp.where(lo, pltpu.roll(acc0, HEAD_DIM, 1), pltpu.roll(acc1, HEAD_DIM, 1))
        o = jnp.where(lo, acc0, acc1) / denom
        o_ref[0, :, sl[p]] = (_seg_rms(o, og_ref[0, :, sl[p]], mxu=True) * _silu(g_ref[0, :, sl[p]])).astype(o_ref.dtype)


def _attn_b_kernel(q_ref, k_ref, v_ref, g_ref, u_ref, og_ref, o_ref, kbf, vbf, c_ref, acc_ref):
    qi = pl.program_id(2)
    t = q_ref.shape[1]
    blk = u_ref.shape[1]
    pairs = range(q_ref.shape[2] // LANES)
    sl = [slice(p * LANES, (p + 1) * LANES) for p in pairs]

    @pl.when(qi == 0)
    def _():
        def stage(rows, n):
            kbf[rows, :] = k_ref[0, rows, :].astype(BF16)
            vbf[rows, :] = v_ref[0, rows, :].astype(BF16)

        _for_row_blocks(k_ref.shape[1], stage)

    qs = [q for p in pairs for q in _head_masked(q_ref[0, :, sl[p]] * (ATT_SCALE * LOG2E))]
    row = lax.broadcasted_iota(jnp.int32, (t, t), 0)
    col = lax.broadcasted_iota(jnp.int32, (t, t), 1)
    umat = u_ref[...]

    def tile(j, diag):
        ks = pl.ds(pl.multiple_of(j * t, t), t)
        zs = [_dot_nt(q, kbf[ks, sl[i // 2]]) for i, q in enumerate(qs)]
        lbs = [jnp.minimum(z, 0.0) - jnp.log2(1.0 + jnp.exp2(-jnp.abs(z))) for z in zs]
        l1s = [lb - z for lb, z in zip(lbs, zs)]
        if diag:
            l1s = [jnp.where(row > col, l1, 0.0) for l1 in l1s]
        tails = [jnp.zeros((t, 1), F32) if diag else c_ref[i] for i in range(len(qs))]
        afters = [[] for _ in qs]
        for b in reversed(range(t // blk)):
            for i, l1 in enumerate(l1s):
                x = l1[:, b * blk:(b + 1) * blk]
                hi = x.astype(BF16)
                lo = (x - hi.astype(F32)).astype(BF16)
                afters[i].insert(0, _dot(jnp.concatenate([hi, lo], axis=1), umat) + tails[i])
                tails[i] = tails[i] + jnp.sum(x, axis=-1, keepdims=True)
        for i, (lb, after, c_new) in enumerate(zip(lbs, afters, tails)):
            a = jnp.exp2(lb + (after[0] if len(after) == 1 else jnp.concatenate(after, axis=1)))
            if diag:
                a = jnp.where(row > col, a, 0.0)
            av = _dot(a.astype(BF16), vbf[ks, sl[i // 2]])
            acc_ref[i] = av if diag else acc_ref[i] + av
            c_ref[i] = c_new
        return functools.reduce(jnp.maximum, [jnp.max(c) for c in tails])

    def alive(state):
        return jnp.logical_and(state[0] < qi, state[1] > STICK_DEAD_LOG2)

    lax.while_loop(alive, lambda st: (st[0] + 1, tile(qi - 1 - st[0], False)), (jnp.int32(0), tile(qi, True)))
    for p in pairs:
        o = jnp.where(_lane_lo((t, LANES)), acc_ref[2 * p], acc_ref[2 * p + 1])
        o_ref[0, :, sl[p]] = (_seg_rms(o, og_ref[0, :, sl[p]]) * _silu(g_ref[0, :, sl[p]])).astype(o_ref.dtype)


def _attn_specs(seq, t, col, pp):
    w = pp * LANES
    q_spec = pl.BlockSpec((1, t, w), lambda b, p, i: (b, i, col // pp + p))
    k_spec = pl.BlockSpec((1, seq, w), lambda b, p, i: (b, 0, (col + N_PAIRS) // pp + p))
    v_spec = pl.BlockSpec((1, seq, w), lambda b, p, i: (b, 0, (col + 2 * N_PAIRS) // pp + p))
    g_spec = pl.BlockSpec((1, t, w), lambda b, p, i: (b, i, (col + 3 * N_PAIRS) // pp + p))
    return [q_spec, k_spec, v_spec, g_spec]


def _pair_spec(pp):
    return pl.BlockSpec((1, 1, pp * LANES), lambda b, p, i: (p, 0, 0))


def _const_spec(shape):
    return pl.BlockSpec(shape, lambda b, p, i: (0,) * len(shape))


def _attn_call(body, proj, col, extra_specs, extra_args, pp, acc_lanes, tile=ATT_TILE, v_lanes=(LANES,)):
    bsz, seq, _ = proj.shape
    t = min(tile, seq)
    w = pp * LANES
    return pl.pallas_call(
        body,
        grid=(bsz, N_PAIRS // pp, seq // t),
        in_specs=_attn_specs(seq, t, col, pp) + extra_specs,
        out_specs=pl.BlockSpec((1, t, w), lambda b, p, i: (b, i, p)),
        out_shape=jax.ShapeDtypeStruct((bsz, seq, GROUP_W), BF16),
        scratch_shapes=[pltpu.VMEM((seq, w), BF16)] + [pltpu.VMEM((seq, pp * n), BF16) for n in v_lanes]
        + [pltpu.VMEM((2 * pp, t, 1), F32), pltpu.VMEM((2 * pp, t, acc_lanes), F32)],
        compiler_params=_cparams(("arbitrary", "arbitrary", "arbitrary")),
    )(proj, proj, proj, proj, *extra_args)


def _pair_gain(g64):
    return jnp.tile(g64.astype(F32), 2).reshape(1, LANES)


def _score_bound_log2(q_gain, k_gain):
    return (HEAD_DIM * ATT_SCALE * LOG2E * 1.02) * jnp.max(jnp.abs(q_gain)) * jnp.max(jnp.abs(k_gain))


def _attn_a(proj, tz, q_gain, k_gain, o_gain, dlam, lam_init, bounded):
    t = tz.shape[-1]
    pp = ATT_PAIRS
    specs = [pl.BlockSpec((pp, 3, t, t), lambda b, p, i: (p, 0, 0, 0)),
             _const_spec((1, LANES)), _const_spec((1, LANES)), _pair_spec(pp), _const_spec((4, HEAD_DIM))]
    args = (tz, _pair_gain(q_gain), _pair_gain(k_gain), o_gain.reshape(N_PAIRS // pp, 1, pp * LANES), dlam)
    return _attn_call(functools.partial(_attn_a_kernel, lam_init=lam_init, bounded=bounded), proj, COL_A, specs,
                      args, pp=pp, v_lanes=(2 * LANES,), acc_lanes=2 * LANES)


def _attn_b(proj, o_gain):
    t = min(STICK_BLOCK, STICK_TILE, proj.shape[1])
    umat = (lax.broadcasted_iota(jnp.int32, (t, t), 0) > lax.broadcasted_iota(jnp.int32, (t, t), 1)).astype(BF16)
    umat = jnp.concatenate([umat, umat], axis=0)
    pp = ATT_PAIRS
    specs = [_const_spec((2 * t, t)), _pair_spec(pp)]
    return _attn_call(_attn_b_kernel, proj, COL_B, specs, (umat, o_gain.reshape(N_PAIRS // pp, 1, pp * LANES)),
                      pp=pp, tile=STICK_TILE, acc_lanes=LANES)


def _attn_c(proj, cum_rows, cum_cols, bound, q_gain, k_gain, o_gain, bounded):
    seq = proj.shape[1]
    t = min(ATT_TILE, seq)
    pp = ATT_PAIRS
    specs = [pl.BlockSpec((1, 1, 2 * pp, seq), lambda b, p, i: (b, p, 0, 0)),
             pl.BlockSpec((1, pp, t, LANES), lambda b, p, i: (b, p, i, 0)),
             _const_spec((1, LANES)), _const_spec((1, LANES)), _const_spec((1, LANES)), _pair_spec(pp)]
    args = (cum_rows.reshape(cum_rows.shape[0], N_PAIRS // pp, 2 * pp, seq), cum_cols,
            jnp.full((1, LANES), bound, F32), _pair_gain(q_gain), _pair_gain(k_gain),
            o_gain.reshape(N_PAIRS // pp, 1, pp * LANES))
    return _attn_call(functools.partial(_attn_c_kernel, bounded=bounded), proj, COL_C, specs, args,
                      pp=pp, v_lanes=(LANES, LANES), acc_lanes=LANES)


def _fox_cum_kernel(s_ref, fb_ref, tri_ref, rows_ref, cols_ref):
    seq = s_ref.shape[1]
    rows_per = tri_ref.shape[0]
    carry = jnp.zeros((1, LANES), F32)
    for c in range(seq // rows_per):
        rows = slice(c * rows_per, (c + 1) * rows_per)
        log_f = _log_sigmoid(s_ref[0, rows, :] + fb_ref[...])
        cum = _mm((tri_ref[...],), _pieces(log_f, 3)) + carry
        carry = cum[rows_per - 1:, :]
        rows_ref[0, :, rows] = cum.T[:8, :]
        lo = _lane_lo(cum.shape)
        for p in range(N_PAIRS):
            cols_ref[0, p, rows, :] = jnp.where(lo, cum[:, 2 * p:2 * p + 1], cum[:, 2 * p + 1:2 * p + 2])


def _fox_cum(proj, forget_bias):
    bsz, seq, _ = proj.shape
    rows_per = min(FOX_ROWS, seq)
    tri = (lax.broadcasted_iota(jnp.int32, (rows_per, rows_per), 0)
           >= lax.broadcasted_iota(jnp.int32, (rows_per, rows_per), 1)).astype(BF16)
    fb = jnp.pad(forget_bias.astype(F32), (0, LANES - forget_bias.shape[0])).reshape(1, LANES)
    cum_rows, cum_cols = pl.pallas_call(
        _fox_cum_kernel,
        grid=(bsz,),
        in_specs=[pl.BlockSpec((1, seq, LANES), lambda b: (b, 0, COL_SMALL)),
                  pl.BlockSpec((1, LANES), lambda b: (0, 0)),
                  pl.BlockSpec((rows_per, rows_per), lambda b: (0, 0))],
        out_specs=[pl.BlockSpec((1, 8, seq), lambda b: (b, 0, 0)),
                   pl.BlockSpec((1, N_PAIRS, seq, LANES), lambda b: (b, 0, 0, 0))],
        out_shape=[jax.ShapeDtypeStruct((bsz, 8, seq), F32),
                   jax.ShapeDtypeStruct((bsz, N_PAIRS, seq, LANES), F32)],
        compiler_params=_cparams(("arbitrary",)),
    )(proj, fb, tri)
    return cum_rows.reshape(bsz, N_PAIRS, 2, seq), cum_cols


def _token_shift(x, prev_row, mu):
    first = lax.broadcasted_iota(jnp.int32, x.shape, 0) == 0
    x_prev = jnp.where(first, prev_row, pltpu.roll(x, 1, 0))
    return x + (x_prev - x) * mu


def _rwkv_prep_kernel(s_ref, mus_ref, wup_ref, w0_ref, aup_ref, a0_ref, lw_o, a_o, prev_s):
    @pl.when(pl.program_id(1) == 0)
    def _():
        prev_s[...] = jnp.zeros_like(prev_s)

    raw = s_ref[0]
    sm = _token_shift(raw, prev_s[...], mus_ref[...])
    prev_s[...] = raw[raw.shape[0] - 1:, :]
    w_pre = w0_ref[...] + _mm(_pieces(jnp.tanh(sm), 2), _pieces(wup_ref[...], 2))
    lw_o[0] = (-math.exp(-0.5)) / (1.0 + jnp.exp(-w_pre))
    a_o[0] = 1.0 / (1.0 + jnp.exp(-(a0_ref[...] + _mm(_pieces(sm, 2), _pieces(aup_ref[...], 2)))))


def _rwkv_prep(proj, mu, w_up, w0, a_up, a0):
    bsz, seq, _ = proj.shape
    ts = min(PREP_ROWS, seq)
    mu = mu.astype(F32)
    zeros = lambda n: jnp.zeros((n,), F32)
    mu_s = jnp.concatenate([zeros(SMALL_WLO), mu[3 * GROUP_W:3 * GROUP_W + RWKV_RANK],
                            mu[3 * GROUP_W + RWKV_RANK:], zeros(LANES - SMALL_ALO - RWKV_RANK)]).reshape(1, LANES)
    wup = jnp.pad(w_up.astype(F32), ((SMALL_WLO, LANES - SMALL_WLO - RWKV_RANK), (0, 0)))
    aup = jnp.pad(a_up.astype(F32), ((SMALL_ALO, LANES - SMALL_ALO - RWKV_RANK), (0, 0)))
    vec = lambda a: a.astype(F32).reshape(1, GROUP_W)
    full = lambda shape: pl.BlockSpec(shape, lambda b, i: (0,) * len(shape))
    out_spec = pl.BlockSpec((1, ts, GROUP_W), lambda b, i: (b, i, 0))
    out_sds = jax.ShapeDtypeStruct((bsz, seq, GROUP_W), F32)
    return pl.pallas_call(
        _rwkv_prep_kernel,
        grid=(bsz, seq // ts),
        in_specs=[pl.BlockSpec((1, ts, LANES), lambda b, i: (b, i, COL_SMALL)),
                  full((1, LANES)), full((LANES, GROUP_W)), full((1, GROUP_W)),
                  full((LANES, GROUP_W)), full((1, GROUP_W))],
        out_specs=[out_spec] * 2,
        out_shape=[out_sds] * 2,
        scratch_shapes=[pltpu.VMEM((1, LANES), F32)],
        compiler_params=_cparams(("arbitrary", "arbitrary")),
    )(proj, mu_s, wup, vec(w0), aup, vec(a0))


def _pieces(x, n):
    out = []
    for i in range(n):
        p = x.astype(BF16)
        out.append(p)
        if i + 1 < n:
            x = x - p.astype(F32)
    return tuple(out)


def _mm(a, b, nt=False):
    n = max(len(a), len(b))
    dot = _dot_nt if nt else _dot
    acc = None
    for i, ai in enumerate(a):
        for j, bj in enumerate(b):
            if i + j < n:
                term = dot(ai, bj)
                acc = term if acc is None else acc + term
    return acc


def _neumann_inverses(mats):
    n = mats[0].shape[0]
    eye = (lax.broadcasted_iota(jnp.int32, (n, n), 0) == lax.broadcasted_iota(jnp.int32, (n, n), 1)).astype(F32)
    ps = [eye + a for a in mats]
    xs = [_pieces(a, RWKV_PIECES) for a in mats]
    for _ in range(int(math.log2(n)) - 1):
        xs = [_pieces(_mm(x, x), RWKV_PIECES) for x in xs]
        ps = [p + _mm(_pieces(p, RWKV_PIECES), x) for p, x in zip(ps, xs)]
    return ps


def _rwkv_scan_kernel(u_ref, lw_ref, a_ref, g_ref, mu_ref, kkr_ref, lng_ref, lnb_ref, tri_ref, o_ref, state, prev_u):
    c = u_ref.shape[1]
    n1, ng = RWKV_PIECES, RWKV_PIECES_G

    @pl.when(pl.program_id(1) == 0)
    def _():
        state[...] = jnp.zeros_like(state)
        prev_u[...] = jnp.zeros_like(prev_u)

    row = lax.broadcasted_iota(jnp.int32, (c, c), 0)
    col = lax.broadcasted_iota(jnp.int32, (c, c), 1)
    strict, incl = row > col, row >= col
    lo = _lane_lo((c, LANES))
    lo2 = _lane_lo((2 * c, LANES))
    diag_blocks = ((lax.broadcasted_iota(jnp.int32, (LANES, LANES), 0) < HEAD_DIM)
                   == (lax.broadcasted_iota(jnp.int32, (LANES, LANES), 1) < HEAD_DIM))
    tri = (tri_ref[...],)
    units = [(bi, p) for bi in range(u_ref.shape[0]) for p in range(N_PAIRS)]
    lanes = [slice(p * LANES, (p + 1) * LANES) for _, p in units]
    nu = len(units)

    st = []
    for i, (bi, p) in enumerate(units):
        r, k, v = (_token_shift(u_ref[bi, :, sec], prev_u[bi, :, sec], mu_ref[:, sec])
                   for sec in (slice(j * GROUP_W + p * LANES, j * GROUP_W + (p + 1) * LANES) for j in range(3)))
        lw, a = lw_ref[bi, :, lanes[i]], a_ref[bi, :, lanes[i]]
        kk = k * kkr_ref[0:1, lanes[i]]
        kk = kk / jnp.maximum(jnp.sqrt(_seg_sum(kk * kk)), 1e-12)
        k = k * (1.0 + (a - 1.0) * kkr_ref[1:2, lanes[i]])
        aa, b = -kk, kk * a
        cl = _mm(tri, _pieces(lw, 3))
        cl_end = cl[c - 1:c, :]
        cl_mid = cl[c // 2 - 1:c // 2, :]
        w_inv = jnp.exp(cl_mid - cl)
        w_rest = jnp.exp(cl_end - cl)
        qa = jnp.concatenate([aa * jnp.exp(cl - lw - cl_mid), r * jnp.exp(cl - cl_mid)], axis=0)
        bk = _pieces(jnp.concatenate([b * w_inv, k * w_inv], axis=0), ng)
        qa_s = jnp.concatenate([aa * jnp.exp(cl - lw), r * jnp.exp(cl)], axis=0)
        bk_end = _pieces(jnp.concatenate([b * w_rest, k * w_rest], axis=0), n1)
        st.append(dict(r=r, k=k, v=v, qa=qa, bk=bk, qa_s=qa_s, bk_end=bk_end, w_end=jnp.exp(cl_end),
                       v_p=_pieces(v, n1)))

    bases = [_mm(_pieces(st[i]["qa_s"], n1), _pieces(state[i], n1), nt=True) for i in range(nu)]
    a_ab, a_ak, a_r = [], [], []
    for i in range(nu):
        for h in range(2):
            g = _mm(_pieces(jnp.where(lo2 == (h == 0), st[i]["qa"], 0.0), ng), st[i]["bk"], nt=True)
            a_ab.append(jnp.where(strict, g[:c, :c], 0.0))
            a_ak.append(jnp.where(strict, g[:c, c:], 0.0))
            a_r.append(jnp.concatenate([jnp.where(incl, g[c:, :c], 0.0), jnp.where(incl, g[c:, c:], 0.0)], axis=1))
    rhs = [bases[j // 2][:c] + _mm(_pieces(a_ak[j], n1), st[j // 2]["v_p"]) for j in range(2 * nu)]

    t_inv = _neumann_inverses(a_ab)
    u_heads = [_mm(_pieces(t_inv[j], n1), _pieces(rhs[j], n1)) for j in range(2 * nu)]
    for i, (bi, _) in enumerate(units):
        r, k, v = st[i]["r"], st[i]["k"], st[i]["v"]
        u = jnp.where(lo, u_heads[2 * i], u_heads[2 * i + 1])
        uv = _pieces(jnp.concatenate([u, v], axis=0), n1)
        y = bases[i][c:] + jnp.where(lo, _mm(_pieces(a_r[2 * i], n1), uv), _mm(_pieces(a_r[2 * i + 1], n1), uv))
        upd = _mm(_pieces(jnp.concatenate([u.T, v.T], axis=1), n1), st[i]["bk_end"])
        state[i] = state[i] * st[i]["w_end"] + jnp.where(diag_blocks, upd, 0.0)

        mean = _seg_sum(y) * (1.0 / HEAD_DIM)
        yc = y - mean
        var = _seg_sum(yc * yc) * (1.0 / HEAD_DIM)
        out = yc * lax.rsqrt(var + RWKV_LN_EPS) * lng_ref[:, lanes[i]] + lnb_ref[:, lanes[i]]
        out = out + _seg_sum(r * k * kkr_ref[2:3, lanes[i]]) * v
        o_ref[bi, :, lanes[i]] = (out * _silu(g_ref[bi, :, lanes[i]])).astype(o_ref.dtype)
    for bi in range(u_ref.shape[0]):
        prev_u[bi] = u_ref[bi, c - 1:, :]


def _rwkv_scan(proj, log_decay, iclr, mu, kkr, ln_gain, ln_bias):
    bsz, seq, _ = proj.shape
    c = min(RWKV_CHUNK, seq)
    tri = (lax.broadcasted_iota(jnp.int32, (c, c), 0) >= lax.broadcasted_iota(jnp.int32, (c, c), 1)).astype(BF16)
    nb = RWKV_ROWS if bsz % RWKV_ROWS == 0 else 1
    tok_spec = pl.BlockSpec((nb, c, GROUP_W), lambda b, i: (b, i, 0))
    full = lambda shape: pl.BlockSpec(shape, lambda b, i: (0,) * len(shape))
    vec = lambda a: a.astype(F32).reshape(1, GROUP_W)
    return pl.pallas_call(
        _rwkv_scan_kernel,
        grid=(bsz // nb, seq // c),
        in_specs=[pl.BlockSpec((nb, c, 3 * GROUP_W), lambda b, i: (b, i, COL_D * LANES // (3 * GROUP_W))),
                  tok_spec, tok_spec,
                  pl.BlockSpec((nb, c, GROUP_W), lambda b, i: (b, i, COL_DG * LANES // GROUP_W)),
                  full((1, 3 * GROUP_W)), full((3, GROUP_W)), full((1, GROUP_W)), full((1, GROUP_W)), full((c, c))],
        out_specs=tok_spec,
        out_shape=jax.ShapeDtypeStruct((bsz, seq, GROUP_W), BF16),
        scratch_shapes=[pltpu.VMEM((nb * N_PAIRS, LANES, LANES), F32), pltpu.VMEM((nb, 1, 3 * GROUP_W), F32)],
        compiler_params=_cparams(("arbitrary", "arbitrary")),
    )(proj, log_decay, iclr, proj, mu.astype(F32)[:3 * GROUP_W].reshape(1, 3 * GROUP_W), kkr.astype(F32),
      vec(ln_gain), vec(ln_bias), tri)


def _out_kernel(a_ref, b_ref, c_ref, d_ref, w_ref, x_ref, o_ref):
    acc = x_ref[...]
    for i, m_ref in enumerate((a_ref, b_ref, c_ref, d_ref)):
        acc = acc + _dot(m_ref[...], w_ref[i * GROUP_W:(i + 1) * GROUP_W, :])
    o_ref[...] = acc


def _out_proj(groups, w_out, x2d):
    m, d = x2d.shape
    tm = min(512, m)
    g_spec = pl.BlockSpec((tm, GROUP_W), lambda i: (i, 0))
    return pl.pallas_call(
        _out_kernel,
        grid=(m // tm,),
        in_specs=[g_spec] * 4 + [pl.BlockSpec(w_out.shape, lambda i: (0, 0)),
                                 pl.BlockSpec((tm, d), lambda i: (i, 0))],
        out_specs=pl.BlockSpec((tm, d), lambda i: (i, 0)),
        out_shape=jax.ShapeDtypeStruct((m, d), F32),
        compiler_params=_cparams(("arbitrary",)),
    )(*[g.reshape(m, GROUP_W) for g in groups], w_out, x2d)


def _t5_causal_bucket(dist):
    max_exact = NUM_BUCKETS // 2
    d = jnp.maximum(dist, 1).astype(F32)
    large = max_exact + (jnp.log(d / max_exact) / math.log(MAX_DISTANCE / max_exact)
                         * (NUM_BUCKETS - max_exact)).astype(jnp.int32)
    large = jnp.minimum(large, NUM_BUCKETS - 1)
    return jnp.where(dist < max_exact, dist, large)


def _bias_tiles(rel_bias, seq, t):
    assert t >= MAX_DISTANCE
    bias_by_dist = rel_bias.astype(F32)[_t5_causal_bucket(jnp.arange(seq))] * LOG2E
    m = np.arange(2 * t)
    tiles = []
    for delta in range(3):
        dist = np.where(m < t, delta * t - m, delta * t + 2 * t - m)
        v = bias_by_dist[np.clip(dist, 0, seq - 1)].T
        skew = jnp.tile(v, (1, t))[:, :t * (2 * t - 1)].reshape(-1, t, 2 * t - 1)
        tiles.append(skew[:, :, :t])
    return jnp.stack(tiles, axis=1)


def _pack_w_in(w):
    d = w.shape[0]
    att = w[:, :12 * GROUP_W]
    cf = w[:, 12 * GROUP_W:12 * GROUP_W + 8]
    off = 12 * GROUP_W + 8
    rkv = w[:, off:off + 3 * GROUP_W]
    w_lo = w[:, off + 3 * GROUP_W:off + 3 * GROUP_W + RWKV_RANK]
    a_lo = w[:, off + 3 * GROUP_W + RWKV_RANK:off + 3 * GROUP_W + 2 * RWKV_RANK]
    dg = w[:, off + 3 * GROUP_W + 2 * RWKV_RANK:]
    zeros = lambda n: jnp.zeros((d, n), w.dtype)
    small = [cf, zeros(SMALL_WLO - cf.shape[1]), w_lo, a_lo, zeros(LANES - SMALL_ALO - RWKV_RANK)]
    return jnp.concatenate([att, rkv, dg] + small + [zeros(LANES)], axis=1).astype(BF16)


def kernel(x, norm_gain, w_in, w_out, rel_bias, qk_gain, diff_lambda, forget_bias, out_gain, rwkv_mu, rwkv_w_up,
           rwkv_w0, rwkv_a_up, rwkv_a0, rwkv_kkr, rwkv_ln_gain, rwkv_ln_bias):
    bsz, seq, d = x.shape
    tz = _bias_tiles(rel_bias, seq, min(ATT_TILE, seq))
    bias_max = jnp.max(jnp.abs(rel_bias.astype(F32))) * LOG2E
    h = x.reshape(bsz * seq, d)
    for l in range(DEPTH):
        og = out_gain[l].astype(F32)
        proj = _proj(h, norm_gain[l].astype(F32), _pack_w_in(w_in[l])).reshape(bsz, seq, N_PROJ)
        lam_init = 0.8 - 0.6 * math.exp(-0.3 * l)
        qg = qk_gain[l].astype(F32)
        bound_a = _score_bound_log2(qg[0], qg[1]) + bias_max
        bound_c = _score_bound_log2(qg[2], qg[3])
        att_a = functools.partial(_attn_a, proj, q_gain=qg[0], k_gain=qg[1], o_gain=og[:GROUP_W],
                                  dlam=diff_lambda[l].astype(F32), lam_init=lam_init)
        oa = lax.cond(bound_a <= MAX_SHIFT_LOG2, lambda: att_a(tz=tz - bound_a, bounded=True),
                      lambda: att_a(tz=tz, bounded=False))
        ob = _attn_b(proj, og[GROUP_W:2 * GROUP_W])
        att_c = functools.partial(_attn_c, proj, *_fox_cum(proj, forget_bias[l]), bound_c, qg[2], qg[3],
                                  og[2 * GROUP_W:3 * GROUP_W])
        oc = lax.cond(bound_c <= MAX_SHIFT_LOG2, lambda: att_c(bounded=True), lambda: att_c(bounded=False))
        log_decay, iclr = _rwkv_prep(proj, rwkv_mu[l], rwkv_w_up[l], rwkv_w0[l], rwkv_a_up[l], rwkv_a0[l])
        od = _rwkv_scan(proj, log_decay, iclr, rwkv_mu[l], rwkv_kkr[l],
                        rwkv_ln_gain[l].astype(F32) * og[3 * GROUP_W:], rwkv_ln_bias[l])
        h = _out_proj((oa, ob, oc, od), w_out[l].astype(BF16), h)
    return h.reshape(bsz, seq, d)
```

```python
import functools
import math

import numpy as np
import jax
import jax.numpy as jnp
from jax import lax
from jax.experimental import pallas as pl
from jax.experimental.pallas import tpu as pltpu

F32 = jnp.float32
BF16 = jnp.bfloat16

D_MODEL = 1024
DEPTH = 2
GROUP_W = 512
HEAD_DIM = 64
H_A = 4
NUM_BUCKETS = 32
MAX_DISTANCE = 128
RWKV_RANK = 32
NORM_EPS = 1e-6
RWKV_LN_EPS = 64e-5
NEG_INF = -1e30

LANES = 128
N_PAIRS = GROUP_W // LANES
ATT_SCALE = HEAD_DIM ** -0.5
LOG2E = math.log2(math.e)

COL_A, COL_B, COL_C = 0, 16, 32
COL_D = 48
COL_DG = 60
COL_SMALL = 64
N_PROJ = (COL_SMALL + 2) * LANES
PROJ_COL_STEPS = 3
SMALL_WLO, SMALL_ALO = 32, 64

ATT_TILE = 512
ATT_PAIRS = 2
MAX_SHIFT_LOG2 = 56.0
STICK_TILE = 256
STICK_BLOCK = 256
STICK_DEAD_LOG2 = -152.0
PREP_ROWS = 512
FOX_ROWS = 256
RWKV_CHUNK = 128
RWKV_ROWS = 2
RWKV_PIECES = 1
RWKV_PIECES_G = 2
VMEM_LIMIT = 56 * 1024 * 1024


def _cparams(sem):
    return pltpu.CompilerParams(dimension_semantics=sem, vmem_limit_bytes=VMEM_LIMIT)


def _dot(a, b, **kw):
    return jnp.dot(a, b, preferred_element_type=F32, **kw)


def _dot_nt(a, b, **kw):
    return lax.dot_general(a, b, (((1,), (1,)), ((), ())), preferred_element_type=F32, **kw)


def _lane_lo(shape):
    return lax.broadcasted_iota(jnp.int32, shape, len(shape) - 1) < HEAD_DIM


def _seg_sum(x):
    lo = _lane_lo(x.shape)
    s_lo = jnp.sum(jnp.where(lo, x, 0.0), axis=-1, keepdims=True)
    s_hi = jnp.sum(jnp.where(lo, 0.0, x), axis=-1, keepdims=True)
    return jnp.where(lo, s_lo, s_hi)


def _group_mean_mxu(x, group):
    n = x.shape[-1]
    same = (lax.broadcasted_iota(jnp.int32, (n, n), 0) // group) == (lax.broadcasted_iota(jnp.int32, (n, n), 1) // group)
    return _mm(_pieces(x, 3), (same.astype(BF16),)) * (1.0 / group)


def _seg_rms(x, gain, mxu=False):
    ms = _group_mean_mxu(x * x, HEAD_DIM) if mxu else _seg_sum(x * x) * (1.0 / HEAD_DIM)
    return x * lax.rsqrt(ms + NORM_EPS) * gain


def _log_sigmoid(z):
    return jnp.minimum(z, 0.0) - jnp.log1p(jnp.exp(-jnp.abs(z)))


def _silu(g):
    return g / (1.0 + jnp.exp(-g))


def _proj_kernel(x_ref, g_ref, w_ref, o_ref, h_ref):
    @pl.when(pl.program_id(1) == 0)
    def _():
        x = x_ref[...]
        ms = jnp.mean(x * x, axis=-1, keepdims=True)
        h_ref[...] = (x * lax.rsqrt(ms + NORM_EPS) * g_ref[...]).astype(BF16)

    o_ref[...] = _dot(h_ref[...], w_ref[...])


def _proj(x2d, gain, w_packed):
    m, d = x2d.shape
    n = w_packed.shape[1]
    tm = min(1024, m)
    tn = n // PROJ_COL_STEPS
    return pl.pallas_call(
        _proj_kernel,
        grid=(m // tm, n // tn),
        in_specs=[pl.BlockSpec((tm, d), lambda i, j: (i, 0)),
                  pl.BlockSpec((1, d), lambda i, j: (0, 0)),
                  pl.BlockSpec((d, tn), lambda i, j: (0, j))],
        out_specs=pl.BlockSpec((tm, tn), lambda i, j: (i, j)),
        out_shape=jax.ShapeDtypeStruct((m, n), F32),
        scratch_shapes=[pltpu.VMEM((tm, d), BF16)],
        compiler_params=_cparams(("arbitrary", "arbitrary")),
    )(x2d, gain.reshape(1, d), w_packed)


def _for_row_blocks(seq, fn):
    rows_per = min(PREP_ROWS, seq)

    def body(c, carry):
        fn(pl.ds(pl.multiple_of(c * rows_per, rows_per), rows_per), rows_per)
        return carry

    lax.fori_loop(0, seq // rows_per, body, 0)


def _head_masked(q):
    lo = _lane_lo(q.shape)
    return jnp.where(lo, q, 0.0).astype(BF16), jnp.where(lo, 0.0, q).astype(BF16)


def _online_softmax(scores, m_ref, acc_ref, values, first):
    row_max = [jnp.max(s, axis=-1, keepdims=True) for s in scores]
    m_old = [None if first else m_ref[i] for i in range(len(scores))]
    m_new = [r if first else jnp.maximum(m, r) for r, m in zip(row_max, m_old)]
    probs = [jnp.exp2(s - m).astype(BF16) for s, m in zip(scores, m_new)]
    for i, (p, mo, mn, v) in enumerate(zip(probs, m_old, m_new, values)):
        acc_ref[i] = _dot(p, v) if first else jnp.exp2(mo - mn) * acc_ref[i] + _dot(p, v)
        m_ref[i] = mn


def _shifted_softmax(scores, m_ref, acc_ref, values, first):
    for i, (s, v) in enumerate(zip(scores, values)):
        pv = _dot(jnp.exp2(s).astype(BF16), v)
        acc_ref[i] = pv if first else acc_ref[i] + pv


def _causal_tiles(qi, tile):
    def two(i, c):
        tile(2 * i, False)
        tile(2 * i + 1, False)
        return c

    def one(j, c):
        tile(j, False)
        return c

    tile(qi, True)
    lax.fori_loop(0, qi // 2, two, 0)
    lax.fori_loop(2 * (qi // 2), qi, one, 0)


def _attn_a_kernel(q_ref, k_ref, v_ref, g_ref, tz_ref, qg_ref, kg_ref, og_ref, dl_ref, o_ref, kbf, vbf, m_ref,
                   acc_ref, *, lam_init, bounded):
    softmax = _shifted_softmax if bounded else _online_softmax
    qi = pl.program_id(2)
    t = q_ref.shape[1]
    heads = range(q_ref.shape[2] // LANES)
    sl = [slice(h * LANES, (h + 1) * LANES) for h in heads]
    vsl = [slice(2 * h * LANES, 2 * (h + 1) * LANES) for h in heads]

    @pl.when(qi == 0)
    def _():
        def stage(rows, n):
            for h in heads:
                kbf[rows, sl[h]] = _seg_rms(k_ref[0, rows, sl[h]], kg_ref[...]).astype(BF16)
                vbf[rows, vsl[h]] = jnp.concatenate([v_ref[0, rows, sl[h]].astype(BF16),
                                                     jnp.ones((n, LANES), BF16)], axis=1)

        _for_row_blocks(k_ref.shape[1], stage)

    qs = [_head_masked(_seg_rms(q_ref[0, :, sl[h]], qg_ref[...]) * (ATT_SCALE * LOG2E)) for h in heads]
    row = lax.broadcasted_iota(jnp.int32, (t, t), 0)
    col = lax.broadcasted_iota(jnp.int32, (t, t), 1)

    def tile(j, diag):
        ks = pl.ds(pl.multiple_of(j * t, t), t)
        delta = jnp.minimum(qi - j, 2)
        scores = [_dot_nt(q, kbf[ks, sl[h]]) + tz_ref[h, delta] for h in heads for q in qs[h]]
        if diag:
            scores = [jnp.where(row >= col, s, NEG_INF) for s in scores]
        softmax(scores, m_ref, acc_ref, [vbf[ks, vsl[h]] for h in heads for _ in range(2)], first=diag)

    _causal_tiles(qi, tile)

    dl = dl_ref[...]
    lam = (jnp.exp(jnp.sum(dl[0:1] * dl[1:2], axis=-1, keepdims=True))
           - jnp.exp(jnp.sum(dl[2:3] * dl[3:4], axis=-1, keepdims=True)) + lam_init)
    for h in heads:
        acc0, acc1 = acc_ref[2 * h], acc_ref[2 * h + 1]
        o = acc0[:, :LANES] / acc0[:, LANES:] - lam * (acc1[:, :LANES] / acc1[:, LANES:])
        y = o * lax.rsqrt(jnp.mean(o * o, axis=-1, keepdims=True) + NORM_EPS) * og_ref[0, :, sl[h]] * (1.0 - lam_init)
        o_ref[0, :, sl[h]] = (y * _silu(g_ref[0, :, sl[h]])).astype(o_ref.dtype)


def _attn_c_kernel(q_ref, k_ref, v_ref, g_ref, f_ref, fq_ref, bound_ref, qg_ref, kg_ref, og_ref, o_ref,
                   kbf, vbf0, vbf1, m_ref, acc_ref, *, bounded):
    softmax = _shifted_softmax if bounded else _online_softmax
    qi = pl.program_id(2)
    t = q_ref.shape[1]
    pairs = range(q_ref.shape[2] // LANES)
    sl = [slice(p * LANES, (p + 1) * LANES) for p in pairs]

    @pl.when(qi == 0)
    def _():
        def stage(rows, n):
            for p in pairs:
                kbf[rows, sl[p]] = _seg_rms(k_ref[0, rows, sl[p]], kg_ref[...]).astype(BF16)
                v = v_ref[0, rows, sl[p]]
                lo = _lane_lo(v.shape)
                vbf0[rows, sl[p]] = jnp.where(lo, v, 1.0).astype(BF16)
                vbf1[rows, sl[p]] = jnp.where(lo, 1.0, v).astype(BF16)

        _for_row_blocks(k_ref.shape[1], stage)

    qs = [_head_masked(_seg_rms(q_ref[0, :, sl[p]], qg_ref[...]) * (ATT_SCALE * LOG2E)) for p in pairs]
    if bounded:
        fqs = [[fq_ref[0, p, :, h * HEAD_DIM:h * HEAD_DIM + 1] * LOG2E - bound_ref[:, :1] for h in range(2)]
               for p in pairs]
    else:
        fqs = [[0.0, 0.0] for _ in pairs]
    row = lax.broadcasted_iota(jnp.int32, (t, t), 0)
    col = lax.broadcasted_iota(jnp.int32, (t, t), 1)

    def tile(j, diag):
        ks = pl.ds(pl.multiple_of(j * t, t), t)
        scores = [_dot_nt(qs[p][h], kbf[ks, sl[p]]) + (fqs[p][h] - f_ref[0, 0, 2 * p + h:2 * p + h + 1, ks] * LOG2E)
                  for p in pairs for h in range(2)]
        if diag:
            scores = [jnp.where(row >= col, s, NEG_INF) for s in scores]
        softmax(scores, m_ref, acc_ref, [vb[ks, sl[p]] for p in pairs for vb in (vbf0, vbf1)], first=diag)

    _causal_tiles(qi, tile)
    for p in pairs:
        acc0, acc1 = acc_ref[2 * p], acc_ref[2 * p + 1]
        lo = _lane_lo(acc0.shape)
        denom = jnp.where(lo, pltpu.roll(acc0, HEAD_DIM, 1), pltpu.roll(acc1, HEAD_DIM, 1))
        o = jnp.where(lo, acc0, acc1) / denom
        o_ref[0, :, sl[p]] = (_seg_rms(o, og_ref[0, :, sl[p]], mxu=True) * _silu(g_ref[0, :, sl[p]])).astype(o_ref.dtype)


def _attn_b_kernel(q_ref, k_ref, v_ref, g_ref, u_ref, og_ref, o_ref, kbf, vbf, c_ref, acc_ref):
    qi = pl.program_id(2)
    t = q_ref.shape[1]
    blk = u_ref.shape[1]
    pairs = range(q_ref.shape[2] // LANES)
    sl = [slice(p * LANES, (p + 1) * LANES) for p in pairs]

    @pl.when(qi == 0)
    def _():
        def stage(rows, n):
            kbf[rows, :] = k_ref[0, rows, :].astype(BF16)
            vbf[rows, :] = v_ref[0, rows, :].astype(BF16)

        _for_row_blocks(k_ref.shape[1], stage)

    qs = [q for p in pairs for q in _head_masked(q_ref[0, :, sl[p]] * (ATT_SCALE * LOG2E))]
    row = lax.broadcasted_iota(jnp.int32, (t, t), 0)
    col = lax.broadcasted_iota(jnp.int32, (t, t), 1)
    umat = u_ref[...]

    def tile(j, diag):
        ks = pl.ds(pl.multiple_of(j * t, t), t)
        zs = [_dot_nt(q, kbf[ks, sl[i // 2]]) for i, q in enumerate(qs)]
        lbs = [jnp.minimum(z, 0.0) - jnp.log2(1.0 + jnp.exp2(-jnp.abs(z))) for z in zs]
        l1s = [lb - z for lb, z in zip(lbs, zs)]
        if diag:
            l1s = [jnp.where(row > col, l1, 0.0) for l1 in l1s]
        tails = [jnp.zeros((t, 1), F32) if diag else c_ref[i] for i in range(len(qs))]
        afters = [[] for _ in qs]
        for b in reversed(range(t // blk)):
            for i, l1 in enumerate(l1s):
                x = l1[:, b * blk:(b + 1) * blk]
                hi = x.astype(BF16)
                lo = (x - hi.astype(F32)).astype(BF16)
                afters[i].insert(0, _dot(jnp.concatenate([hi, lo], axis=1), umat) + tails[i])
                tails[i] = tails[i] + jnp.sum(x, axis=-1, keepdims=True)
        for i, (lb, after, c_new) in enumerate(zip(lbs, afters, tails)):
            a = jnp.exp2(lb + (after[0] if len(after) == 1 else jnp.concatenate(after, axis=1)))
            if diag:
                a = jnp.where(row > col, a, 0.0)
            av = _dot(a.astype(BF16), vbf[ks, sl[i // 2]])
            acc_ref[i] = av if diag else acc_ref[i] + av
            c_ref[i] = c_new
        return functools.reduce(jnp.maximum, [jnp.max(c) for c in tails])

    def alive(state):
        return jnp.logical_and(state[0] < qi, state[1] > STICK_DEAD_LOG2)

    lax.while_loop(alive, lambda st: (st[0] + 1, tile(qi - 1 - st[0], False)), (jnp.int32(0), tile(qi, True)))
    for p in pairs:
        o = jnp.where(_lane_lo((t, LANES)), acc_ref[2 * p], acc_ref[2 * p + 1])
        o_ref[0, :, sl[p]] = (_seg_rms(o, og_ref[0, :, sl[p]]) * _silu(g_ref[0, :, sl[p]])).astype(o_ref.dtype)


def _attn_specs(seq, t, col, pp):
    w = pp * LANES
    q_spec = pl.BlockSpec((1, t, w), lambda b, p, i: (b, i, col // pp + p))
    k_spec = pl.BlockSpec((1, seq, w), lambda b, p, i: (b, 0, (col + N_PAIRS) // pp + p))
    v_spec = pl.BlockSpec((1, seq, w), lambda b, p, i: (b, 0, (col + 2 * N_PAIRS) // pp + p))
    g_spec = pl.BlockSpec((1, t, w), lambda b, p, i: (b, i, (col + 3 * N_PAIRS) // pp + p))
    return [q_spec, k_spec, v_spec, g_spec]


def _pair_spec(pp):
    return pl.BlockSpec((1, 1, pp * LANES), lambda b, p, i: (p, 0, 0))


def _const_spec(shape):
    return pl.BlockSpec(shape, lambda b, p, i: (0,) * len(shape))


def _attn_call(body, proj, col, extra_specs, extra_args, pp, acc_lanes, tile=ATT_TILE, v_lanes=(LANES,)):
    bsz, seq, _ = proj.shape
    t = min(tile, seq)
    w = pp * LANES
    return pl.pallas_call(
        body,
        grid=(bsz, N_PAIRS // pp, seq // t),
        in_specs=_attn_specs(seq, t, col, pp) + extra_specs,
        out_specs=pl.BlockSpec((1, t, w), lambda b, p, i: (b, i, p)),
        out_shape=jax.ShapeDtypeStruct((bsz, seq, GROUP_W), BF16),
        scratch_shapes=[pltpu.VMEM((seq, w), BF16)] + [pltpu.VMEM((seq, pp * n), BF16) for n in v_lanes]
        + [pltpu.VMEM((2 * pp, t, 1), F32), pltpu.VMEM((2 * pp, t, acc_lanes), F32)],
        compiler_params=_cparams(("arbitrary", "arbitrary", "arbitrary")),
    )(proj, proj, proj, proj, *extra_args)


def _pair_gain(g64):
    return jnp.tile(g64.astype(F32), 2).reshape(1, LANES)


def _score_bound_log2(q_gain, k_gain):
    return (HEAD_DIM * ATT_SCALE * LOG2E * 1.02) * jnp.max(jnp.abs(q_gain)) * jnp.max(jnp.abs(k_gain))


def _attn_a(proj, tz, q_gain, k_gain, o_gain, dlam, lam_init, bounded):
    t = tz.shape[-1]
    pp = ATT_PAIRS
    specs = [pl.BlockSpec((pp, 3, t, t), lambda b, p, i: (p, 0, 0, 0)),
             _const_spec((1, LANES)), _const_spec((1, LANES)), _pair_spec(pp), _const_spec((4, HEAD_DIM))]
    args = (tz, _pair_gain(q_gain), _pair_gain(k_gain), o_gain.reshape(N_PAIRS // pp, 1, pp * LANES), dlam)
    return _attn_call(functools.partial(_attn_a_kernel, lam_init=lam_init, bounded=bounded), proj, COL_A, specs,
                      args, pp=pp, v_lanes=(2 * LANES,), acc_lanes=2 * LANES)


def _attn_b(proj, o_gain):
    t = min(STICK_BLOCK, STICK_TILE, proj.shape[1])
    umat = (lax.broadcasted_iota(jnp.int32, (t, t), 0) > lax.broadcasted_iota(jnp.int32, (t, t), 1)).astype(BF16)
    umat = jnp.concatenate([umat, umat], axis=0)
    pp = ATT_PAIRS
    specs = [_const_spec((2 * t, t)), _pair_spec(pp)]
    return _attn_call(_attn_b_kernel, proj, COL_B, specs, (umat, o_gain.reshape(N_PAIRS // pp, 1, pp * LANES)),
                      pp=pp, tile=STICK_TILE, acc_lanes=LANES)


def _attn_c(proj, cum_rows, cum_cols, bound, q_gain, k_gain, o_gain, bounded):
    seq = proj.shape[1]
    t = min(ATT_TILE, seq)
    pp = ATT_PAIRS
    specs = [pl.BlockSpec((1, 1, 2 * pp, seq), lambda b, p, i: (b, p, 0, 0)),
             pl.BlockSpec((1, pp, t, LANES), lambda b, p, i: (b, p, i, 0)),
             _const_spec((1, LANES)), _const_spec((1, LANES)), _const_spec((1, LANES)), _pair_spec(pp)]
    args = (cum_rows.reshape(cum_rows.shape[0], N_PAIRS // pp, 2 * pp, seq), cum_cols,
            jnp.full((1, LANES), bound, F32), _pair_gain(q_gain), _pair_gain(k_gain),
            o_gain.reshape(N_PAIRS // pp, 1, pp * LANES))
    return _attn_call(functools.partial(_attn_c_kernel, bounded=bounded), proj, COL_C, specs, args,
                      pp=pp, v_lanes=(LANES, LANES), acc_lanes=LANES)


def _fox_cum_kernel(s_ref, fb_ref, tri_ref, rows_ref, cols_ref):
    seq = s_ref.shape[1]
    rows_per = tri_ref.shape[0]
    carry = jnp.zeros((1, LANES), F32)
    for c in range(seq // rows_per):
        rows = slice(c * rows_per, (c + 1) * rows_per)
        log_f = _log_sigmoid(s_ref[0, rows, :] + fb_ref[...])
        cum = _mm((tri_ref[...],), _pieces(log_f, 3)) + carry
        carry = cum[rows_per - 1:, :]
        rows_ref[0, :, rows] = cum.T[:8, :]
        lo = _lane_lo(cum.shape)
        for p in range(N_PAIRS):
            cols_ref[0, p, rows, :] = jnp.where(lo, cum[:, 2 * p:2 * p + 1], cum[:, 2 * p + 1:2 * p + 2])


def _fox_cum(proj, forget_bias):
    bsz, seq, _ = proj.shape
    rows_per = min(FOX_ROWS, seq)
    tri = (lax.broadcasted_iota(jnp.int32, (rows_per, rows_per), 0)
           >= lax.broadcasted_iota(jnp.int32, (rows_per, rows_per), 1)).astype(BF16)
    fb = jnp.pad(forget_bias.astype(F32), (0, LANES - forget_bias.shape[0])).reshape(1, LANES)
    cum_rows, cum_cols = pl.pallas_call(
        _fox_cum_kernel,
        grid=(bsz,),
        in_specs=[pl.BlockSpec((1, seq, LANES), lambda b: (b, 0, COL_SMALL)),
                  pl.BlockSpec((1, LANES), lambda b: (0, 0)),
                  pl.BlockSpec((rows_per, rows_per), lambda b: (0, 0))],
        out_specs=[pl.BlockSpec((1, 8, seq), lambda b: (b, 0, 0)),
                   pl.BlockSpec((1, N_PAIRS, seq, LANES), lambda b: (b, 0, 0, 0))],
        out_shape=[jax.ShapeDtypeStruct((bsz, 8, seq), F32),
                   jax.ShapeDtypeStruct((bsz, N_PAIRS, seq, LANES), F32)],
        compiler_params=_cparams(("arbitrary",)),
    )(proj, fb, tri)
    return cum_rows.reshape(bsz, N_PAIRS, 2, seq), cum_cols


def _token_shift(x, prev_row, mu):
    first = lax.broadcasted_iota(jnp.int32, x.shape, 0) == 0
    x_prev = jnp.where(first, prev_row, pltpu.roll(x, 1, 0))
    return x + (x_prev - x) * mu


def _rwkv_prep_kernel(s_ref, mus_ref, wup_ref, w0_ref, aup_ref, a0_ref, lw_o, a_o, prev_s):
    @pl.when(pl.program_id(1) == 0)
    def _():
        prev_s[...] = jnp.zeros_like(prev_s)

    raw = s_ref[0]
    sm = _token_shift(raw, prev_s[...], mus_ref[...])
    prev_s[...] = raw[raw.shape[0] - 1:, :]
    w_pre = w0_ref[...] + _mm(_pieces(jnp.tanh(sm), 2), _pieces(wup_ref[...], 2))
    lw_o[0] = (-math.exp(-0.5)) / (1.0 + jnp.exp(-w_pre))
    a_o[0] = 1.0 / (1.0 + jnp.exp(-(a0_ref[...] + _mm(_pieces(sm, 2), _pieces(aup_ref[...], 2)))))


def _rwkv_prep(proj, mu, w_up, w0, a_up, a0):
    bsz, seq, _ = proj.shape
    ts = min(PREP_ROWS, seq)
    mu = mu.astype(F32)
    zeros = lambda n: jnp.zeros((n,), F32)
    mu_s = jnp.concatenate([zeros(SMALL_WLO), mu[3 * GROUP_W:3 * GROUP_W + RWKV_RANK],
                            mu[3 * GROUP_W + RWKV_RANK:], zeros(LANES - SMALL_ALO - RWKV_RANK)]).reshape(1, LANES)
    wup = jnp.pad(w_up.astype(F32), ((SMALL_WLO, LANES - SMALL_WLO - RWKV_RANK), (0, 0)))
    aup = jnp.pad(a_up.astype(F32), ((SMALL_ALO, LANES - SMALL_ALO - RWKV_RANK), (0, 0)))
    vec = lambda a: a.astype(F32).reshape(1, GROUP_W)
    full = lambda shape: pl.BlockSpec(shape, lambda b, i: (0,) * len(shape))
    out_spec = pl.BlockSpec((1, ts, GROUP_W), lambda b, i: (b, i, 0))
    out_sds = jax.ShapeDtypeStruct((bsz, seq, GROUP_W), F32)
    return pl.pallas_call(
        _rwkv_prep_kernel,
        grid=(bsz, seq // ts),
        in_specs=[pl.BlockSpec((1, ts, LANES), lambda b, i: (b, i, COL_SMALL)),
                  full((1, LANES)), full((LANES, GROUP_W)), full((1, GROUP_W)),
                  full((LANES, GROUP_W)), full((1, GROUP_W))],
        out_specs=[out_spec] * 2,
        out_shape=[out_sds] * 2,
        scratch_shapes=[pltpu.VMEM((1, LANES), F32)],
        compiler_params=_cparams(("arbitrary", "arbitrary")),
    )(proj, mu_s, wup, vec(w0), aup, vec(a0))


def _pieces(x, n):
    out = []
    for i in range(n):
        p = x.astype(BF16)
        out.append(p)
        if i + 1 < n:
            x = x - p.astype(F32)
    return tuple(out)


def _mm(a, b, nt=False):
    n = max(len(a), len(b))
    dot = _dot_nt if nt else _dot
    acc = None
    for i, ai in enumerate(a):
        for j, bj in enumerate(b):
            if i + j < n:
                term = dot(ai, bj)
                acc = term if acc is None else acc + term
    return acc


def _neumann_inverses(mats):
    n = mats[0].shape[0]
    eye = (lax.broadcasted_iota(jnp.int32, (n, n), 0) == lax.broadcasted_iota(jnp.int32, (n, n), 1)).astype(F32)
    ps = [eye + a for a in mats]
    xs = [_pieces(a, RWKV_PIECES) for a in mats]
    for _ in range(int(math.log2(n)) - 1):
        xs = [_pieces(_mm(x, x), RWKV_PIECES) for x in xs]
        ps = [p + _mm(_pieces(p, RWKV_PIECES), x) for p, x in zip(ps, xs)]
    return ps


def _rwkv_scan_kernel(u_ref, lw_ref, a_ref, g_ref, mu_ref, kkr_ref, lng_ref, lnb_ref, tri_ref, o_ref, state, prev_u):
    c = u_ref.shape[1]
    n1, ng = RWKV_PIECES, RWKV_PIECES_G

    @pl.when(pl.program_id(1) == 0)
    def _():
        state[...] = jnp.zeros_like(state)
        prev_u[...] = jnp.zeros_like(prev_u)

    row = lax.broadcasted_iota(jnp.int32, (c, c), 0)
    col = lax.broadcasted_iota(jnp.int32, (c, c), 1)
    strict, incl = row > col, row >= col
    lo = _lane_lo((c, LANES))
    lo2 = _lane_lo((2 * c, LANES))
    diag_blocks = ((lax.broadcasted_iota(jnp.int32, (LANES, LANES), 0) < HEAD_DIM)
                   == (lax.broadcasted_iota(jnp.int32, (LANES, LANES), 1) < HEAD_DIM))
    tri = (tri_ref[...],)
    units = [(bi, p) for bi in range(u_ref.shape[0]) for p in range(N_PAIRS)]
    lanes = [slice(p * LANES, (p + 1) * LANES) for _, p in units]
    nu = len(units)

    st = []
    for i, (bi, p) in enumerate(units):
        r, k, v = (_token_shift(u_ref[bi, :, sec], prev_u[bi, :, sec], mu_ref[:, sec])
                   for sec in (slice(j * GROUP_W + p * LANES, j * GROUP_W + (p + 1) * LANES) for j in range(3)))
        lw, a = lw_ref[bi, :, lanes[i]], a_ref[bi, :, lanes[i]]
        kk = k * kkr_ref[0:1, lanes[i]]
        kk = kk / jnp.maximum(jnp.sqrt(_seg_sum(kk * kk)), 1e-12)
        k = k * (1.0 + (a - 1.0) * kkr_ref[1:2, lanes[i]])
        aa, b = -kk, kk * a
        cl = _mm(tri, _pieces(lw, 3))
        cl_end = cl[c - 1:c, :]
        cl_mid = cl[c // 2 - 1:c // 2, :]
        w_inv = jnp.exp(cl_mid - cl)
        w_rest = jnp.exp(cl_end - cl)
        qa = jnp.concatenate([aa * jnp.exp(cl - lw - cl_mid), r * jnp.exp(cl - cl_mid)], axis=0)
        bk = _pieces(jnp.concatenate([b * w_inv, k * w_inv], axis=0), ng)
        qa_s = jnp.concatenate([aa * jnp.exp(cl - lw), r * jnp.exp(cl)], axis=0)
        bk_end = _pieces(jnp.concatenate([b * w_rest, k * w_rest], axis=0), n1)
        st.append(dict(r=r, k=k, v=v, qa=qa, bk=bk, qa_s=qa_s, bk_end=bk_end, w_end=jnp.exp(cl_end),
                       v_p=_pieces(v, n1)))

    bases = [_mm(_pieces(st[i]["qa_s"], n1), _pieces(state[i], n1), nt=True) for i in range(nu)]
    a_ab, a_ak, a_r = [], [], []
    for i in range(nu):
        for h in range(2):
            qa_h = jnp.where(lo2 == (h == 0), st[i]["qa"], 0.0)
            g_a = _mm(_pieces(qa_h[:c], n1), st[i]["bk"][:n1], nt=True)
            g_r = _mm(_pieces(qa_h[c:], ng), st[i]["bk"], nt=True)
            a_ab.append(jnp.where(strict, g_a[:, :c], 0.0))
            a_ak.append(jnp.where(strict, g_a[:, c:], 0.0))
            a_r.append(jnp.concatenate([jnp.where(incl, g_r[:, :c], 0.0), jnp.where(incl, g_r[:, c:], 0.0)], axis=1))
    rhs = [bases[j // 2][:c] + _mm(_pieces(a_ak[j], n1), st[j // 2]["v_p"]) for j in range(2 * nu)]

    t_inv = _neumann_inverses(a_ab)
    u_heads = [_mm(_pieces(t_inv[j], n1), _pieces(rhs[j], n1)) for j in range(2 * nu)]
    for i, (bi, _) in enumerate(units):
        r, k, v = st[i]["r"], st[i]["k"], st[i]["v"]
        u = jnp.where(lo, u_heads[2 * i], u_heads[2 * i + 1])
        uv = _pieces(jnp.concatenate([u, v], axis=0), n1)
        y = bases[i][c:] + jnp.where(lo, _mm(_pieces(a_r[2 * i], n1), uv), _mm(_pieces(a_r[2 * i + 1], n1), uv))
        upd = _mm(_pieces(jnp.concatenate([u.T, v.T], axis=1), n1), st[i]["bk_end"])
        state[i] = state[i] * st[i]["w_end"] + jnp.where(diag_blocks, upd, 0.0)

        mean = _seg_sum(y) * (1.0 / HEAD_DIM)
        yc = y - mean
        var = _seg_sum(yc * yc) * (1.0 / HEAD_DIM)
        out = yc * lax.rsqrt(var + RWKV_LN_EPS) * lng_ref[:, lanes[i]] + lnb_ref[:, lanes[i]]
        out = out + _seg_sum(r * k * kkr_ref[2:3, lanes[i]]) * v
        o_ref[bi, :, lanes[i]] = (out * _silu(g_ref[bi, :, lanes[i]])).astype(o_ref.dtype)
    for bi in range(u_ref.shape[0]):
        prev_u[bi] = u_ref[bi, c - 1:, :]


def _rwkv_scan(proj, log_decay, iclr, mu, kkr, ln_gain, ln_bias):
    bsz, seq, _ = proj.shape
    c = min(RWKV_CHUNK, seq)
    tri = (lax.broadcasted_iota(jnp.int32, (c, c), 0) >= lax.broadcasted_iota(jnp.int32, (c, c), 1)).astype(BF16)
    nb = RWKV_ROWS if bsz % RWKV_ROWS == 0 else 1
    tok_spec = pl.BlockSpec((nb, c, GROUP_W), lambda b, i: (b, i, 0))
    full = lambda shape: pl.BlockSpec(shape, lambda b, i: (0,) * len(shape))
    vec = lambda a: a.astype(F32).reshape(1, GROUP_W)
    return pl.pallas_call(
        _rwkv_scan_kernel,
        grid=(bsz // nb, seq // c),
        in_specs=[pl.BlockSpec((nb, c, 3 * GROUP_W), lambda b, i: (b, i, COL_D * LANES // (3 * GROUP_W))),
                  tok_spec, tok_spec,
                  pl.BlockSpec((nb, c, GROUP_W), lambda b, i: (b, i, COL_DG * LANES // GROUP_W)),
                  full((1, 3 * GROUP_W)), full((3, GROUP_W)), full((1, GROUP_W)), full((1, GROUP_W)), full((c, c))],
        out_specs=tok_spec,
        out_shape=jax.ShapeDtypeStruct((bsz, seq, GROUP_W), BF16),
        scratch_shapes=[pltpu.VMEM((nb * N_PAIRS, LANES, LANES), F32), pltpu.VMEM((nb, 1, 3 * GROUP_W), F32)],
        compiler_params=_cparams(("arbitrary", "arbitrary")),
    )(proj, log_decay, iclr, proj, mu.astype(F32)[:3 * GROUP_W].reshape(1, 3 * GROUP_W), kkr.astype(F32),
      vec(ln_gain), vec(ln_bias), tri)


def _out_kernel(a_ref, b_ref, c_ref, d_ref, w_ref, x_ref, o_ref):
    acc = x_ref[...]
    for i, m_ref in enumerate((a_ref, b_ref, c_ref, d_ref)):
        acc = acc + _dot(m_ref[...], w_ref[i * GROUP_W:(i + 1) * GROUP_W, :])
    o_ref[...] = acc


def _out_proj(groups, w_out, x2d):
    m, d = x2d.shape
    tm = min(512, m)
    g_spec = pl.BlockSpec((tm, GROUP_W), lambda i: (i, 0))
    return pl.pallas_call(
        _out_kernel,
        grid=(m // tm,),
        in_specs=[g_spec] * 4 + [pl.BlockSpec(w_out.shape, lambda i: (0, 0)),
                                 pl.BlockSpec((tm, d), lambda i: (i, 0))],
        out_specs=pl.BlockSpec((tm, d), lambda i: (i, 0)),
        out_shape=jax.ShapeDtypeStruct((m, d), F32),
        compiler_params=_cparams(("arbitrary",)),
    )(*[g.reshape(m, GROUP_W) for g in groups], w_out, x2d)


def _t5_causal_bucket(dist):
    max_exact = NUM_BUCKETS // 2
    d = jnp.maximum(dist, 1).astype(F32)
    large = max_exact + (jnp.log(d / max_exact) / math.log(MAX_DISTANCE / max_exact)
                         * (NUM_BUCKETS - max_exact)).astype(jnp.int32)
    large = jnp.minimum(large, NUM_BUCKETS - 1)
    return jnp.where(dist < max_exact, dist, large)


def _bias_tiles(rel_bias, seq, t):
    assert t >= MAX_DISTANCE
    bias_by_dist = rel_bias.astype(F32)[_t5_causal_bucket(jnp.arange(seq))] * LOG2E
    m = np.arange(2 * t)
    tiles = []
    for delta in range(3):
        dist = np.where(m < t, delta * t - m, delta * t + 2 * t - m)
        v = bias_by_dist[np.clip(dist, 0, seq - 1)].T
        skew = jnp.tile(v, (1, t))[:, :t * (2 * t - 1)].reshape(-1, t, 2 * t - 1)
        tiles.append(skew[:, :, :t])
    return jnp.stack(tiles, axis=1)


def _pack_w_in(w):
    d = w.shape[0]
    att = w[:, :12 * GROUP_W]
    cf = w[:, 12 * GROUP_W:12 * GROUP_W + 8]
    off = 12 * GROUP_W + 8
    rkv = w[:, off:off + 3 * GROUP_W]
    w_lo = w[:, off + 3 * GROUP_W:off + 3 * GROUP_W + RWKV_RANK]
    a_lo = w[:, off + 3 * GROUP_W + RWKV_RANK:off + 3 * GROUP_W + 2 * RWKV_RANK]
    dg = w[:, off + 3 * GROUP_W + 2 * RWKV_RANK:]
    zeros = lambda n: jnp.zeros((d, n), w.dtype)
    small = [cf, zeros(SMALL_WLO - cf.shape[1]), w_lo, a_lo, zeros(LANES - SMALL_ALO - RWKV_RANK)]
    return jnp.concatenate([att, rkv, dg] + small + [zeros(LANES)], axis=1).astype(BF16)


def kernel(x, norm_gain, w_in, w_out, rel_bias, qk_gain, diff_lambda, forget_bias, out_gain, rwkv_mu, rwkv_w_up,
           rwkv_w0, rwkv_a_up, rwkv_a0, rwkv_kkr, rwkv_ln_gain, rwkv_ln_bias):
    bsz, seq, d = x.shape
    tz = _bias_tiles(rel_bias, seq, min(ATT_TILE, seq))
    bias_max = jnp.max(jnp.abs(rel_bias.astype(F32))) * LOG2E
    h = x.reshape(bsz * seq, d)
    for l in range(DEPTH):
        og = out_gain[l].astype(F32)
        proj = _proj(h, norm_gain[l].astype(F32), _pack_w_in(w_in[l])).reshape(bsz, seq, N_PROJ)
        lam_init = 0.8 - 0.6 * math.exp(-0.3 * l)
        qg = qk_gain[l].astype(F32)
        bound_a = _score_bound_log2(qg[0], qg[1]) + bias_max
        bound_c = _score_bound_log2(qg[2], qg[3])
        att_a = functools.partial(_attn_a, proj, q_gain=qg[0], k_gain=qg[1], o_gain=og[:GROUP_W],
                                  dlam=diff_lambda[l].astype(F32), lam_init=lam_init)
        oa = lax.cond(bound_a <= MAX_SHIFT_LOG2, lambda: att_a(tz=tz - bound_a, bounded=True),
                      lambda: att_a(tz=tz, bounded=False))
        ob = _attn_b(proj, og[GROUP_W:2 * GROUP_W])
        att_c = functools.partial(_attn_c, proj, *_fox_cum(proj, forget_bias[l]), bound_c, qg[2], qg[3],
                                  og[2 * GROUP_W:3 * GROUP_W])
        oc = lax.cond(bound_c <= MAX_SHIFT_LOG2, lambda: att_c(bounded=True), lambda: att_c(bounded=False))
        log_decay, iclr = _rwkv_prep(proj, rwkv_mu[l], rwkv_w_up[l], rwkv_w0[l], rwkv_a_up[l], rwkv_a0[l])
        od = _rwkv_scan(proj, log_decay, iclr, rwkv_mu[l], rwkv_kkr[l],
                        rwkv_ln_gain[l].astype(F32) * og[3 * GROUP_W:], rwkv_ln_bias[l])
        h = _out_proj((oa, ob, oc, od), w_out[l].astype(BF16), h)
    return h.reshape(bsz, seq, d)
```

```python
import functools
import math

import numpy as np
import jax
import jax.numpy as jnp
from jax import lax
from jax.experimental import pallas as pl
from jax.experimental.pallas import tpu as pltpu

F32 = jnp.float32
BF16 = jnp.bfloat16

DEPTH = 2
GROUP_W = 512
HEAD_DIM = 64
NUM_BUCKETS = 32
MAX_DISTANCE = 128
RWKV_RANK = 32
NORM_EPS = 1e-6
RWKV_LN_EPS = 64e-5
NEG_INF = -1e30

LANES = 128
N_PAIRS = GROUP_W // LANES
ATT_SCALE = HEAD_DIM ** -0.5
LOG2E = math.log2(math.e)

COL_A, COL_B, COL_C = 0, 16, 32
COL_D = 48
COL_DG = 60
COL_SMALL = 64
N_PROJ = (COL_SMALL + 2) * LANES
PROJ_COL_STEPS = 3
SMALL_WLO, SMALL_ALO = 32, 64

ATT_TILE = 512
ATT_PAIRS = 2
MAX_SHIFT_LOG2 = 56.0
STICK_TILE = 256
STICK_BLOCK = 256
STICK_DEAD_LOG2 = -152.0
PREP_ROWS = 512
FOX_ROWS = 256
RWKV_CHUNK = 128
RWKV_ROWS = 2
RWKV_PIECES = 1
RWKV_PIECES_G = 2
VMEM_LIMIT = 56 * 1024 * 1024


def _cparams(sem):
    return pltpu.CompilerParams(dimension_semantics=sem, vmem_limit_bytes=VMEM_LIMIT)


def _dot(a, b):
    return jnp.dot(a, b, preferred_element_type=F32)


def _dot_nt(a, b):
    return lax.dot_general(a, b, (((1,), (1,)), ((), ())), preferred_element_type=F32)


def _lane_lo(shape):
    return lax.broadcasted_iota(jnp.int32, shape, len(shape) - 1) < HEAD_DIM


def _seg_sum(x):
    lo = _lane_lo(x.shape)
    s_lo = jnp.sum(jnp.where(lo, x, 0.0), axis=-1, keepdims=True)
    s_hi = jnp.sum(jnp.where(lo, 0.0, x), axis=-1, keepdims=True)
    return jnp.where(lo, s_lo, s_hi)


def _group_mean_mxu(x, group):
    n = x.shape[-1]
    same = (lax.broadcasted_iota(jnp.int32, (n, n), 0) // group) == (lax.broadcasted_iota(jnp.int32, (n, n), 1) // group)
    return _mm(_pieces(x, 3), (same.astype(BF16),)) * (1.0 / group)


def _seg_rms(x, gain, mxu=False):
    ms = _group_mean_mxu(x * x, HEAD_DIM) if mxu else _seg_sum(x * x) * (1.0 / HEAD_DIM)
    return x * lax.rsqrt(ms + NORM_EPS) * gain


def _log_sigmoid(z):
    return jnp.minimum(z, 0.0) - jnp.log1p(jnp.exp(-jnp.abs(z)))


def _silu(g):
    return g / (1.0 + jnp.exp(-g))


def _proj_kernel(x_ref, g_ref, w_ref, o_ref, h_ref):
    @pl.when(pl.program_id(1) == 0)
    def _():
        x = x_ref[...]
        ms = jnp.mean(x * x, axis=-1, keepdims=True)
        h_ref[...] = (x * lax.rsqrt(ms + NORM_EPS) * g_ref[...]).astype(BF16)

    o_ref[...] = _dot(h_ref[...], w_ref[...])


def _proj(x2d, gain, w_packed):
    m, d = x2d.shape
    n = w_packed.shape[1]
    tm = min(1024, m)
    tn = n // PROJ_COL_STEPS
    return pl.pallas_call(
        _proj_kernel,
        grid=(m // tm, n // tn),
        in_specs=[pl.BlockSpec((tm, d), lambda i, j: (i, 0)),
                  pl.BlockSpec((1, d), lambda i, j: (0, 0)),
                  pl.BlockSpec((d, tn), lambda i, j: (0, j))],
        out_specs=pl.BlockSpec((tm, tn), lambda i, j: (i, j)),
        out_shape=jax.ShapeDtypeStruct((m, n), F32),
        scratch_shapes=[pltpu.VMEM((tm, d), BF16)],
        compiler_params=_cparams(("arbitrary", "arbitrary")),
    )(x2d, gain.reshape(1, d), w_packed)


def _for_row_blocks(seq, fn):
    rows_per = min(PREP_ROWS, seq)

    def body(c, carry):
        fn(pl.ds(pl.multiple_of(c * rows_per, rows_per), rows_per), rows_per)
        return carry

    lax.fori_loop(0, seq // rows_per, body, 0)


def _head_masked(q):
    lo = _lane_lo(q.shape)
    return jnp.where(lo, q, 0.0).astype(BF16), jnp.where(lo, 0.0, q).astype(BF16)


def _online_softmax(scores, m_ref, acc_ref, values, first):
    row_max = [jnp.max(s, axis=-1, keepdims=True) for s in scores]
    m_old = [None if first else m_ref[i] for i in range(len(scores))]
    m_new = [r if first else jnp.maximum(m, r) for r, m in zip(row_max, m_old)]
    probs = [jnp.exp2(s - m).astype(BF16) for s, m in zip(scores, m_new)]
    for i, (p, mo, mn, v) in enumerate(zip(probs, m_old, m_new, values)):
        acc_ref[i] = _dot(p, v) if first else jnp.exp2(mo - mn) * acc_ref[i] + _dot(p, v)
        m_ref[i] = mn


def _shifted_softmax(scores, m_ref, acc_ref, values, first):
    for i, (s, v) in enumerate(zip(scores, values)):
        pv = _dot(jnp.exp2(s).astype(BF16), v)
        acc_ref[i] = pv if first else acc_ref[i] + pv


def _causal_tiles(qi, tile):
    def two(i, c):
        tile(2 * i, False)
        tile(2 * i + 1, False)
        return c

    def one(j, c):
        tile(j, False)
        return c

    tile(qi, True)
    lax.fori_loop(0, qi // 2, two, 0)
    lax.fori_loop(2 * (qi // 2), qi, one, 0)


def _attn_a_kernel(q_ref, k_ref, v_ref, g_ref, tz_ref, qg_ref, kg_ref, og_ref, dl_ref, o_ref, kbf, vbf, m_ref,
                   acc_ref, *, lam_init, bounded):
    softmax = _shifted_softmax if bounded else _online_softmax
    qi = pl.program_id(2)
    t = q_ref.shape[1]
    heads = range(q_ref.shape[2] // LANES)
    sl = [slice(h * LANES, (h + 1) * LANES) for h in heads]
    vsl = [slice(2 * h * LANES, 2 * (h + 1) * LANES) for h in heads]

    @pl.when(qi == 0)
    def _():
        def stage(rows, n):
            for h in heads:
                kbf[rows, sl[h]] = _seg_rms(k_ref[0, rows, sl[h]], kg_ref[...]).astype(BF16)
                vbf[rows, vsl[h]] = jnp.concatenate([v_ref[0, rows, sl[h]].astype(BF16),
                                                     jnp.ones((n, LANES), BF16)], axis=1)

        _for_row_blocks(k_ref.shape[1], stage)

    qs = [_head_masked(_seg_rms(q_ref[0, :, sl[h]], qg_ref[...]) * (ATT_SCALE * LOG2E)) for h in heads]
    row = lax.broadcasted_iota(jnp.int32, (t, t), 0)
    col = lax.broadcasted_iota(jnp.int32, (t, t), 1)

    def tile(j, diag):
        ks = pl.ds(pl.multiple_of(j * t, t), t)
        delta = jnp.minimum(qi - j, 2)
        scores = [_dot_nt(q, kbf[ks, sl[h]]) + tz_ref[h, delta] for h in heads for q in qs[h]]
        if diag:
            scores = [jnp.where(row >= col, s, NEG_INF) for s in scores]
        softmax(scores, m_ref, acc_ref, [vbf[ks, vsl[h]] for h in heads for _ in range(2)], first=diag)

    _causal_tiles(qi, tile)

    dl = dl_ref[...]
    lam = (jnp.exp(jnp.sum(dl[0:1] * dl[1:2], axis=-1, keepdims=True))
           - jnp.exp(jnp.sum(dl[2:3] * dl[3:4], axis=-1, keepdims=True)) + lam_init)
    for h in heads:
        acc0, acc1 = acc_ref[2 * h], acc_ref[2 * h + 1]
        o = acc0[:, :LANES] / acc0[:, LANES:] - lam * (acc1[:, :LANES] / acc1[:, LANES:])
        y = o * lax.rsqrt(jnp.mean(o * o, axis=-1, keepdims=True) + NORM_EPS) * og_ref[0, :, sl[h]]
        o_ref[0, :, sl[h]] = (y * _silu(g_ref[0, :, sl[h]])).astype(o_ref.dtype)


def _attn_c_kernel(q_ref, k_ref, v_ref, g_ref, f_ref, fq_ref, bound_ref, qg_ref, kg_ref, og_ref, o_ref,
                   kbf, vbf0, vbf1, m_ref, acc_ref, *, bounded):
    softmax = _shifted_softmax if bounded else _online_softmax
    qi = pl.program_id(2)
    t = q_ref.shape[1]
    pairs = range(q_ref.shape[2] // LANES)
    sl = [slice(p * LANES, (p + 1) * LANES) for p in pairs]

    @pl.when(qi == 0)
    def _():
        def stage(rows, n):
            for p in pairs:
                kbf[rows, sl[p]] = _seg_rms(k_ref[0, rows, sl[p]], kg_ref[...]).astype(BF16)
                v = v_ref[0, rows, sl[p]]
                lo = _lane_lo(v.shape)
                vbf0[rows, sl[p]] = jnp.where(lo, v, 1.0).astype(BF16)
                vbf1[rows, sl[p]] = jnp.where(lo, 1.0, v).astype(BF16)

        _for_row_blocks(k_ref.shape[1], stage)

    qs = [_head_masked(_seg_rms(q_ref[0, :, sl[p]], qg_ref[...]) * (ATT_SCALE * LOG2E)) for p in pairs]
    if bounded:
        fqs = [[fq_ref[0, p, :, h * HEAD_DIM:h * HEAD_DIM + 1] * LOG2E - bound_ref[:, :1] for h in range(2)]
               for p in pairs]
    else:
        fqs = [[0.0, 0.0] for _ in pairs]
    row = lax.broadcasted_iota(jnp.int32, (t, t), 0)
    col = lax.broadcasted_iota(jnp.int32, (t, t), 1)

    def tile(j, diag):
        ks = pl.ds(pl.multiple_of(j * t, t), t)
        scores = [_dot_nt(qs[p][h], kbf[ks, sl[p]]) + (fqs[p][h] - f_ref[0, 0, 2 * p + h:2 * p + h + 1, ks] * LOG2E)
                  for p in pairs for h in range(2)]
        if diag:
            scores = [jnp.where(row >= col, s, NEG_INF) for s in scores]
        softmax(scores, m_ref, acc_ref, [vb[ks, sl[p]] for p in pairs for vb in (vbf0, vbf1)], first=diag)

    _causal_tiles(qi, tile)
    for p in pairs:
        acc0, acc1 = acc_ref[2 * p], acc_ref[2 * p + 1]
        lo = _lane_lo(acc0.shape)
        denom = jnp.where(lo, pltpu.roll(acc0, HEAD_DIM, 1), pltpu.roll(acc1, HEAD_DIM, 1))
        o = jnp.where(lo, acc0, acc1) / denom
        o_ref[0, :, sl[p]] = (_seg_rms(o, og_ref[0, :, sl[p]], mxu=True) * _silu(g_ref[0, :, sl[p]])).astype(o_ref.dtype)


def _attn_b_kernel(q_ref, k_ref, v_ref, g_ref, u_ref, og_ref, o_ref, kbf, vbf, c_ref, acc_ref):
    qi = pl.program_id(2)
    t = q_ref.shape[1]
    blk = u_ref.shape[1]
    pairs = range(q_ref.shape[2] // LANES)
    sl = [slice(p * LANES, (p + 1) * LANES) for p in pairs]

    @pl.when(qi == 0)
    def _():
        def stage(rows, n):
            kbf[rows, :] = k_ref[0, rows, :].astype(BF16)
            vbf[rows, :] = v_ref[0, rows, :].astype(BF16)

        _for_row_blocks(k_ref.shape[1], stage)

    qs = [q for p in pairs for q in _head_masked(q_ref[0, :, sl[p]] * (ATT_SCALE * LOG2E))]
    row = lax.broadcasted_iota(jnp.int32, (t, t), 0)
    col = lax.broadcasted_iota(jnp.int32, (t, t), 1)
    umat = u_ref[...]

    def tile(j, diag):
        ks = pl.ds(pl.multiple_of(j * t, t), t)
        zs = [_dot_nt(q, kbf[ks, sl[i // 2]]) for i, q in enumerate(qs)]
        lbs = [jnp.minimum(z, 0.0) - jnp.log2(1.0 + jnp.exp2(-jnp.abs(z))) for z in zs]
        l1s = [lb - z for lb, z in zip(lbs, zs)]
        if diag:
            l1s = [jnp.where(row > col, l1, 0.0) for l1 in l1s]
        tails = [jnp.zeros((t, 1), F32) if diag else c_ref[i] for i in range(len(qs))]
        afters = [[] for _ in qs]
        for b in reversed(range(t // blk)):
            for i, l1 in enumerate(l1s):
                x = l1[:, b * blk:(b + 1) * blk]
                hi = x.astype(BF16)
                lo = (x - hi.astype(F32)).astype(BF16)
                afters[i].insert(0, _dot(jnp.concatenate([hi, lo], axis=1), umat) + tails[i])
                tails[i] = tails[i] + jnp.sum(x, axis=-1, keepdims=True)
        for i, (lb, after, c_new) in enumerate(zip(lbs, afters, tails)):
            a = jnp.exp2(lb + (after[0] if len(after) == 1 else jnp.concatenate(after, axis=1)))
            if diag:
                a = jnp.where(row > col, a, 0.0)
            av = _dot(a.astype(BF16), vbf[ks, sl[i // 2]])
            acc_ref[i] = av if diag else acc_ref[i] + av
            c_ref[i] = c_new
        return functools.reduce(jnp.maximum, [jnp.max(c) for c in tails])

    def alive(state):
        return jnp.logical_and(state[0] < qi, state[1] > STICK_DEAD_LOG2)

    lax.while_loop(alive, lambda st: (st[0] + 1, tile(qi - 1 - st[0], False)), (jnp.int32(0), tile(qi, True)))
    for p in pairs:
        o = jnp.where(_lane_lo((t, LANES)), acc_ref[2 * p], acc_ref[2 * p + 1])
        o_ref[0, :, sl[p]] = (_seg_rms(o, og_ref[0, :, sl[p]]) * _silu(g_ref[0, :, sl[p]])).astype(o_ref.dtype)


def _attn_specs(seq, t, col, pp):
    w = pp * LANES
    q_spec = pl.BlockSpec((1, t, w), lambda b, p, i: (b, i, col // pp + p))
    k_spec = pl.BlockSpec((1, seq, w), lambda b, p, i: (b, 0, (col + N_PAIRS) // pp + p))
    v_spec = pl.BlockSpec((1, seq, w), lambda b, p, i: (b, 0, (col + 2 * N_PAIRS) // pp + p))
    g_spec = pl.BlockSpec((1, t, w), lambda b, p, i: (b, i, (col + 3 * N_PAIRS) // pp + p))
    return [q_spec, k_spec, v_spec, g_spec]


def _pair_spec(pp):
    return pl.BlockSpec((1, 1, pp * LANES), lambda b, p, i: (p, 0, 0))


def _const_spec(shape):
    return pl.BlockSpec(shape, lambda b, p, i: (0,) * len(shape))


def _attn_call(body, proj, col, extra_specs, extra_args, pp, acc_lanes, tile=ATT_TILE, v_lanes=(LANES,)):
    bsz, seq, _ = proj.shape
    t = min(tile, seq)
    w = pp * LANES
    return pl.pallas_call(
        body,
        grid=(bsz, N_PAIRS // pp, seq // t),
        in_specs=_attn_specs(seq, t, col, pp) + extra_specs,
        out_specs=pl.BlockSpec((1, t, w), lambda b, p, i: (b, i, p)),
        out_shape=jax.ShapeDtypeStruct((bsz, seq, GROUP_W), BF16),
        scratch_shapes=[pltpu.VMEM((seq, w), BF16)] + [pltpu.VMEM((seq, pp * n), BF16) for n in v_lanes]
        + [pltpu.VMEM((2 * pp, t, 1), F32), pltpu.VMEM((2 * pp, t, acc_lanes), F32)],
        compiler_params=_cparams(("arbitrary", "arbitrary", "arbitrary")),
    )(proj, proj, proj, proj, *extra_args)


def _pair_gain(g64):
    return jnp.tile(g64.astype(F32), 2).reshape(1, LANES)


def _score_bound_log2(q_gain, k_gain):
    return (HEAD_DIM * ATT_SCALE * LOG2E * 1.02) * jnp.max(jnp.abs(q_gain)) * jnp.max(jnp.abs(k_gain))


def _attn_a(proj, tz, q_gain, k_gain, o_gain, dlam, lam_init, bounded):
    t = tz.shape[-1]
    pp = ATT_PAIRS
    specs = [pl.BlockSpec((pp, 3, t, t), lambda b, p, i: (p, 0, 0, 0)),
             _const_spec((1, LANES)), _const_spec((1, LANES)), _pair_spec(pp), _const_spec((4, HEAD_DIM))]
    args = (tz, _pair_gain(q_gain), _pair_gain(k_gain),
            (o_gain * (1.0 - lam_init)).reshape(N_PAIRS // pp, 1, pp * LANES), dlam)
    return _attn_call(functools.partial(_attn_a_kernel, lam_init=lam_init, bounded=bounded), proj, COL_A, specs,
                      args, pp=pp, v_lanes=(2 * LANES,), acc_lanes=2 * LANES)


def _attn_b(proj, o_gain):
    t = min(STICK_BLOCK, STICK_TILE, proj.shape[1])
    umat = (lax.broadcasted_iota(jnp.int32, (t, t), 0) > lax.broadcasted_iota(jnp.int32, (t, t), 1)).astype(BF16)
    umat = jnp.concatenate([umat, umat], axis=0)
    pp = ATT_PAIRS
    specs = [_const_spec((2 * t, t)), _pair_spec(pp)]
    return _attn_call(_attn_b_kernel, proj, COL_B, specs, (umat, o_gain.reshape(N_PAIRS // pp, 1, pp * LANES)),
                      pp=pp, tile=STICK_TILE, acc_lanes=LANES)


def _attn_c(proj, cum_rows, cum_cols, bound, q_gain, k_gain, o_gain, bounded):
    seq = proj.shape[1]
    t = min(ATT_TILE, seq)
    pp = ATT_PAIRS
    specs = [pl.BlockSpec((1, 1, 2 * pp, seq), lambda b, p, i: (b, p, 0, 0)),
             pl.BlockSpec((1, pp, t, LANES), lambda b, p, i: (b, p, i, 0)),
             _const_spec((1, LANES)), _const_spec((1, LANES)), _const_spec((1, LANES)), _pair_spec(pp)]
    args = (cum_rows.reshape(cum_rows.shape[0], N_PAIRS // pp, 2 * pp, seq), cum_cols,
            jnp.full((1, LANES), bound, F32), _pair_gain(q_gain), _pair_gain(k_gain),
            o_gain.reshape(N_PAIRS // pp, 1, pp * LANES))
    return _attn_call(functools.partial(_attn_c_kernel, bounded=bounded), proj, COL_C, specs, args,
                      pp=pp, v_lanes=(LANES, LANES), acc_lanes=LANES)


def _fox_cum_kernel(s_ref, fb_ref, tri_ref, rows_ref, cols_ref):
    seq = s_ref.shape[1]
    rows_per = tri_ref.shape[0]
    carry = jnp.zeros((1, LANES), F32)
    for c in range(seq // rows_per):
        rows = slice(c * rows_per, (c + 1) * rows_per)
        log_f = _log_sigmoid(s_ref[0, rows, :] + fb_ref[...])
        cum = _mm((tri_ref[...],), _pieces(log_f, 3)) + carry
        carry = cum[rows_per - 1:, :]
        rows_ref[0, :, rows] = cum.T[:8, :]
        lo = _lane_lo(cum.shape)
        for p in range(N_PAIRS):
            cols_ref[0, p, rows, :] = jnp.where(lo, cum[:, 2 * p:2 * p + 1], cum[:, 2 * p + 1:2 * p + 2])


def _fox_cum(proj, forget_bias):
    bsz, seq, _ = proj.shape
    rows_per = min(FOX_ROWS, seq)
    tri = (lax.broadcasted_iota(jnp.int32, (rows_per, rows_per), 0)
           >= lax.broadcasted_iota(jnp.int32, (rows_per, rows_per), 1)).astype(BF16)
    fb = jnp.pad(forget_bias.astype(F32), (0, LANES - forget_bias.shape[0])).reshape(1, LANES)
    cum_rows, cum_cols = pl.pallas_call(
        _fox_cum_kernel,
        grid=(bsz,),
        in_specs=[pl.BlockSpec((1, seq, LANES), lambda b: (b, 0, COL_SMALL)),
                  pl.BlockSpec((1, LANES), lambda b: (0, 0)),
                  pl.BlockSpec((rows_per, rows_per), lambda b: (0, 0))],
        out_specs=[pl.BlockSpec((1, 8, seq), lambda b: (b, 0, 0)),
                   pl.BlockSpec((1, N_PAIRS, seq, LANES), lambda b: (b, 0, 0, 0))],
        out_shape=[jax.ShapeDtypeStruct((bsz, 8, seq), F32),
                   jax.ShapeDtypeStruct((bsz, N_PAIRS, seq, LANES), F32)],
        compiler_params=_cparams(("arbitrary",)),
    )(proj, fb, tri)
    return cum_rows.reshape(bsz, N_PAIRS, 2, seq), cum_cols


def _token_shift(x, prev_row, mu):
    first = lax.broadcasted_iota(jnp.int32, x.shape, 0) == 0
    x_prev = jnp.where(first, prev_row, pltpu.roll(x, 1, 0))
    return x + (x_prev - x) * mu


def _rwkv_prep_kernel(s_ref, mus_ref, wup_ref, w0_ref, aup_ref, a0_ref, lw_o, a_o, prev_s):
    @pl.when(pl.program_id(1) == 0)
    def _():
        prev_s[...] = jnp.zeros_like(prev_s)

    raw = s_ref[0]
    sm = _token_shift(raw, prev_s[...], mus_ref[...])
    prev_s[...] = raw[raw.shape[0] - 1:, :]
    w_pre = w0_ref[...] + _mm(_pieces(jnp.tanh(sm), 2), _pieces(wup_ref[...], 2))
    lw_o[0] = (-math.exp(-0.5)) / (1.0 + jnp.exp(-w_pre))
    a_o[0] = 1.0 / (1.0 + jnp.exp(-(a0_ref[...] + _mm(_pieces(sm, 2), _pieces(aup_ref[...], 2)))))


def _rwkv_prep(proj, mu, w_up, w0, a_up, a0):
    bsz, seq, _ = proj.shape
    ts = min(PREP_ROWS, seq)
    mu = mu.astype(F32)
    zeros = lambda n: jnp.zeros((n,), F32)
    mu_s = jnp.concatenate([zeros(SMALL_WLO), mu[3 * GROUP_W:3 * GROUP_W + RWKV_RANK],
                            mu[3 * GROUP_W + RWKV_RANK:], zeros(LANES - SMALL_ALO - RWKV_RANK)]).reshape(1, LANES)
    wup = jnp.pad(w_up.astype(F32), ((SMALL_WLO, LANES - SMALL_WLO - RWKV_RANK), (0, 0)))
    aup = jnp.pad(a_up.astype(F32), ((SMALL_ALO, LANES - SMALL_ALO - RWKV_RANK), (0, 0)))
    vec = lambda a: a.astype(F32).reshape(1, GROUP_W)
    full = lambda shape: pl.BlockSpec(shape, lambda b, i: (0,) * len(shape))
    out_spec = pl.BlockSpec((1, ts, GROUP_W), lambda b, i: (b, i, 0))
    out_sds = jax.ShapeDtypeStruct((bsz, seq, GROUP_W), F32)
    return pl.pallas_call(
        _rwkv_prep_kernel,
        grid=(bsz, seq // ts),
        in_specs=[pl.BlockSpec((1, ts, LANES), lambda b, i: (b, i, COL_SMALL)),
                  full((1, LANES)), full((LANES, GROUP_W)), full((1, GROUP_W)),
                  full((LANES, GROUP_W)), full((1, GROUP_W))],
        out_specs=[out_spec] * 2,
        out_shape=[out_sds] * 2,
        scratch_shapes=[pltpu.VMEM((1, LANES), F32)],
        compiler_params=_cparams(("arbitrary", "arbitrary")),
    )(proj, mu_s, wup, vec(w0), aup, vec(a0))


def _pieces(x, n):
    out = []
    for i in range(n):
        p = x.astype(BF16)
        out.append(p)
        if i + 1 < n:
            x = x - p.astype(F32)
    return tuple(out)


def _mm(a, b, nt=False):
    n = max(len(a), len(b))
    dot = _dot_nt if nt else _dot
    acc = None
    for i, ai in enumerate(a):
        for j, bj in enumerate(b):
            if i + j < n:
                term = dot(ai, bj)
                acc = term if acc is None else acc + term
    return acc


def _neumann_inverses(mats):
    n = mats[0].shape[0]
    eye = (lax.broadcasted_iota(jnp.int32, (n, n), 0) == lax.broadcasted_iota(jnp.int32, (n, n), 1)).astype(F32)
    ps = [eye + a for a in mats]
    xs = [_pieces(a, RWKV_PIECES) for a in mats]
    for _ in range(int(math.log2(n)) - 1):
        xs = [_pieces(_mm(x, x), RWKV_PIECES) for x in xs]
        ps = [p + _mm(_pieces(p, RWKV_PIECES), x) for p, x in zip(ps, xs)]
    return ps


def _rwkv_scan_kernel(u_ref, lw_ref, a_ref, g_ref, mu_ref, kkr_ref, lng_ref, lnb_ref, tri_ref, o_ref, state, prev_u):
    c = u_ref.shape[1]
    n1, ng = RWKV_PIECES, RWKV_PIECES_G

    @pl.when(pl.program_id(1) == 0)
    def _():
        state[...] = jnp.zeros_like(state)
        prev_u[...] = jnp.zeros_like(prev_u)

    row = lax.broadcasted_iota(jnp.int32, (c, c), 0)
    col = lax.broadcasted_iota(jnp.int32, (c, c), 1)
    strict, incl = row > col, row >= col
    lo = _lane_lo((c, LANES))
    lo2 = _lane_lo((2 * c, LANES))
    diag_blocks = ((lax.broadcasted_iota(jnp.int32, (LANES, LANES), 0) < HEAD_DIM)
                   == (lax.broadcasted_iota(jnp.int32, (LANES, LANES), 1) < HEAD_DIM))
    tri = (tri_ref[...],)
    units = [(bi, p) for bi in range(u_ref.shape[0]) for p in range(N_PAIRS)]
    lanes = [slice(p * LANES, (p + 1) * LANES) for _, p in units]
    nu = len(units)

    st = []
    for i, (bi, p) in enumerate(units):
        r, k, v = (_token_shift(u_ref[bi, :, sec], prev_u[bi, :, sec], mu_ref[:, sec])
                   for sec in (slice(j * GROUP_W + p * LANES, j * GROUP_W + (p + 1) * LANES) for j in range(3)))
        lw, a = lw_ref[bi, :, lanes[i]], a_ref[bi, :, lanes[i]]
        kk = k * kkr_ref[0:1, lanes[i]]
        kk = kk / jnp.maximum(jnp.sqrt(_seg_sum(kk * kk)), 1e-12)
        k = k * (1.0 + (a - 1.0) * kkr_ref[1:2, lanes[i]])
        aa, b = -kk, kk * a
        cl = _mm(tri, _pieces(lw, 3))
        cl_end = cl[c - 1:c, :]
        cl_mid = cl[c // 2 - 1:c // 2, :]
        w_inv = jnp.exp(cl_mid - cl)
        w_rest = jnp.exp(cl_end - cl)
        qa = jnp.concatenate([aa * jnp.exp(cl - lw - cl_mid), r * jnp.exp(cl - cl_mid)], axis=0)
        bk = _pieces(jnp.concatenate([b * w_inv, k * w_inv], axis=0), ng)
        qa_s = jnp.concatenate([aa * jnp.exp(cl - lw), r * jnp.exp(cl)], axis=0)
        bk_end = _pieces(jnp.concatenate([b * w_rest, k * w_rest], axis=0), n1)
        st.append(dict(r=r, k=k, v=v, qa=qa, bk=bk, qa_s=qa_s, bk_end=bk_end, w_end=jnp.exp(cl_end),
                       v_p=_pieces(v, n1)))

    bases = [_mm(_pieces(st[i]["qa_s"], n1), _pieces(state[i], n1), nt=True) for i in range(nu)]
    a_ab, a_ak, a_r = [], [], []
    for i in range(nu):
        for h in range(2):
            qa_h = jnp.where(lo2 == (h == 0), st[i]["qa"], 0.0)
            g_a = _mm(_pieces(qa_h[:c], n1), st[i]["bk"][:n1], nt=True)
            g_r = _mm(_pieces(qa_h[c:], ng), st[i]["bk"], nt=True)
            a_ab.append(jnp.where(strict, g_a[:, :c], 0.0))
            a_ak.append(jnp.where(strict, g_a[:, c:], 0.0))
            a_r.append(jnp.concatenate([jnp.where(incl, g_r[:, :c], 0.0), jnp.where(incl, g_r[:, c:], 0.0)], axis=1))
    rhs = [bases[j // 2][:c] + _mm(_pieces(a_ak[j], n1), st[j // 2]["v_p"]) for j in range(2 * nu)]

    t_inv = _neumann_inverses(a_ab)
    u_heads = [_mm(_pieces(t_inv[j], n1), _pieces(rhs[j], n1)) for j in range(2 * nu)]
    for i, (bi, _) in enumerate(units):
        r, k, v = st[i]["r"], st[i]["k"], st[i]["v"]
        u = jnp.where(lo, u_heads[2 * i], u_heads[2 * i + 1])
        uv = _pieces(jnp.concatenate([u, v], axis=0), n1)
        y = bases[i][c:] + jnp.where(lo, _mm(_pieces(a_r[2 * i], n1), uv), _mm(_pieces(a_r[2 * i + 1], n1), uv))
        upd = _mm(_pieces(jnp.concatenate([u.T, v.T], axis=1), n1), st[i]["bk_end"])
        state[i] = state[i] * st[i]["w_end"] + jnp.where(diag_blocks, upd, 0.0)

        mean = _seg_sum(y) * (1.0 / HEAD_DIM)
        yc = y - mean
        var = _seg_sum(yc * yc) * (1.0 / HEAD_DIM)
        out = yc * lax.rsqrt(var + RWKV_LN_EPS) * lng_ref[:, lanes[i]] + lnb_ref[:, lanes[i]]
        out = out + _seg_sum(r * k * kkr_ref[2:3, lanes[i]]) * v
        o_ref[bi, :, lanes[i]] = (out * _silu(g_ref[bi, :, lanes[i]])).astype(o_ref.dtype)
    for bi in range(u_ref.shape[0]):
        prev_u[bi] = u_ref[bi, c - 1:, :]


def _rwkv_scan(proj, log_decay, iclr, mu, kkr, ln_gain, ln_bias):
    bsz, seq, _ = proj.shape
    c = min(RWKV_CHUNK, seq)
    tri = (lax.broadcasted_iota(jnp.int32, (c, c), 0) >= lax.broadcasted_iota(jnp.int32, (c, c), 1)).astype(BF16)
    nb = RWKV_ROWS if bsz % RWKV_ROWS == 0 else 1
    tok_spec = pl.BlockSpec((nb, c, GROUP_W), lambda b, i: (b, i, 0))
    full = lambda shape: pl.BlockSpec(shape, lambda b, i: (0,) * len(shape))
    vec = lambda a: a.astype(F32).reshape(1, GROUP_W)
    return pl.pallas_call(
        _rwkv_scan_kernel,
        grid=(bsz // nb, seq // c),
        in_specs=[pl.BlockSpec((nb, c, 3 * GROUP_W), lambda b, i: (b, i, COL_D * LANES // (3 * GROUP_W))),
                  tok_spec, tok_spec,
                  pl.BlockSpec((nb, c, GROUP_W), lambda b, i: (b, i, COL_DG * LANES // GROUP_W)),
                  full((1, 3 * GROUP_W)), full((3, GROUP_W)), full((1, GROUP_W)), full((1, GROUP_W)), full((c, c))],
        out_specs=tok_spec,
        out_shape=jax.ShapeDtypeStruct((bsz, seq, GROUP_W), BF16),
        scratch_shapes=[pltpu.VMEM((nb * N_PAIRS, LANES, LANES), F32), pltpu.VMEM((nb, 1, 3 * GROUP_W), F32)],
        compiler_params=_cparams(("arbitrary", "arbitrary")),
    )(proj, log_decay, iclr, proj, mu.astype(F32)[:3 * GROUP_W].reshape(1, 3 * GROUP_W), kkr.astype(F32),
      vec(ln_gain), vec(ln_bias), tri)


def _out_kernel(a_ref, b_ref, c_ref, d_ref, w_ref, x_ref, o_ref):
    acc = x_ref[...]
    for i, m_ref in enumerate((a_ref, b_ref, c_ref, d_ref)):
        acc = acc + _dot(m_ref[...], w_ref[i * GROUP_W:(i + 1) * GROUP_W, :])
    o_ref[...] = acc


def _out_proj(groups, w_out, x2d):
    m, d = x2d.shape
    tm = min(512, m)
    g_spec = pl.BlockSpec((tm, GROUP_W), lambda i: (i, 0))
    return pl.pallas_call(
        _out_kernel,
        grid=(m // tm,),
        in_specs=[g_spec] * 4 + [pl.BlockSpec(w_out.shape, lambda i: (0, 0)),
                                 pl.BlockSpec((tm, d), lambda i: (i, 0))],
        out_specs=pl.BlockSpec((tm, d), lambda i: (i, 0)),
        out_shape=jax.ShapeDtypeStruct((m, d), F32),
        compiler_params=_cparams(("arbitrary",)),
    )(*[g.reshape(m, GROUP_W) for g in groups], w_out, x2d)


def _t5_causal_bucket(dist):
    max_exact = NUM_BUCKETS // 2
    d = jnp.maximum(dist, 1).astype(F32)
    large = max_exact + (jnp.log(d / max_exact) / math.log(MAX_DISTANCE / max_exact)
                         * (NUM_BUCKETS - max_exact)).astype(jnp.int32)
    large = jnp.minimum(large, NUM_BUCKETS - 1)
    return jnp.where(dist < max_exact, dist, large)


def _bias_tiles(rel_bias, seq, t):
    assert t >= MAX_DISTANCE
    bias_by_dist = rel_bias.astype(F32)[_t5_causal_bucket(jnp.arange(seq))] * LOG2E
    m = np.arange(2 * t)
    tiles = []
    for delta in range(3):
        dist = np.where(m < t, delta * t - m, delta * t + 2 * t - m)
        v = bias_by_dist[np.clip(dist, 0, seq - 1)].T
        skew = jnp.tile(v, (1, t))[:, :t * (2 * t - 1)].reshape(-1, t, 2 * t - 1)
        tiles.append(skew[:, :, :t])
    return jnp.stack(tiles, axis=1)


def _pack_w_in(w):
    d = w.shape[0]
    att = w[:, :12 * GROUP_W]
    cf = w[:, 12 * GROUP_W:12 * GROUP_W + 8]
    off = 12 * GROUP_W + 8
    rkv = w[:, off:off + 3 * GROUP_W]
    w_lo = w[:, off + 3 * GROUP_W:off + 3 * GROUP_W + RWKV_RANK]
    a_lo = w[:, off + 3 * GROUP_W + RWKV_RANK:off + 3 * GROUP_W + 2 * RWKV_RANK]
    dg = w[:, off + 3 * GROUP_W + 2 * RWKV_RANK:]
    zeros = lambda n: jnp.zeros((d, n), w.dtype)
    small = [cf, zeros(SMALL_WLO - cf.shape[1]), w_lo, a_lo, zeros(LANES - SMALL_ALO - RWKV_RANK)]
    return jnp.concatenate([att, rkv, dg] + small + [zeros(LANES)], axis=1).astype(BF16)


def kernel(x, norm_gain, w_in, w_out, rel_bias, qk_gain, diff_lambda, forget_bias, out_gain, rwkv_mu, rwkv_w_up,
           rwkv_w0, rwkv_a_up, rwkv_a0, rwkv_kkr, rwkv_ln_gain, rwkv_ln_bias):
    bsz, seq, d = x.shape
    tz = _bias_tiles(rel_bias, seq, min(ATT_TILE, seq))
    bias_max = jnp.max(jnp.abs(rel_bias.astype(F32))) * LOG2E
    h = x.reshape(bsz * seq, d)
    for l in range(DEPTH):
        og = out_gain[l].astype(F32)
        proj = _proj(h, norm_gain[l].astype(F32), _pack_w_in(w_in[l])).reshape(bsz, seq, N_PROJ)
        lam_init = 0.8 - 0.6 * math.exp(-0.3 * l)
        qg = qk_gain[l].astype(F32)
        bound_a = _score_bound_log2(qg[0], qg[1]) + bias_max
        bound_c = _score_bound_log2(qg[2], qg[3])
        att_a = functools.partial(_attn_a, proj, q_gain=qg[0], k_gain=qg[1], o_gain=og[:GROUP_W],
                                  dlam=diff_lambda[l].astype(F32), lam_init=lam_init)
        oa = lax.cond(bound_a <= MAX_SHIFT_LOG2, lambda: att_a(tz=tz - bound_a, bounded=True),
                      lambda: att_a(tz=tz, bounded=False))
        ob = _attn_b(proj, og[GROUP_W:2 * GROUP_W])
        att_c = functools.partial(_attn_c, proj, *_fox_cum(proj, forget_bias[l]), bound_c, qg[2], qg[3],
                                  og[2 * GROUP_W:3 * GROUP_W])
        oc = lax.cond(bound_c <= MAX_SHIFT_LOG2, lambda: att_c(bounded=True), lambda: att_c(bounded=False))
        log_decay, iclr = _rwkv_prep(proj, rwkv_mu[l], rwkv_w_up[l], rwkv_w0[l], rwkv_a_up[l], rwkv_a0[l])
        od = _rwkv_scan(proj, log_decay, iclr, rwkv_mu[l], rwkv_kkr[l],
                        rwkv_ln_gain[l].astype(F32) * og[3 * GROUP_W:], rwkv_ln_bias[l])
        h = _out_proj((oa, ob, oc, od), w_out[l].astype(BF16), h)
    return h.reshape(bsz, seq, d)
```
